```python
import math
import jax
import jax.numpy as jnp
from jax import lax
import numpy as np


D_MODEL = 1024
BATCH = 8
SEQ = 2048
DEPTH = 2

GRID_W = 64
CTX_LEN = 256
N_MOD = 9
SC_WIDTH = 512
SC_KSIZE = 3
ML_HEADS = 4
ML_DQK = 128
ML_DV = 256
ML_CHUNK = 64
ML_QK_W = ML_HEADS * ML_DQK
ML_V_W = ML_HEADS * ML_DV
DA_HEADS = 4
DA_HALF = 64
DA_DV = 2 * DA_HALF
DA_QK_W = DA_HEADS * 2 * DA_HALF
DA_V_W = DA_HEADS * DA_DV
Q_BLOCK = 128
ROPE_BASE = 10000.0
FFN_HIDDEN = 2816
ALPHA = (2 * DEPTH) ** 0.25
BETA = (8 * DEPTH) ** -0.25
LN_EPS = 1e-5

kernel_name = 'hybrid_conv_mlstm_diffattn_prefix_block'


def mixer_split_sizes():
    return (SC_WIDTH, SC_WIDTH, SC_WIDTH,
            ML_QK_W, ML_QK_W, ML_V_W, ML_V_W, 4 * ML_HEADS,
            DA_QK_W, DA_QK_W, DA_V_W,
            D_MODEL, D_MODEL, D_MODEL)


def split_cols(z):
    idx = np.cumsum(mixer_split_sizes())[:-1].tolist()
    return jnp.split(z, idx, axis=-1)


def layer_norm(x, g, b):
    xf = x.astype(jnp.float32)
    mu = jnp.mean(xf, axis=-1, keepdims=True)
    var = jnp.mean(jnp.square(xf - mu), axis=-1, keepdims=True)
    return ((xf - mu) * lax.rsqrt(var + LN_EPS) * g + b).astype(x.dtype)


def rms_norm(x, g):
    xf = x.astype(jnp.float32)
    return (xf * lax.rsqrt(jnp.mean(jnp.square(xf), axis=-1, keepdims=True) + LN_EPS) * g).astype(x.dtype)


def post_norm(h, delta, g, b):
    return layer_norm(ALPHA * h + delta, g, b)


def modulate(h, shift, scale):
    return h * (1.0 + scale) + shift


def swiglu(u, w_up, w_down):
    a, v = jnp.split(u @ w_up, 2, axis=-1)
    return (jax.nn.silu(a) * v) @ w_down


def centred_depthwise_conv(z, w):
    k, ch = w.shape
    return lax.conv_general_dilated(z, w[:, None, :].astype(z.dtype), window_strides=(1,),
                                    padding=[(k // 2, k // 2)],
                                    dimension_numbers=('NWC', 'WIO', 'NWC'),
                                    feature_group_count=ch)


def axial_rope_tables(row_ids, col_ids, rot_dim):
    n_freq = rot_dim // 4
    inv = ROPE_BASE ** (-jnp.arange(n_freq, dtype=jnp.float32) / n_freq)
    ang = jnp.concatenate([row_ids[:, None] * inv, col_ids[:, None] * inv], axis=-1)
    return jnp.cos(ang), jnp.sin(ang)


def apply_rope(x, cos, sin):
    x1, x2 = jnp.split(x.astype(jnp.float32), 2, axis=-1)
    return jnp.concatenate([x1 * cos - x2 * sin, x1 * sin + x2 * cos], axis=-1).astype(x.dtype)


def to_heads(a, n_heads):
    b, t, _ = a.shape
    return a.reshape(b, t, n_heads, -1).transpose(0, 2, 1, 3)


def from_heads(a):
    b, h, t, d = a.shape
    return a.transpose(0, 2, 1, 3).reshape(b, t, h * d)


def flip_t(a):
    return jnp.flip(a, axis=2)


def mlstm_chunk_scan(q, k, v, log_i, log_f, state):
    b, h, t, _ = q.shape
    dv = v.shape[-1]
    n_chunks = t // ML_CHUNK

    def chunks(a):
        return jnp.moveaxis(a.reshape(b, h, n_chunks, ML_CHUNK, *a.shape[3:]), 2, 0)

    lower = jnp.tril(jnp.ones((ML_CHUNK, ML_CHUNK), dtype=bool))

    def step(carry, xs):
        C, n, m = carry
        qc, kc, vc, ic, fc = xs
        bcum = jnp.cumsum(fc, axis=-1)
        d_log = jnp.where(lower, bcum[..., :, None] - bcum[..., None, :] + ic[..., None, :], -jnp.inf)
        m_inter = bcum + m[..., None]
        m_t = jnp.maximum(m_inter, jnp.max(d_log, axis=-1))
        w_inter = jnp.exp(m_inter - m_t)
        s = jnp.einsum('bhtd,bhsd->bhts', qc, kc).astype(jnp.float32) * jnp.exp(d_log - m_t[..., None])
        num = w_inter[..., None] * jnp.einsum('bhtd,bhde->bhte', qc, C) + jnp.einsum('bhts,bhse->bhte', s, vc)
        den = w_inter * jnp.einsum('bhtd,bhd->bht', qc, n) + jnp.sum(s, axis=-1)
        h_out = num / jnp.maximum(jnp.abs(den), jnp.exp(-m_t))[..., None]
        b_end = bcum[..., -1]
        g_log = b_end[..., None] - bcum + ic
        m_new = jnp.maximum(b_end + m, jnp.max(g_log, axis=-1))
        a_prev = jnp.exp(b_end + m - m_new)
        kw = kc * jnp.exp(g_log - m_new[..., None])[..., None]
        C_new = a_prev[..., None, None] * C + jnp.einsum('bhsd,bhse->bhde', kw, vc)
        n_new = a_prev[..., None] * n + jnp.sum(kw, axis=2)
        return (C_new, n_new, m_new), h_out

    state, hs = lax.scan(step, state, (chunks(q), chunks(k), chunks(v), chunks(log_i), chunks(log_f)))
    return jnp.moveaxis(hs, 0, 2).reshape(b, h, t, dv), state


def mlstm_inputs(mq, mk, mv, mg):
    b, t, _ = mg.shape
    g = mg.astype(jnp.float32).reshape(b, t, 4, ML_HEADS).transpose(2, 0, 3, 1)
    gates = (g[0], g[1], jax.nn.log_sigmoid(g[2]), jax.nn.log_sigmoid(g[3]))
    return (to_heads(mq, ML_HEADS), to_heads(mk, ML_HEADS) * (ML_DQK ** -0.5), to_heads(mv, ML_HEADS), gates)


def bidir_mlstm(q, k, v, gates, state_fwd, state_bwd):
    i_f, i_b, lf_f, lf_b = gates
    h_f, st_f = mlstm_chunk_scan(q, k, v, i_f, lf_f, state_fwd)
    h_b, st_b = mlstm_chunk_scan(flip_t(q), flip_t(k), flip_t(v), flip_t(i_b), flip_t(lf_b), state_bwd)
    return h_f + flip_t(h_b), st_f, st_b


def da_heads(a):
    b, t, _ = a.shape
    return a.reshape(b, t, DA_HEADS, 2, DA_HALF).transpose(0, 2, 3, 1, 4)


def diff_attend(q, k, v, lam):
    s = jnp.einsum('bhjqd,bhjkd->bhjqk', q, k).astype(jnp.float32) * (DA_HALF ** -0.5)
    p = jax.nn.softmax(s, axis=-1)
    w = p[:, :, 0] - lam * p[:, :, 1]
    return jnp.einsum('bhqk,bhkd->bhqd', w.astype(v.dtype), v)


def diff_head_out(o, g, lam_init):
    return from_heads(rms_norm(o, g) * (1.0 - lam_init))


def token_mixer(u, uc, cos, sin, w_in, b_in, conv_w, w_sc, w_ml, w_da, w_o, lam, lam_init, da_g, need_ctx):
    (sb, scg, sx, mq, mk, mv, mo, mg, dq, dk, dv, gs, gm, gd) = split_cols(u @ w_in + b_in)
    (sbc, scc, sxc, mqc, mkc, mvc, moc, mgc, dqc, dkc, dvc, gsc, gmc, gdc) = split_cols(uc @ w_in + b_in)

    y_sc = sb * centred_depthwise_conv(scg * sx, conv_w)

    ql, kl, vl, gl = mlstm_inputs(mq, mk, mv, mg)
    qcm, kcm, vcm, gcm = mlstm_inputs(mqc, mkc, mvc, mgc)
    bsz = ql.shape[0]
    zero = (jnp.zeros((bsz, ML_HEADS, ML_DQK, ML_DV), jnp.float32),
            jnp.zeros((bsz, ML_HEADS, ML_DQK), jnp.float32),
            jnp.zeros((bsz, ML_HEADS), jnp.float32))
    hc_ml, st_f, st_b = bidir_mlstm(qcm, kcm, vcm, gcm, zero, zero)
    hl_ml, _, _ = bidir_mlstm(ql, kl, vl, gl, st_f, st_b)
    y_ml = jax.nn.sigmoid(mo) * from_heads(hl_ml).astype(mo.dtype)

    q_l = apply_rope(da_heads(dq), cos, sin)
    k_l = apply_rope(da_heads(dk), cos, sin)
    v_l = to_heads(dv, DA_HEADS)
    k_c = da_heads(dkc)
    v_c = to_heads(dvc, DA_HEADS)
    k_all = jnp.concatenate([k_l, k_c], axis=3)
    v_all = jnp.concatenate([v_l, v_c], axis=2)
    _, nh, _, t, dh = q_l.shape
    nb = t // Q_BLOCK
    q_blocks = q_l.reshape(bsz, nh, 2, nb, Q_BLOCK, dh).transpose(3, 0, 1, 2, 4, 5)
    o_blocks = lax.map(lambda qb: diff_attend(qb, k_all, v_all, lam), q_blocks)
    o_l = o_blocks.transpose(1, 2, 0, 3, 4).reshape(bsz, nh, t, DA_DV)
    y_da = diff_head_out(o_l, da_g, lam_init)

    y = (jax.nn.sigmoid(gs) * (y_sc @ w_sc) + jax.nn.sigmoid(gm) * (y_ml @ w_ml)
         + jax.nn.sigmoid(gd) * (y_da @ w_da)) @ w_o
    if not need_ctx:
        return y, None

    yc_sc = sbc * centred_depthwise_conv(scc * sxc, conv_w)
    yc_ml = jax.nn.sigmoid(moc) * from_heads(hc_ml).astype(moc.dtype)
    yc_da = diff_head_out(diff_attend(da_heads(dqc), k_c, v_c, lam), da_g, lam_init)
    yc = (jax.nn.sigmoid(gsc) * (yc_sc @ w_sc) + jax.nn.sigmoid(gmc) * (yc_ml @ w_ml)
          + jax.nn.sigmoid(gdc) * (yc_da @ w_da)) @ w_o
    return y, yc


def setup_inputs(seed: int = 0) -> dict:
    key = jax.random.key(seed)
    ks = iter(jax.random.split(key, 32))

    def nrm(shape, s):
        return jax.random.normal(next(ks), shape, jnp.float32) * s

    n_in = sum(mixer_split_sizes())
    f_off = 3 * SC_WIDTH + 2 * ML_QK_W + 2 * ML_V_W + 2 * ML_HEADS
    forget_bias = jnp.tile(jnp.linspace(3.0, 6.0, ML_HEADS, dtype=jnp.float32), 2)
    return {
        'x': nrm((BATCH, SEQ, D_MODEL), 1.0),
        'c': nrm((BATCH, D_MODEL), 1.0),
        'ctx': nrm((BATCH, CTX_LEN, D_MODEL), 1.0),
        'c_ctx': nrm((D_MODEL,), 1.0),
        'w_ada': nrm((DEPTH, D_MODEL, N_MOD * D_MODEL), 0.5 * D_MODEL ** -0.5),
        'b_ada': nrm((DEPTH, N_MOD * D_MODEL), 0.02),
        'ln_g': 1.0 + nrm((DEPTH, 3, D_MODEL), 0.02),
        'ln_b': nrm((DEPTH, 3, D_MODEL), 0.02),
        'ffn1_up': nrm((DEPTH, D_MODEL, 2 * FFN_HIDDEN), D_MODEL ** -0.5),
        'ffn1_down': nrm((DEPTH, FFN_HIDDEN, D_MODEL), BETA * FFN_HIDDEN ** -0.5),
        'ffn2_up': nrm((DEPTH, D_MODEL, 2 * FFN_HIDDEN), D_MODEL ** -0.5),
        'ffn2_down': nrm((DEPTH, FFN_HIDDEN, D_MODEL), BETA * FFN_HIDDEN ** -0.5),
        'w_in': nrm((DEPTH, D_MODEL, n_in), D_MODEL ** -0.5),
        'b_in': nrm((DEPTH, n_in), 0.02).at[:, f_off:f_off + 2 * ML_HEADS].add(forget_bias),
        'conv_w': nrm((DEPTH, SC_KSIZE, SC_WIDTH), SC_KSIZE ** -0.5),
        'w_sc': nrm((DEPTH, SC_WIDTH, D_MODEL), BETA * SC_WIDTH ** -0.5),
        'w_ml': nrm((DEPTH, ML_V_W, D_MODEL), BETA * ML_V_W ** -0.5),
        'w_da': nrm((DEPTH, DA_V_W, D_MODEL), BETA * DA_V_W ** -0.5),
        'w_o': nrm((DEPTH, D_MODEL, D_MODEL), BETA * D_MODEL ** -0.5),
        'lam_q1': nrm((DEPTH, DA_HALF), 0.1),
        'lam_k1': nrm((DEPTH, DA_HALF), 0.1),
        'lam_q2': nrm((DEPTH, DA_HALF), 0.1),
        'lam_k2': nrm((DEPTH, DA_HALF), 0.1),
        'da_norm_g': 1.0 + nrm((DEPTH, DA_DV), 0.02),
    }


def reference(x, c, ctx, c_ctx, w_ada, b_ada, ln_g, ln_b, ffn1_up, ffn1_down, ffn2_up, ffn2_down,
              w_in, b_in, conv_w, w_sc, w_ml, w_da, w_o, lam_q1, lam_k1, lam_q2, lam_k2, da_norm_g):
    n_tok = x.shape[1]
    ROWS = n_tok // GRID_W
    row_ids = jnp.repeat(jnp.arange(ROWS, dtype=jnp.float32), GRID_W)
    col_ids = jnp.tile(jnp.arange(GRID_W, dtype=jnp.float32), ROWS)
    cos, sin = axial_rope_tables(row_ids, col_ids, DA_HALF)

    h, hc = x, ctx
    for l in range(DEPTH):
        last = l == DEPTH - 1
        mods = jnp.split((jax.nn.silu(c) @ w_ada[l] + b_ada[l])[:, None, :], N_MOD, axis=-1)
        mods_c = jnp.split(jax.nn.silu(c_ctx) @ w_ada[l] + b_ada[l], N_MOD, axis=-1)

        h = post_norm(h, 0.5 * mods[2] * swiglu(modulate(h, mods[0], mods[1]), ffn1_up[l], ffn1_down[l]),
                      ln_g[l, 0], ln_b[l, 0])
        hc = post_norm(hc, 0.5 * mods_c[2] * swiglu(modulate(hc, mods_c[0], mods_c[1]), ffn1_up[l], ffn1_down[l]),
                       ln_g[l, 0], ln_b[l, 0])

        lam_init = 0.8 - 0.6 * math.exp(-0.3 * l)
        lam = (jnp.exp(jnp.sum(lam_q1[l].astype(jnp.float32) * lam_k1[l].astype(jnp.float32)))
               - jnp.exp(jnp.sum(lam_q2[l].astype(jnp.float32) * lam_k2[l].astype(jnp.float32))) + lam_init)
        y, yc = token_mixer(modulate(h, mods[3], mods[4]), modulate(hc, mods_c[3], mods_c[4]), cos, sin,
                            w_in[l], b_in[l], conv_w[l], w_sc[l], w_ml[l], w_da[l], w_o[l],
                            lam, lam_init, da_norm_g[l], not last)
        h = post_norm(h, mods[5] * y, ln_g[l, 1], ln_b[l, 1])

        h = post_norm(h, 0.5 * mods[8] * swiglu(modulate(h, mods[6], mods[7]), ffn2_up[l], ffn2_down[l]),
                      ln_g[l, 2], ln_b[l, 2])
        if not last:
            hc = post_norm(hc, mods_c[5] * yc, ln_g[l, 1], ln_b[l, 1])
            hc = post_norm(hc, 0.5 * mods_c[8] * swiglu(modulate(hc, mods_c[6], mods_c[7]), ffn2_up[l], ffn2_down[l]),
                           ln_g[l, 2], ln_b[l, 2])
    return h
```

```python
import functools
import math

import jax
import jax.numpy as jnp
from jax import lax
from jax.experimental import pallas as pl
from jax.experimental.pallas import tpu as pltpu

GRID_W = 64
N_MOD = 9
SC_WIDTH = 512
SC_KSIZE = 3
ML_HEADS = 4
ML_DQK = 128
ML_DV = 256
DA_HEADS = 4
DA_HALF = 64
DA_DV = 2 * DA_HALF
ROPE_BASE = 10000.0
LN_EPS = 1e-5

ML_QK_W = ML_HEADS * ML_DQK
ML_V_W = ML_HEADS * ML_DV
DA_QK_W = DA_HEADS * 2 * DA_HALF
DA_V_W = DA_HEADS * DA_DV
N_GATES = 4 * ML_HEADS

ML_BLOCK = 256
FFN_COLS = 256
LANES = 128
BF16_SUBLANES = 16
V7X_VMEM_BYTES = 64 * 1024 * 1024
VMEM_LIMIT = V7X_VMEM_BYTES - 8 * 1024 * 1024

F32 = jnp.float32
BF16 = jnp.bfloat16
NEG_BIG = -1e30


def _dot(a, b, precision=None):
    return jnp.dot(a, b, preferred_element_type=F32, precision=precision)


def _dot_nt(a, b):
    return lax.dot_general(a, b, (((1,), (1,)), ((), ())), preferred_element_type=F32)


def _dot_tn(a, b):
    return lax.dot_general(a, b, (((0,), (0,)), ((), ())), preferred_element_type=F32)


def _layer_norm(y, g, b):
    mu = jnp.mean(y, axis=-1, keepdims=True)
    yc = y - mu
    var = jnp.mean(yc * yc, axis=-1, keepdims=True)
    return yc * lax.rsqrt(var + LN_EPS) * g + b


def _resident(shape):
    return pl.BlockSpec(shape, lambda *_: (0,) * len(shape), pipeline_mode=pl.Buffered(1))


def _params(n_axes):
    return pltpu.CompilerParams(dimension_semantics=("parallel",) * n_axes,
                                vmem_limit_bytes=VMEM_LIMIT)


def _ada_kernel(c_ref, w_ref, b_ref, o_ref):
    c = c_ref[...]
    a = (c * jax.nn.sigmoid(c)).astype(BF16)
    o_ref[0] = _dot(a, w_ref[0].astype(BF16)) + b_ref[0]


def _ada(cc, w_ada, b_ada):
    depth, d, n = w_ada.shape
    tn = d
    return pl.pallas_call(
        _ada_kernel,
        grid=(depth, n // tn),
        in_specs=[pl.BlockSpec(cc.shape, lambda l, j: (0, 0)),
                  pl.BlockSpec((1, d, tn), lambda l, j: (l, 0, j)),
                  pl.BlockSpec((1, 1, tn), lambda l, j: (l, 0, j))],
        out_specs=pl.BlockSpec((1, cc.shape[0], tn), lambda l, j: (l, 0, j)),
        out_shape=jax.ShapeDtypeStruct((depth, cc.shape[0], n), F32),
        compiler_params=_params(2),
        name="ada",
    )(cc, w_ada, b_ada.reshape(depth, 1, n))


def _ffn_kernel(h_ref, mod_ref, wup_ref, wdn_ref, lng_ref, lnb_ref, o_ref, g_scr, *, mod_base, alpha):
    f = wdn_ref.shape[0]
    h = h_ref[...]
    shift = mod_ref[0, mod_base:mod_base + 1, :]
    scale = mod_ref[0, mod_base + 1:mod_base + 2, :]
    gate = mod_ref[0, mod_base + 2:mod_base + 3, :]
    u = (h * (1.0 + scale) + shift).astype(BF16)
    for c in range(f // FFN_COLS):
        lo, hi = c * FFN_COLS, (c + 1) * FFN_COLS
        a = _dot(u, wup_ref[:, lo:hi])
        v = _dot(u, wup_ref[:, f + lo:f + hi])
        g_scr[:, lo:hi] = (a * jax.nn.sigmoid(a) * v).astype(BF16)
    d = _dot(g_scr[...], wdn_ref[...])
    o_ref[...] = _layer_norm(alpha * h + (0.5 * gate) * d, lng_ref[...], lnb_ref[...])


def _ffn(h, mods3, w_up, w_dn, ln_g, ln_b, *, mod_base, n_rows, seq, batch, alpha, tm=512):
    d = h.shape[1]
    f = w_dn.shape[0]
    tm = min(tm, seq)
    kernel = functools.partial(_ffn_kernel, mod_base=mod_base, alpha=alpha)
    return pl.pallas_call(
        kernel,
        grid=(n_rows // tm,),
        in_specs=[pl.BlockSpec((tm, d), lambda i: (i, 0)),
                  pl.BlockSpec((1, N_MOD, d), lambda i: (jnp.minimum(i * tm // seq, batch), 0, 0)),
                  _resident((d, 2 * f)),
                  _resident((f, d)),
                  pl.BlockSpec((1, d), lambda i: (0, 0)),
                  pl.BlockSpec((1, d), lambda i: (0, 0))],
        out_specs=pl.BlockSpec((tm, d), lambda i: (i, 0)),
        out_shape=jax.ShapeDtypeStruct((n_rows, d), F32),
        scratch_shapes=[pltpu.VMEM((tm, f), BF16)],
        compiler_params=_params(1),
        name="ffn",
    )(h, mods3, w_up, w_dn, ln_g.reshape(1, d), ln_b.reshape(1, d))


def _main_cols(d):
    sizes = (("sb", SC_WIDTH), ("sc", SC_WIDTH), ("sx", SC_WIDTH),
             ("mq", ML_QK_W), ("mk", ML_QK_W), ("mv", ML_V_W), ("mo", ML_V_W),
             ("dq", DA_QK_W), ("dk", DA_QK_W), ("dv", DA_V_W),
             ("gs", d), ("gm", d), ("gd", d))
    cols, off = {}, 0
    for name, w in sizes:
        cols[name] = (off, off + w)
        off += w
    return cols, off


def _mixer_in_kernel(h_ref, mod_ref, w_ref, b_ref, wg_ref, bg_ref, cos_ref, sin_ref,
                     sb_o, p_o, mq_o, mk_o, mv_o, smo_o, mg_o, dq_o, dk_o, dv_o, gs_o, gm_o, gd_o):
    d = h_ref.shape[1]
    cols, _ = _main_cols(d)
    h = h_ref[...]
    shift = mod_ref[0, 3:4, :]
    scale = mod_ref[0, 4:5, :]
    u = (h * (1.0 + scale) + shift).astype(BF16)

    def proj(name):
        lo, hi = cols[name]
        return _dot(u, w_ref[:, lo:hi]) + b_ref[:, lo:hi]

    sb_o[...] = proj("sb").astype(BF16)
    p_o[...] = (proj("sc") * proj("sx")).astype(BF16)
    mq_o[...] = proj("mq").astype(BF16)
    mk_o[...] = (proj("mk") * (ML_DQK ** -0.5)).astype(BF16)
    mv_o[...] = proj("mv").astype(BF16)
    smo_o[...] = jax.nn.sigmoid(proj("mo")).astype(BF16)
    mg_o[...] = (_dot(u, wg_ref[...]) + bg_ref[...])[:, :N_GATES]

    cos_t = cos_ref[...]
    sin_t = sin_ref[...]
    lane = lax.broadcasted_iota(jnp.int32, cos_t.shape, 1)
    first_half = (lane % DA_HALF) < (DA_HALF // 2)

    def rope_store(z, out_ref, mult):
        for k in range(z.shape[1] // LANES):
            x = z[:, k * LANES:(k + 1) * LANES]
            partner = jnp.where(first_half,
                                pltpu.roll(x, LANES - DA_HALF // 2, 1),
                                pltpu.roll(x, DA_HALF // 2, 1))
            out_ref[:, k * LANES:(k + 1) * LANES] = ((x * cos_t + partner * sin_t) * mult).astype(BF16)

    rope_store(proj("dq"), dq_o, DA_HALF ** -0.5)
    rope_store(proj("dk"), dk_o, 1.0)
    dv_o[...] = proj("dv").astype(BF16)
    gs_o[...] = jax.nn.sigmoid(proj("gs")).astype(BF16)
    gm_o[...] = jax.nn.sigmoid(proj("gm")).astype(BF16)
    gd_o[...] = jax.nn.sigmoid(proj("gd")).astype(BF16)


def _mixer_in(h, mods3, w_main, b_main, w_gate, b_gate, cos_t, sin_t, *, seq, batch, n_lat, tm=256):
    rows, d = h.shape
    _, n_main = _main_cols(d)
    tm = min(tm, seq)
    lat_tiles = n_lat // tm
    rope_blocks = seq // tm

    def row_spec(width):
        return pl.BlockSpec((tm, width), lambda i: (i, 0))

    def rope_map(i):
        return (jnp.where(i < lat_tiles, i % rope_blocks, rope_blocks), 0)

    widths = (SC_WIDTH, SC_WIDTH, ML_QK_W, ML_QK_W, ML_V_W, ML_V_W, N_GATES,
              DA_QK_W, DA_QK_W, DA_V_W, d, d, d)
    dtypes = (BF16,) * 6 + (F32,) + (BF16,) * 6
    return pl.pallas_call(
        _mixer_in_kernel,
        grid=(rows // tm,),
        in_specs=[row_spec(d),
                  pl.BlockSpec((1, N_MOD, d), lambda i: (jnp.minimum(i * tm // seq, batch), 0, 0)),
                  _resident((d, n_main)),
                  _resident((1, n_main)),
                  _resident((d, LANES)),
                  _resident((1, LANES)),
                  pl.BlockSpec((tm, LANES), rope_map),
                  pl.BlockSpec((tm, LANES), rope_map)],
        out_specs=[row_spec(w) for w in widths],
        out_shape=[jax.ShapeDtypeStruct((rows, w), dt) for w, dt in zip(widths, dtypes)],
        compiler_params=_params(1),
        name="mixer_in",
    )(h, mods3, w_main, b_main, w_gate, b_gate, cos_t, sin_t)


def _log_sigmoid(x):
    return jnp.minimum(x, 0.0) - jnp.log(1.0 + jnp.exp(-jnp.abs(x)))


def _mlstm_kernel(ql_ref, qc_ref, kl_ref, kc_ref, vl_ref, vc_ref, gcl_ref, gcc_ref, grl_ref, grc_ref,
                  *rest, need_ctx):
    if need_ctx:
        hl_ref, hc_ref, accl, accc = rest
    else:
        hl_ref, accl = rest
        hc_ref = accc = None
    blk = ML_BLOCK
    n_lat_blk = ql_ref.shape[0] // blk
    n_ctx_blk = qc_ref.shape[0] // blk
    head = pl.program_id(1)

    t_idx = lax.broadcasted_iota(jnp.int32, (blk, blk), 0)
    s_idx = lax.broadcasted_iota(jnp.int32, (blk, blk), 1)
    lower = s_idx <= t_idx
    upper = s_idx >= t_idx
    tril = lower.astype(F32)
    triu = upper.astype(F32)
    lane_g = lax.broadcasted_iota(jnp.int32, (blk, N_GATES), 1)
    sub_g = lax.broadcasted_iota(jnp.int32, (N_GATES, blk), 0)
    lane_t = lax.broadcasted_iota(jnp.int32, (1, blk), 1)

    def step(q, k, v, g_col, g_row, state, bwd, need_h):
        c_st, n_st, m_st = state
        mask = upper if bwd else lower
        i_idx = head + (ML_HEADS if bwd else 0)
        f_idx = head + (3 * ML_HEADS if bwd else 2 * ML_HEADS)
        ls_col = _log_sigmoid(g_col)
        ls_row = _log_sigmoid(g_row)
        hp = lax.Precision.HIGHEST
        fc_all = _dot(triu if bwd else tril, ls_col, hp)
        fr_all = _dot(ls_row, tril if bwd else triu, hp)
        f_col = jnp.sum(jnp.where(lane_g == f_idx, fc_all, 0.0), axis=1, keepdims=True)
        i_col = jnp.sum(jnp.where(lane_g == i_idx, g_col, 0.0), axis=1, keepdims=True)
        f_row = jnp.sum(jnp.where(sub_g == f_idx, fr_all, 0.0), axis=0, keepdims=True)
        i_row = jnp.sum(jnp.where(sub_g == i_idx, g_row, 0.0), axis=0, keepdims=True)
        r_row = i_row - f_row
        end_lane = 0 if bwd else blk - 1
        b_end = jnp.sum(jnp.where(lane_t == end_lane, f_row, 0.0), axis=1, keepdims=True)

        h_out = None
        if need_h:
            d_log = jnp.where(mask, f_col + r_row, NEG_BIG)
            m_inter = f_col + m_st
            m_t = jnp.maximum(m_inter, jnp.max(d_log, axis=1, keepdims=True))
            w_inter = jnp.exp(m_inter - m_t)
            s = _dot_nt(q, k) * jnp.exp(d_log - m_t)
            num = w_inter * _dot(q, c_st.astype(BF16)) + _dot(s.astype(BF16), v)
            den = (w_inter * jnp.sum(q.astype(F32) * n_st, axis=1, keepdims=True)
                   + jnp.sum(s, axis=1, keepdims=True))
            h_out = num * (1.0 / jnp.maximum(jnp.abs(den), jnp.exp(-m_t)))

        g_row_log = b_end + r_row
        g_col_log = b_end + (i_col - f_col)
        m_new = jnp.maximum(b_end + m_st, jnp.max(g_row_log, axis=1, keepdims=True))
        a_prev = jnp.exp(b_end + m_st - m_new)
        kw = k.astype(F32) * jnp.exp(g_col_log - m_new)
        c_new = a_prev * c_st + _dot_tn(kw.astype(BF16), v)
        n_new = a_prev * n_st + jnp.sum(kw, axis=0, keepdims=True)
        return h_out, (c_new, n_new, m_new)

    def rows(j):
        return pl.ds(pl.multiple_of(j * blk, blk), blk)

    for bwd in (False, True):
        state = (jnp.zeros((ML_DQK, ML_DV), F32), jnp.zeros((1, ML_DQK), F32), jnp.zeros((1, 1), F32))
        for jj in range(n_ctx_blk):
            j = n_ctx_blk - 1 - jj if bwd else jj
            sl = pl.ds(j * blk, blk)
            h_out, state = step(qc_ref[sl, :], kc_ref[sl, :], vc_ref[sl, :], gcc_ref[sl, :], grc_ref[j],
                                state, bwd, need_ctx)
            if need_ctx:
                if bwd:
                    hc_ref[sl, :] = (accc[sl, :] + h_out).astype(hc_ref.dtype)
                else:
                    accc[sl, :] = h_out

        def body(jj, st, bwd=bwd):
            j = n_lat_blk - 1 - jj if bwd else jj
            sl = rows(j)
            h_out, st = step(ql_ref[sl, :], kl_ref[sl, :], vl_ref[sl, :], gcl_ref[sl, :], grl_ref[j],
                             st, bwd, True)
            if bwd:
                hl_ref[sl, :] = (accl[sl, :] + h_out).astype(hl_ref.dtype)
            else:
                accl[sl, :] = h_out
            return st

        lax.fori_loop(0, n_lat_blk, body, state)


def _mlstm(mq, mk, mv, mg, *, seq, ctx_len, batch, n_lat, need_ctx):
    blk = ML_BLOCK
    rows = mq.shape[0]
    ctx0 = n_lat // ctx_len
    mg_rows = mg.T.reshape(N_GATES, rows // blk, blk).transpose(1, 0, 2)

    def lat(width):
        return pl.BlockSpec((seq, width), lambda b, h: (b, h))

    def ctx(width):
        return pl.BlockSpec((ctx_len, width), lambda b, h: (ctx0 + b, h))

    in_specs = [lat(ML_DQK), ctx(ML_DQK), lat(ML_DQK), ctx(ML_DQK), lat(ML_DV), ctx(ML_DV),
                pl.BlockSpec((seq, N_GATES), lambda b, h: (b, 0)),
                pl.BlockSpec((ctx_len, N_GATES), lambda b, h: (ctx0 + b, 0)),
                pl.BlockSpec((seq // blk, N_GATES, blk), lambda b, h: (b, 0, 0)),
                pl.BlockSpec((ctx_len // blk, N_GATES, blk), lambda b, h: (ctx0 + b, 0, 0))]
    out_specs = [pl.BlockSpec((seq, ML_DV), lambda b, h: (b, h))]
    out_shape = [jax.ShapeDtypeStruct((n_lat, ML_V_W), BF16)]
    scratch = [pltpu.VMEM((seq, ML_DV), F32)]
    if need_ctx:
        out_specs.append(pl.BlockSpec((ctx_len, ML_DV), lambda b, h: (b, h)))
        out_shape.append(jax.ShapeDtypeStruct((batch * ctx_len, ML_V_W), BF16))
        scratch.append(pltpu.VMEM((ctx_len, ML_DV), F32))
    out = pl.pallas_call(
        functools.partial(_mlstm_kernel, need_ctx=need_ctx),
        grid=(batch, ML_HEADS),
        in_specs=in_specs,
        out_specs=out_specs,
        out_shape=out_shape,
        scratch_shapes=scratch,
        compiler_params=_params(2),
        name="mlstm",
    )(mq, mq, mk, mk, mv, mv, mg, mg, mg_rows, mg_rows)
    return out if need_ctx else (out[0], None)


def _diffattn_kernel(lam_ref, g_ref, q_ref, *refs, n_seg, lam_init):
    k_refs = refs[:n_seg]
    v_refs = refs[n_seg:2 * n_seg]
    o_ref = refs[2 * n_seg]
    lv = lam_ref[...]
    lam = (jnp.exp(jnp.sum(lv[0:1] * lv[1:2], axis=1, keepdims=True))
           - jnp.exp(jnp.sum(lv[2:3] * lv[3:4], axis=1, keepdims=True)) + lam_init)
    q = q_ref[...]
    lane = lax.broadcasted_iota(jnp.int32, q.shape, 1)
    zero = jnp.zeros_like(q)
    probs, coefs = [], []
    for half in range(2):
        in_half = (lane < DA_HALF) if half == 0 else (lane >= DA_HALF)
        qh = jnp.where(in_half, q, zero)
        scores = [_dot_nt(qh, k_ref[...]) for k_ref in k_refs]
        m = functools.reduce(jnp.maximum, [jnp.max(s, axis=1, keepdims=True) for s in scores])
        es = [jnp.exp(s - m) for s in scores]
        total = functools.reduce(jnp.add, [jnp.sum(e, axis=1, keepdims=True) for e in es])
        probs.append(es)
        coefs.append(1.0 / total)
    c0 = coefs[0]
    c1 = lam * coefs[1]
    o = None
    for j in range(n_seg):
        w = (probs[0][j] * c0 - probs[1][j] * c1).astype(BF16)
        part = _dot(w, v_refs[j][...])
        o = part if o is None else o + part
    ms = jnp.mean(o * o, axis=1, keepdims=True)
    o_ref[...] = (o * lax.rsqrt(ms + LN_EPS) * g_ref[...] * (1.0 - lam_init)).astype(o_ref.dtype)


def _diffattn(dq, dk, dv, lam_vecs, da_g, *, q_row0, q_len, segments, batch, lam_init, tq=256):
    tq = min(tq, q_len)
    nq = q_len // tq
    q0 = q_row0 // tq

    def seg_spec(row0, length):
        return pl.BlockSpec((length, LANES), lambda b, h, i: (row0 // length + b, h))

    seg_specs = [seg_spec(r0, ln) for r0, ln in segments]
    return pl.pallas_call(
        functools.partial(_diffattn_kernel, n_seg=len(segments), lam_init=lam_init),
        grid=(batch, DA_HEADS, nq),
        in_specs=[pl.BlockSpec(lam_vecs.shape, lambda b, h, i: (0, 0)),
                  pl.BlockSpec((1, DA_DV), lambda b, h, i: (0, 0)),
                  pl.BlockSpec((tq, LANES), lambda b, h, i: (q0 + b * nq + i, h))]
                 + seg_specs + seg_specs,
        out_specs=pl.BlockSpec((tq, DA_DV), lambda b, h, i: (b * nq + i, h)),
        out_shape=jax.ShapeDtypeStruct((batch * q_len, DA_V_W), BF16),
        compiler_params=_params(3),
        name="diffattn",
    )(lam_vecs, da_g.reshape(1, DA_DV), dq, *([dk] * len(segments)), *([dv] * len(segments)))


def _mixer_out_kernel(h_ref, mod_ref, sb_ref, p_ref, pprev_ref, pnext_ref, smo_ref, hml_ref, yda_ref,
                      gs_ref, gm_ref, gd_ref, convw_ref, wsc_ref, wml_ref, wda_ref, wo_ref,
                      lng_ref, lnb_ref, o_ref, *, seq, ctx_len, n_lat, alpha):
    tm = h_ref.shape[0]
    r0 = pl.program_id(0) * tm
    is_lat = r0 < n_lat
    pos = jnp.where(is_lat, r0 % seq, (r0 - n_lat) % ctx_len)
    seq_len = jnp.where(is_lat, seq, ctx_len)
    keep_prev = (pos != 0).astype(F32)
    keep_next = (pos + tm != seq_len).astype(F32)

    p = p_ref[...].astype(F32)
    prev_row = pprev_ref[...].astype(F32)[BF16_SUBLANES - 1:BF16_SUBLANES, :] * keep_prev
    next_row = pnext_ref[...].astype(F32)[0:1, :] * keep_next
    row = lax.broadcasted_iota(jnp.int32, p.shape, 0)
    p_before = jnp.where(row == 0, prev_row, pltpu.roll(p, 1, 0))
    p_after = jnp.where(row == tm - 1, next_row, pltpu.roll(p, tm - 1, 0))
    cw = convw_ref[...]
    conv = cw[0:1, :] * p_before + cw[1:2, :] * p + cw[2:3, :] * p_after
    y_sc = (sb_ref[...].astype(F32) * conv).astype(BF16)
    y_ml = smo_ref[...] * hml_ref[...]
    y = (gs_ref[...].astype(F32) * _dot(y_sc, wsc_ref[...])
         + gm_ref[...].astype(F32) * _dot(y_ml, wml_ref[...])
         + gd_ref[...].astype(F32) * _dot(yda_ref[...], wda_ref[...]))
    y = _dot(y.astype(BF16), wo_ref[...])
    h = h_ref[...]
    o_ref[...] = _layer_norm(alpha * h + mod_ref[0, 5:6, :] * y, lng_ref[...], lnb_ref[...])


def _mixer_out(h, mods3, sb, p, smo, hml, yda, gs, gm, gd, conv_w, w_sc, w_ml, w_da, w_o, ln_g, ln_b,
               *, n_rows, seq, ctx_len, batch, n_lat, alpha, tm=256):
    d = h.shape[1]
    tm = min(tm, ctx_len)
    halo = BF16_SUBLANES
    last_halo = p.shape[0] // halo - 1

    def row_spec(width):
        return pl.BlockSpec((tm, width), lambda i: (i, 0))

    kernel = functools.partial(_mixer_out_kernel, seq=seq, ctx_len=ctx_len, n_lat=n_lat, alpha=alpha)
    return pl.pallas_call(
        kernel,
        grid=(n_rows // tm,),
        in_specs=[row_spec(d),
                  pl.BlockSpec((1, N_MOD, d), lambda i: (jnp.minimum(i * tm // seq, batch), 0, 0)),
                  row_spec(SC_WIDTH), row_spec(SC_WIDTH),
                  pl.BlockSpec((halo, SC_WIDTH), lambda i: (jnp.maximum(i * (tm // halo) - 1, 0), 0)),
                  pl.BlockSpec((halo, SC_WIDTH),
                               lambda i: (jnp.minimum((i + 1) * (tm // halo), last_halo), 0)),
                  row_spec(ML_V_W), row_spec(ML_V_W), row_spec(DA_V_W),
                  row_spec(d), row_spec(d), row_spec(d),
                  _resident((SC_KSIZE, SC_WIDTH)),
                  _resident((SC_WIDTH, d)), _resident((ML_V_W, d)), _resident((DA_V_W, d)),
                  _resident((d, d)),
                  pl.BlockSpec((1, d), lambda i: (0, 0)),
                  pl.BlockSpec((1, d), lambda i: (0, 0))],
        out_specs=row_spec(d),
        out_shape=jax.ShapeDtypeStruct((n_rows, d), F32),
        compiler_params=_params(1),
        name="mixer_out",
    )(h, mods3, sb, p, p, p, smo, hml, yda, gs, gm, gd, conv_w, w_sc, w_ml, w_da, w_o,
      ln_g.reshape(1, d), ln_b.reshape(1, d))


def _rope_tables(seq, tile):
    n_freq = DA_HALF // 4
    t = jnp.arange(seq)
    row_ids = (t // GRID_W).astype(F32)
    col_ids = (t % GRID_W).astype(F32)
    inv = ROPE_BASE ** (-jnp.arange(n_freq, dtype=F32) / n_freq)
    ang = jnp.concatenate([row_ids[:, None] * inv, col_ids[:, None] * inv], axis=-1)
    cos, sin = jnp.cos(ang), jnp.sin(ang)
    reps = LANES // DA_HALF
    cos_t = jnp.tile(jnp.concatenate([cos, cos], axis=-1), (1, reps))
    sin_t = jnp.tile(jnp.concatenate([-sin, sin], axis=-1), (1, reps))
    cos_t = jnp.concatenate([cos_t, jnp.ones((tile, LANES), F32)], axis=0)
    sin_t = jnp.concatenate([sin_t, jnp.zeros((tile, LANES), F32)], axis=0)
    return cos_t, sin_t


def kernel(x, c, ctx, c_ctx, w_ada, b_ada, ln_g, ln_b, ffn1_up, ffn1_down, ffn2_up, ffn2_down, w_in, b_in,
           conv_w, w_sc, w_ml, w_da, w_o, lam_q1, lam_k1, lam_q2, lam_k2, da_norm_g):
    batch, seq, d = x.shape
    ctx_len = ctx.shape[1]
    depth = w_ada.shape[0]
    alpha = (2 * depth) ** 0.25
    n_lat = batch * seq
    n_ctx = batch * ctx_len
    mixer_in_tile = min(256, seq)

    n_cond = -(-(batch + 1) // 8) * 8
    cc = jnp.concatenate([c, c_ctx[None, :], jnp.zeros((n_cond - batch - 1, d), F32)], axis=0)
    mods = _ada(cc, w_ada, b_ada).reshape(depth, n_cond, N_MOD, d)

    cos_t, sin_t = _rope_tables(seq, mixer_in_tile)
    gate_lo = 3 * SC_WIDTH + 2 * ML_QK_W + 2 * ML_V_W
    gate_hi = gate_lo + N_GATES

    h = jnp.concatenate([x.reshape(n_lat, d), ctx.reshape(n_ctx, d)], axis=0)
    for l in range(depth):
        last = l == depth - 1
        lam_init = 0.8 - 0.6 * math.exp(-0.3 * l)
        mods3 = mods[l]
        w_main = jnp.concatenate([w_in[l, :, :gate_lo], w_in[l, :, gate_hi:]], axis=1).astype(BF16)
        b_main = jnp.concatenate([b_in[l, :gate_lo], b_in[l, gate_hi:]])[None, :]
        w_gate = jnp.pad(w_in[l, :, gate_lo:gate_hi], ((0, 0), (0, LANES - N_GATES))).astype(BF16)
        b_gate = jnp.pad(b_in[l, gate_lo:gate_hi], (0, LANES - N_GATES))[None, :]
        lam_vecs = jnp.stack([lam_q1[l], lam_k1[l], lam_q2[l], lam_k2[l]]).astype(F32)

        h = _ffn(h, mods3, ffn1_up[l].astype(BF16), ffn1_down[l].astype(BF16), ln_g[l, 0], ln_b[l, 0],
                 mod_base=0, n_rows=n_lat + n_ctx, seq=seq, batch=batch, alpha=alpha)

        (sb, p, mq, mk, mv, smo, mg, dq, dk, dv, gs, gm, gd) = _mixer_in(
            h, mods3, w_main, b_main, w_gate, b_gate, cos_t, sin_t,
            seq=seq, batch=batch, n_lat=n_lat, tm=mixer_in_tile)

        hml, hml_ctx = _mlstm(mq, mk, mv, mg, seq=seq, ctx_len=ctx_len, batch=batch, n_lat=n_lat,
                              need_ctx=not last)
        yda = _diffattn(dq, dk, dv, lam_vecs, da_norm_g[l], q_row0=0, q_len=seq,
                        segments=((0, seq), (n_lat, ctx_len)), batch=batch, lam_init=lam_init)
        if not last:
            yda_ctx = _diffattn(dq, dk, dv, lam_vecs, da_norm_g[l], q_row0=n_lat, q_len=ctx_len,
                                segments=((n_lat, ctx_len),), batch=batch, lam_init=lam_init)
            hml = jnp.concatenate([hml, hml_ctx], axis=0)
            yda = jnp.concatenate([yda, yda_ctx], axis=0)

        n_rows = n_lat if last else n_lat + n_ctx
        h = _mixer_out(h, mods3, sb, p, smo, hml, yda, gs, gm, gd, conv_w[l],
                       w_sc[l].astype(BF16), w_ml[l].astype(BF16), w_da[l].astype(BF16), w_o[l].astype(BF16),
                       ln_g[l, 1], ln_b[l, 1],
                       n_rows=n_rows, seq=seq, ctx_len=ctx_len, batch=batch, n_lat=n_lat, alpha=alpha)
        h = _ffn(h, mods3, ffn2_up[l].astype(BF16), ffn2_down[l].astype(BF16), ln_g[l, 2], ln_b[l, 2],
                 mod_base=6, n_rows=n_rows, seq=seq, batch=batch, alpha=alpha)
    return h[:n_lat].reshape(batch, seq, d)
```

```python
import functools
import math

import jax
import jax.numpy as jnp
from jax import lax
from jax.experimental import pallas as pl
from jax.experimental.pallas import tpu as pltpu

GRID_W = 64
N_MOD = 9
SC_WIDTH = 512
SC_KSIZE = 3
ML_HEADS = 4
ML_DQK = 128
ML_DV = 256
DA_HEADS = 4
DA_HALF = 64
DA_DV = 2 * DA_HALF
ROPE_BASE = 10000.0
LN_EPS = 1e-5

ML_QK_W = ML_HEADS * ML_DQK
ML_V_W = ML_HEADS * ML_DV
DA_QK_W = DA_HEADS * 2 * DA_HALF
DA_V_W = DA_HEADS * DA_DV
N_GATES = 4 * ML_HEADS

ML_BLOCK = 256
FFN_COLS = 256
LANES = 128
BF16_SUBLANES = 16
V7X_VMEM_BYTES = 64 * 1024 * 1024
VMEM_LIMIT = V7X_VMEM_BYTES - 8 * 1024 * 1024

F32 = jnp.float32
BF16 = jnp.bfloat16
NEG_BIG = -1e30
LOG2_E = 1.4426950408889634

assert ML_DQK == LANES and DA_DV == LANES and ML_BLOCK == 2 * LANES


def _dot(a, b, precision=None):
    return jnp.dot(a, b, preferred_element_type=F32, precision=precision)


def _dot_nt(a, b):
    return lax.dot_general(a, b, (((1,), (1,)), ((), ())), preferred_element_type=F32)


def _wide(x, width):
    return jnp.concatenate([x] * (width // LANES), axis=1)


def _layer_norm(y, g, b):
    mu = jnp.mean(y, axis=-1, keepdims=True)
    yc = y - mu
    var = jnp.mean(yc * yc, axis=-1, keepdims=True)
    return yc * lax.rsqrt(var + LN_EPS) * g + b


def _resident(shape):
    return pl.BlockSpec(shape, lambda *_: (0,) * len(shape), pipeline_mode=pl.Buffered(1))


def _params(n_axes):
    return pltpu.CompilerParams(dimension_semantics=("parallel",) * n_axes,
                                vmem_limit_bytes=VMEM_LIMIT)


def _ada_kernel(c_ref, w_ref, b_ref, o_ref):
    c = c_ref[...]
    a = (c * jax.nn.sigmoid(c)).astype(BF16)
    o_ref[0] = _dot(a, w_ref[0].astype(BF16)) + b_ref[0]


def _ada(cc, w_ada, b_ada):
    depth, d, n = w_ada.shape
    tn = d
    return pl.pallas_call(
        _ada_kernel,
        grid=(depth, n // tn),
        in_specs=[pl.BlockSpec(cc.shape, lambda l, j: (0, 0)),
                  pl.BlockSpec((1, d, tn), lambda l, j: (l, 0, j)),
                  pl.BlockSpec((1, 1, tn), lambda l, j: (l, 0, j))],
        out_specs=pl.BlockSpec((1, cc.shape[0], tn), lambda l, j: (l, 0, j)),
        out_shape=jax.ShapeDtypeStruct((depth, cc.shape[0], n), F32),
        compiler_params=_params(2),
        name="ada",
    )(cc, w_ada, b_ada.reshape(depth, 1, n))


def _ffn_kernel(h_ref, mod_ref, wup_ref, wdn_ref, lng_ref, lnb_ref, o_ref, g_scr, *, mod_base, alpha):
    f = wdn_ref.shape[0]
    h = h_ref[...]
    shift = mod_ref[0, mod_base:mod_base + 1, :]
    scale = mod_ref[0, mod_base + 1:mod_base + 2, :]
    gate = mod_ref[0, mod_base + 2:mod_base + 3, :]
    u = (h * (1.0 + scale) + shift).astype(BF16)
    for c in range(f // FFN_COLS):
        lo, hi = c * FFN_COLS, (c + 1) * FFN_COLS
        a = _dot(u, wup_ref[:, lo:hi])
        v = _dot(u, wup_ref[:, f + lo:f + hi])
        g_scr[:, lo:hi] = (a * jax.nn.sigmoid(a) * v).astype(BF16)
    d = _dot(g_scr[...], wdn_ref[...])
    o_ref[...] = _layer_norm(alpha * h + (0.5 * gate) * d, lng_ref[...], lnb_ref[...])


def _ffn(h, mods3, w_up, w_dn, ln_g, ln_b, *, mod_base, n_rows, seq, batch, alpha, tm=512):
    d = h.shape[1]
    f = w_dn.shape[0]
    tm = min(tm, seq)
    kernel = functools.partial(_ffn_kernel, mod_base=mod_base, alpha=alpha)
    return pl.pallas_call(
        kernel,
        grid=(n_rows // tm,),
        in_specs=[pl.BlockSpec((tm, d), lambda i: (i, 0)),
                  pl.BlockSpec((1, N_MOD, d), lambda i: (jnp.minimum(i * tm // seq, batch), 0, 0)),
                  _resident((d, 2 * f)),
                  _resident((f, d)),
                  pl.BlockSpec((1, d), lambda i: (0, 0)),
                  pl.BlockSpec((1, d), lambda i: (0, 0))],
        out_specs=pl.BlockSpec((tm, d), lambda i: (i, 0)),
        out_shape=jax.ShapeDtypeStruct((n_rows, d), F32),
        scratch_shapes=[pltpu.VMEM((tm, f), BF16)],
        compiler_params=_params(1),
        name="ffn",
    )(h, mods3, w_up, w_dn, ln_g.reshape(1, d), ln_b.reshape(1, d))


def _main_cols(d):
    sizes = (("sb", SC_WIDTH), ("sc", SC_WIDTH), ("sx", SC_WIDTH),
             ("mq", ML_QK_W), ("mk", ML_QK_W), ("mv", ML_V_W), ("mo", ML_V_W),
             ("dq", DA_QK_W), ("dk", DA_QK_W), ("dv", DA_V_W),
             ("gs", d), ("gm", d), ("gd", d))
    cols, off = {}, 0
    for name, w in sizes:
        cols[name] = (off, off + w)
        off += w
    return cols, off


def _mixer_in_kernel(h_ref, mod_ref, w_ref, b_ref, wg_ref, bg_ref, cos_ref, sin_ref,
                     sb_o, p_o, mq_o, mk_o, mv_o, smo_o, mg_o, dq_o, dk_o, dv_o, gs_o, gm_o, gd_o):
    d = h_ref.shape[1]
    tm = h_ref.shape[0]
    cols, _ = _main_cols(d)
    h = h_ref[...]
    shift = mod_ref[0, 3:4, :]
    scale = mod_ref[0, 4:5, :]
    u = (h * (1.0 + scale) + shift).astype(BF16)

    def proj(name):
        lo, hi = cols[name]
        return _dot(u, w_ref[:, lo:hi]) + b_ref[:, lo:hi]

    sb_o[...] = proj("sb").astype(BF16)
    p_o[...] = (proj("sc") * proj("sx")).astype(BF16)
    mq_o[...] = proj("mq").astype(BF16)
    mk_o[...] = (proj("mk") * (ML_DQK ** -0.5)).astype(BF16)
    mv_o[...] = proj("mv").astype(BF16)
    smo_o[...] = jax.nn.sigmoid(proj("mo")).astype(BF16)
    mg_o[...] = (_dot(u, wg_ref[...]) + bg_ref[...])[:, :N_GATES]

    cos_t = cos_ref[...]
    sin_t = sin_ref[...]
    lane = lax.broadcasted_iota(jnp.int32, cos_t.shape, 1)
    first_half = (lane % DA_HALF) < (DA_HALF // 2)

    def rope_store(z, out_ref, mult):
        for k in range(z.shape[1] // LANES):
            x = z[:, k * LANES:(k + 1) * LANES]
            partner = jnp.where(first_half,
                                pltpu.roll(x, LANES - DA_HALF // 2, 1),
                                pltpu.roll(x, DA_HALF // 2, 1))
            out_ref[:, k * LANES:(k + 1) * LANES] = ((x * cos_t + partner * sin_t) * mult).astype(BF16)

    rope_store(proj("dq"), dq_o, (DA_HALF ** -0.5) * LOG2_E)
    rope_store(proj("dk"), dk_o, 1.0)
    dv = proj("dv").astype(BF16)
    ones = jnp.ones((tm, DA_DV), BF16)
    for k in range(DA_HEADS):
        dv_o[:, 2 * k * DA_DV:(2 * k + 1) * DA_DV] = dv[:, k * DA_DV:(k + 1) * DA_DV]
        dv_o[:, (2 * k + 1) * DA_DV:(2 * k + 2) * DA_DV] = ones
    gs_o[...] = jax.nn.sigmoid(proj("gs")).astype(BF16)
    gm_o[...] = jax.nn.sigmoid(proj("gm")).astype(BF16)
    gd_o[...] = jax.nn.sigmoid(proj("gd")).astype(BF16)


def _mixer_in(h, mods3, w_main, b_main, w_gate, b_gate, cos_t, sin_t, *, seq, batch, n_lat, tm=256):
    rows, d = h.shape
    _, n_main = _main_cols(d)
    tm = min(tm, seq)
    lat_tiles = n_lat // tm
    rope_blocks = seq // tm

    def row_spec(width):
        return pl.BlockSpec((tm, width), lambda i: (i, 0))

    def rope_map(i):
        return (jnp.where(i < lat_tiles, i % rope_blocks, rope_blocks), 0)

    widths = (SC_WIDTH, SC_WIDTH, ML_QK_W, ML_QK_W, ML_V_W, ML_V_W, N_GATES,
              DA_QK_W, DA_QK_W, 2 * DA_V_W, d, d, d)
    dtypes = (BF16,) * 6 + (F32,) + (BF16,) * 6
    return pl.pallas_call(
        _mixer_in_kernel,
        grid=(rows // tm,),
        in_specs=[row_spec(d),
                  pl.BlockSpec((1, N_MOD, d), lambda i: (jnp.minimum(i * tm // seq, batch), 0, 0)),
                  _resident((d, n_main)),
                  _resident((1, n_main)),
                  _resident((d, LANES)),
                  _resident((1, LANES)),
                  pl.BlockSpec((tm, LANES), rope_map),
                  pl.BlockSpec((tm, LANES), rope_map)],
        out_specs=[row_spec(w) for w in widths],
        out_shape=[jax.ShapeDtypeStruct((rows, w), dt) for w, dt in zip(widths, dtypes)],
        compiler_params=_params(1),
        name="mixer_in",
    )(h, mods3, w_main, b_main, w_gate, b_gate, cos_t, sin_t)


def _log_sigmoid(x):
    return jnp.minimum(x, 0.0) - jnp.log(1.0 + jnp.exp(-jnp.abs(x)))


def _mlstm_gate_kernel(g_ref, o_ref, o3_ref):
    g = g_ref[...]
    blk = g.shape[0]
    t_idx = lax.broadcasted_iota(jnp.int32, (blk, blk), 0)
    s_idx = lax.broadcasted_iota(jnp.int32, (blk, blk), 1)
    ls = _log_sigmoid(g)
    hp = lax.Precision.HIGHEST
    prefix = _dot((s_idx <= t_idx).astype(F32), ls, hp)
    suffix = _dot((s_idx >= t_idx).astype(F32), ls, hp)
    lane = lax.broadcasted_iota(jnp.int32, g.shape, 1)
    out = jnp.where(lane < 2 * ML_HEADS, g, jnp.where(lane < 3 * ML_HEADS, prefix, suffix))
    o_ref[...] = out
    src = lax.broadcasted_iota(jnp.int32, (N_GATES, LANES), 0)
    dst = lax.broadcasted_iota(jnp.int32, (N_GATES, LANES), 1)
    rest = out
    placed = None
    for term in range(3):
        piece = rest.astype(BF16)
        rest = rest - piece.astype(F32)
        part = _dot(piece, (dst == src + term * N_GATES).astype(BF16))
        placed = part if placed is None else placed + part
    o3_ref[...] = placed.astype(BF16)


def _mlstm_gates(mg):
    rows = mg.shape[0]
    return pl.pallas_call(
        _mlstm_gate_kernel,
        grid=(rows // ML_BLOCK,),
        in_specs=[pl.BlockSpec((ML_BLOCK, N_GATES), lambda i: (i, 0))],
        out_specs=[pl.BlockSpec((ML_BLOCK, N_GATES), lambda i: (i, 0)),
                   pl.BlockSpec((ML_BLOCK, LANES), lambda i: (i, 0))],
        out_shape=[jax.ShapeDtypeStruct((rows, N_GATES), F32),
                   jax.ShapeDtypeStruct((rows, LANES), BF16)],
        compiler_params=_params(1),
        name="mlstm_gates",
    )(mg)


def _mlstm_scan_kernel(ql_ref, qc_ref, kl_ref, kc_ref, ktl_ref, ktc_ref, vl_ref, vc_ref,
                       g3l_ref, g3c_ref, grl_ref, grc_ref, *rest, need_ctx):
    if need_ctx:
        hl_ref, hc_ref, c_state, n_state, m_state, c_in, n_in, m_in = rest
    else:
        hl_ref, c_state, n_state, m_state, c_in, n_in, m_in = rest
        hc_ref = None
    blk = ML_BLOCK
    n_lat_blk = ql_ref.shape[0] // blk
    n_ctx_blk = qc_ref.shape[0] // blk
    head = pl.program_id(1)

    t_idx = lax.broadcasted_iota(jnp.int32, (blk, blk), 0)
    s_idx = lax.broadcasted_iota(jnp.int32, (blk, blk), 1)
    lane_t = lax.broadcasted_iota(jnp.int32, (1, blk), 1)
    ones_k = jnp.ones((blk, LANES), BF16)
    sel_src = lax.broadcasted_iota(jnp.int32, (LANES, 2 * LANES), 0)
    sel_dst = lax.broadcasted_iota(jnp.int32, (LANES, 2 * LANES), 1)
    sel_col = jnp.where(sel_dst < LANES, 2 * ML_HEADS + head, 3 * ML_HEADS + head)
    sel = jnp.logical_and(sel_src % N_GATES == sel_col, sel_src < 3 * N_GATES).astype(BF16)

    def gate_rows(gr_ref, j, bwd):
        i_idx = head + (ML_HEADS if bwd else 0)
        f_idx = head + (3 * ML_HEADS if bwd else 2 * ML_HEADS)
        return gr_ref[j, pl.ds(f_idx, 1), :], gr_ref[j, pl.ds(i_idx, 1), :]

    def advance(kt, v, f_row, i_row, bwd, slot):
        d = 1 if bwd else 0
        c_st, n_st, m_st = c_state[d], n_state[d], m_state[d]
        c_in[slot, d * ML_DQK:(d + 1) * ML_DQK, :] = c_st.astype(BF16)
        n_in[slot, :, d * LANES:(d + 1) * LANES] = n_st.astype(BF16)
        m_in[slot, :, d * LANES:(d + 1) * LANES] = m_st
        m_prev = m_st[0:1, 0:1]
        b_end = jnp.sum(jnp.where(lane_t == (0 if bwd else blk - 1), f_row, 0.0), axis=1, keepdims=True)
        g_log = b_end + (i_row - f_row)
        m_new = jnp.maximum(b_end + m_prev, jnp.max(g_log, axis=1, keepdims=True))
        a_prev = jnp.exp(b_end + m_prev - m_new)
        kw = (kt.astype(F32) * jnp.exp(g_log - m_new)).astype(BF16)
        c_state[d] = a_prev * c_st + _dot(kw, v)
        n_state[d] = a_prev * n_st + _dot(kw, ones_k)
        m_state[d] = jnp.broadcast_to(m_new, m_st.shape)

    def outputs(q, k, v, g3, rows, slot):
        qk = _dot_nt(q, k)
        f_both = _dot(g3, sel)
        qn_both = _dot(q, n_in[slot])
        m_prev_both = m_in[slot][0:1, :]
        qf = q.astype(F32)
        scaled, inter_w = [], []
        for d, bwd in enumerate((False, True)):
            f_row, i_row = rows[d]
            f_rep = f_both[:, d * LANES:(d + 1) * LANES]
            mask = (s_idx >= t_idx) if bwd else (s_idx <= t_idx)
            d_log = jnp.where(mask, _wide(f_rep, blk) + (i_row - f_row), NEG_BIG)
            m_loc = jnp.broadcast_to(jnp.max(d_log, axis=1, keepdims=True), (blk, LANES))
            s = qk * jnp.exp(d_log - _wide(m_loc, blk))
            den_loc = jnp.broadcast_to(jnp.sum(s, axis=1, keepdims=True), (blk, LANES))
            m_inter = f_rep + m_prev_both[:, d * LANES:(d + 1) * LANES]
            m_t = jnp.maximum(m_inter, m_loc)
            w_inter = jnp.exp(m_inter - m_t)
            w_loc = jnp.exp(m_loc - m_t)
            den = w_inter * qn_both[:, d * LANES:(d + 1) * LANES] + w_loc * den_loc
            inv = 1.0 / jnp.maximum(jnp.abs(den), jnp.exp(-m_t))
            scaled.append(s * _wide(w_loc * inv, blk))
            inter_w.append((qf * (w_inter * inv)).astype(BF16))
        x = jnp.concatenate(inter_w, axis=1)
        return _dot(x, c_in[slot]) + _dot((scaled[0] + scaled[1]).astype(BF16), v)

    def lat_rows(j):
        return pl.ds(pl.multiple_of(j * blk, blk), blk)

    c_state[...] = jnp.zeros_like(c_state)
    n_state[...] = jnp.zeros_like(n_state)
    m_state[...] = jnp.zeros_like(m_state)
    for bwd in (False, True):
        for jj in range(n_ctx_blk):
            j = n_ctx_blk - 1 - jj if bwd else jj
            f_row, i_row = gate_rows(grc_ref, j, bwd)
            advance(ktc_ref[j], vc_ref[pl.ds(j * blk, blk), :], f_row, i_row, bwd, j)

    def scan_body(jj, carry):
        for bwd in (False, True):
            j = n_lat_blk - 1 - jj if bwd else jj
            f_row, i_row = gate_rows(grl_ref, j, bwd)
            advance(ktl_ref[j], vl_ref[lat_rows(j), :], f_row, i_row, bwd, n_ctx_blk + j)
        return carry

    lax.fori_loop(0, n_lat_blk, scan_body, 0)

    if need_ctx:
        for j in range(n_ctx_blk):
            sl = pl.ds(j * blk, blk)
            rows = (gate_rows(grc_ref, j, False), gate_rows(grc_ref, j, True))
            hc_ref[sl, :] = outputs(qc_ref[sl, :], kc_ref[sl, :], vc_ref[sl, :], g3c_ref[sl, :], rows,
                                    j).astype(hc_ref.dtype)

    def out_body(j, carry):
        sl = lat_rows(j)
        rows = (gate_rows(grl_ref, j, False), gate_rows(grl_ref, j, True))
        hl_ref[sl, :] = outputs(ql_ref[sl, :], kl_ref[sl, :], vl_ref[sl, :], g3l_ref[sl, :], rows,
                                n_ctx_blk + j).astype(hl_ref.dtype)
        return carry

    lax.fori_loop(0, n_lat_blk, out_body, 0)


def _mlstm_scan(mq, mk, mv, mg, *, seq, ctx_len, batch, n_lat, need_ctx):
    blk = ML_BLOCK
    rows = mq.shape[0]
    ctx0 = n_lat // ctx_len
    n_blk = (seq + ctx_len) // blk
    gates, gates3 = _mlstm_gates(mg)
    gates_rows = gates.T.reshape(N_GATES, rows // blk, blk).transpose(1, 0, 2)
    mk_t = mk.reshape(rows // blk, blk, ML_QK_W).transpose(0, 2, 1)

    def lat(width):
        return pl.BlockSpec((seq, width), lambda b, h: (b, h))

    def ctx(width):
        return pl.BlockSpec((ctx_len, width), lambda b, h: (ctx0 + b, h))

    in_specs = [lat(ML_DQK), ctx(ML_DQK), lat(ML_DQK), ctx(ML_DQK),
                pl.BlockSpec((seq // blk, ML_DQK, blk), lambda b, h: (b, h, 0)),
                pl.BlockSpec((ctx_len // blk, ML_DQK, blk), lambda b, h: (ctx0 + b, h, 0)),
                lat(ML_DV), ctx(ML_DV),
                pl.BlockSpec((seq, LANES), lambda b, h: (b, 0)),
                pl.BlockSpec((ctx_len, LANES), lambda b, h: (ctx0 + b, 0)),
                pl.BlockSpec((seq // blk, N_GATES, blk), lambda b, h: (b, 0, 0)),
                pl.BlockSpec((ctx_len // blk, N_GATES, blk), lambda b, h: (ctx0 + b, 0, 0))]
    out_specs = [pl.BlockSpec((seq, ML_DV), lambda b, h: (b, h))]
    out_shape = [jax.ShapeDtypeStruct((n_lat, ML_V_W), BF16)]
    if need_ctx:
        out_specs.append(pl.BlockSpec((ctx_len, ML_DV), lambda b, h: (b, h)))
        out_shape.append(jax.ShapeDtypeStruct((batch * ctx_len, ML_V_W), BF16))
    scratch = [pltpu.VMEM((2, ML_DQK, ML_DV), F32),
               pltpu.VMEM((2, ML_DQK, LANES), F32),
               pltpu.VMEM((2, 8, LANES), F32),
               pltpu.VMEM((n_blk, 2 * ML_DQK, ML_DV), BF16),
               pltpu.VMEM((n_blk, ML_DQK, 2 * LANES), BF16),
               pltpu.VMEM((n_blk, 8, 2 * LANES), F32)]
    out = pl.pallas_call(
        functools.partial(_mlstm_scan_kernel, need_ctx=need_ctx),
        grid=(batch, ML_HEADS),
        in_specs=in_specs,
        out_specs=out_specs,
        out_shape=out_shape,
        scratch_shapes=scratch,
        compiler_params=_params(2),
        name="mlstm",
    )(mq, mq, mk, mk, mk_t, mk_t, mv, mv, gates3, gates3, gates_rows, gates_rows)
    return out if need_ctx else (out[0], None)


def _diffattn_kernel(lam_ref, g_ref, q_ref, *refs, n_seg, lam_init):
    k_refs = refs[:n_seg]
    v_refs = refs[n_seg:2 * n_seg]
    o_ref = refs[2 * n_seg]
    lv = lam_ref[...]
    lam = (jnp.exp(jnp.sum(lv[0:1] * lv[1:2], axis=1, keepdims=True))
           - jnp.exp(jnp.sum(lv[2:3] * lv[3:4], axis=1, keepdims=True)) + lam_init)
    q = q_ref[...]
    lane = lax.broadcasted_iota(jnp.int32, q.shape, 1)
    zero = jnp.zeros_like(q)
    heads = []
    for half in range(2):
        in_half = (lane < DA_HALF) if half == 0 else (lane >= DA_HALF)
        qh = jnp.where(in_half, q, zero)
        scores = [_dot_nt(qh, k_ref[...]) for k_ref in k_refs]
        m = functools.reduce(jnp.maximum, [jnp.max(s, axis=1, keepdims=True) for s in scores])
        acc = None
        for s, v_ref in zip(scores, v_refs):
            part = _dot(jnp.exp2((s - m).astype(BF16)), v_ref[...])
            acc = part if acc is None else acc + part
        heads.append(acc[:, :DA_DV] * (1.0 / acc[:, DA_DV:]))
    o = heads[0] - lam * heads[1]
    ms = jnp.mean(o * o, axis=1, keepdims=True)
    o_ref[...] = (o * lax.rsqrt(ms + LN_EPS) * g_ref[...] * (1.0 - lam_init)).astype(o_ref.dtype)


def _diffattn(dq, dk, dv, lam_vecs, da_g, *, q_row0, q_len, segments, batch, lam_init, tq=256):
    tq = min(tq, q_len)
    nq = q_len // tq
    q0 = q_row0 // tq

    def seg_spec(row0, length, width):
        return pl.BlockSpec((length, width), lambda b, h, i: (row0 // length + b, h))

    return pl.pallas_call(
        functools.partial(_diffattn_kernel, n_seg=len(segments), lam_init=lam_init),
        grid=(batch, DA_HEADS, nq),
        in_specs=[pl.BlockSpec(lam_vecs.shape, lambda b, h, i: (0, 0)),
                  pl.BlockSpec((1, DA_DV), lambda b, h, i: (0, 0)),
                  pl.BlockSpec((tq, LANES), lambda b, h, i: (q0 + b * nq + i, h))]
                 + [seg_spec(r0, ln, LANES) for r0, ln in segments]
                 + [seg_spec(r0, ln, 2 * DA_DV) for r0, ln in segments],
        out_specs=pl.BlockSpec((tq, DA_DV), lambda b, h, i: (b * nq + i, h)),
        out_shape=jax.ShapeDtypeStruct((batch * q_len, DA_V_W), BF16),
        compiler_params=_params(3),
        name="diffattn",
    )(lam_vecs, da_g.reshape(1, DA_DV), dq, *([dk] * len(segments)), *([dv] * len(segments)))


def _mixer_out_kernel(h_ref, mod_ref, sb_ref, p_ref, pprev_ref, pnext_ref, smo_ref, hml_ref, yda_ref,
                      gs_ref, gm_ref, gd_ref, convw_ref, wsc_ref, wml_ref, wda_ref, wo_ref,
                      lng_ref, lnb_ref, o_ref, *, seq, ctx_len, n_lat, alpha):
    tm = h_ref.shape[0]
    r0 = pl.program_id(0) * tm
    is_lat = r0 < n_lat
    pos = jnp.where(is_lat, r0 % seq, (r0 - n_lat) % ctx_len)
    seq_len = jnp.where(is_lat, seq, ctx_len)
    keep_prev = (pos != 0).astype(F32)
    keep_next = (pos + tm != seq_len).astype(F32)

    p = p_ref[...].astype(F32)
    prev_row = pprev_ref[...].astype(F32)[BF16_SUBLANES - 1:BF16_SUBLANES, :] * keep_prev
    next_row = pnext_ref[...].astype(F32)[0:1, :] * keep_next
    row = lax.broadcasted_iota(jnp.int32, p.shape, 0)
    p_before = jnp.where(row == 0, prev_row, pltpu.roll(p, 1, 0))
    p_after = jnp.where(row == tm - 1, next_row, pltpu.roll(p, tm - 1, 0))
    cw = convw_ref[...]
    conv = cw[0:1, :] * p_before + cw[1:2, :] * p + cw[2:3, :] * p_after
    y_sc = (sb_ref[...].astype(F32) * conv).astype(BF16)
    y_ml = smo_ref[...] * hml_ref[...]
    y = (gs_ref[...].astype(F32) * _dot(y_sc, wsc_ref[...])
         + gm_ref[...].astype(F32) * _dot(y_ml, wml_ref[...])
         + gd_ref[...].astype(F32) * _dot(yda_ref[...], wda_ref[...]))
    y = _dot(y.astype(BF16), wo_ref[...])
    h = h_ref[...]
    o_ref[...] = _layer_norm(alpha * h + mod_ref[0, 5:6, :] * y, lng_ref[...], lnb_ref[...])


def _mixer_out(h, mods3, sb, p, smo, hml, yda, gs, gm, gd, conv_w, w_sc, w_ml, w_da, w_o, ln_g, ln_b,
               *, n_rows, seq, ctx_len, batch, n_lat, alpha, tm=256):
    d = h.shape[1]
    tm = min(tm, ctx_len)
    halo = BF16_SUBLANES
    last_halo = p.shape[0] // halo - 1

    def row_spec(width):
        return pl.BlockSpec((tm, width), lambda i: (i, 0))

    kernel = functools.partial(_mixer_out_kernel, seq=seq, ctx_len=ctx_len, n_lat=n_lat, alpha=alpha)
    return pl.pallas_call(
        kernel,
        grid=(n_rows // tm,),
        in_specs=[row_spec(d),
                  pl.BlockSpec((1, N_MOD, d), lambda i: (jnp.minimum(i * tm // seq, batch), 0, 0)),
                  row_spec(SC_WIDTH), row_spec(SC_WIDTH),
                  pl.BlockSpec((halo, SC_WIDTH), lambda i: (jnp.maximum(i * (tm // halo) - 1, 0), 0)),
                  pl.BlockSpec((halo, SC_WIDTH),
                               lambda i: (jnp.minimum((i + 1) * (tm // halo), last_halo), 0)),
                  row_spec(ML_V_W), row_spec(ML_V_W), row_spec(DA_V_W),
                  row_spec(d), row_spec(d), row_spec(d),
                  _resident((SC_KSIZE, SC_WIDTH)),
                  _resident((SC_WIDTH, d)), _resident((ML_V_W, d)), _resident((DA_V_W, d)),
                  _resident((d, d)),
                  pl.BlockSpec((1, d), lambda i: (0, 0)),
                  pl.BlockSpec((1, d), lambda i: (0, 0))],
        out_specs=row_spec(d),
        out_shape=jax.ShapeDtypeStruct((n_rows, d), F32),
        compiler_params=_params(1),
        name="mixer_out",
    )(h, mods3, sb, p, p, p, smo, hml, yda, gs, gm, gd, conv_w, w_sc, w_ml, w_da, w_o,
      ln_g.reshape(1, d), ln_b.reshape(1, d))


def _rope_tables(seq, tile):
    n_freq = DA_HALF // 4
    t = jnp.arange(seq)
    row_ids = (t // GRID_W).astype(F32)
    col_ids = (t % GRID_W).astype(F32)
    inv = ROPE_BASE ** (-jnp.arange(n_freq, dtype=F32) / n_freq)
    ang = jnp.concatenate([row_ids[:, None] * inv, col_ids[:, None] * inv], axis=-1)
    cos, sin = jnp.cos(ang), jnp.sin(ang)
    reps = LANES // DA_HALF
    cos_t = jnp.tile(jnp.concatenate([cos, cos], axis=-1), (1, reps))
    sin_t = jnp.tile(jnp.concatenate([-sin, sin], axis=-1), (1, reps))
    cos_t = jnp.concatenate([cos_t, jnp.ones((tile, LANES), F32)], axis=0)
    sin_t = jnp.concatenate([sin_t, jnp.zeros((tile, LANES), F32)], axis=0)
    return cos_t, sin_t


def kernel(x, c, ctx, c_ctx, w_ada, b_ada, ln_g, ln_b, ffn1_up, ffn1_down, ffn2_up, ffn2_down, w_in, b_in,
           conv_w, w_sc, w_ml, w_da, w_o, lam_q1, lam_k1, lam_q2, lam_k2, da_norm_g):
    batch, seq, d = x.shape
    ctx_len = ctx.shape[1]
    depth = w_ada.shape[0]
    alpha = (2 * depth) ** 0.25
    n_lat = batch * seq
    n_ctx = batch * ctx_len
    mixer_in_tile = min(256, seq)

    n_cond = -(-(batch + 1) // 8) * 8
    cc = jnp.concatenate([c, c_ctx[None, :], jnp.zeros((n_cond - batch - 1, d), F32)], axis=0)
    mods = _ada(cc, w_ada, b_ada).reshape(depth, n_cond, N_MOD, d)

    cos_t, sin_t = _rope_tables(seq, mixer_in_tile)
    gate_lo = 3 * SC_WIDTH + 2 * ML_QK_W + 2 * ML_V_W
    gate_hi = gate_lo + N_GATES

    h = jnp.concatenate([x.reshape(n_lat, d), ctx.reshape(n_ctx, d)], axis=0)
    for l in range(depth):
        last = l == depth - 1
        lam_init = 0.8 - 0.6 * math.exp(-0.3 * l)
        mods3 = mods[l]
        w_main = jnp.concatenate([w_in[l, :, :gate_lo], w_in[l, :, gate_hi:]], axis=1).astype(BF16)
        b_main = jnp.concatenate([b_in[l, :gate_lo], b_in[l, gate_hi:]])[None, :]
        w_gate = jnp.pad(w_in[l, :, gate_lo:gate_hi], ((0, 0), (0, LANES - N_GATES))).astype(BF16)
        b_gate = jnp.pad(b_in[l, gate_lo:gate_hi], (0, LANES - N_GATES))[None, :]
        lam_vecs = jnp.stack([lam_q1[l], lam_k1[l], lam_q2[l], lam_k2[l]]).astype(F32)

        h = _ffn(h, mods3, ffn1_up[l].astype(BF16), ffn1_down[l].astype(BF16), ln_g[l, 0], ln_b[l, 0],
                 mod_base=0, n_rows=n_lat + n_ctx, seq=seq, batch=batch, alpha=alpha)

        (sb, p, mq, mk, mv, smo, mg, dq, dk, dv, gs, gm, gd) = _mixer_in(
            h, mods3, w_main, b_main, w_gate, b_gate, cos_t, sin_t,
            seq=seq, batch=batch, n_lat=n_lat, tm=mixer_in_tile)

        hml, hml_ctx = _mlstm_scan(mq, mk, mv, mg, seq=seq, ctx_len=ctx_len, batch=batch, n_lat=n_lat,
                                   need_ctx=not last)
        yda = _diffattn(dq, dk, dv, lam_vecs, da_norm_g[l], q_row0=0, q_len=seq,
                        segments=((0, seq), (n_lat, ctx_len)), batch=batch, lam_init=lam_init)
        if not last:
            yda_ctx = _diffattn(dq, dk, dv, lam_vecs, da_norm_g[l], q_row0=n_lat, q_len=ctx_len,
                                segments=((n_lat, ctx_len),), batch=batch, lam_init=lam_init)
            hml = jnp.concatenate([hml, hml_ctx], axis=0)
            yda = jnp.concatenate([yda, yda_ctx], axis=0)

        n_rows = n_lat if last else n_lat + n_ctx
        h = _mixer_out(h, mods3, sb, p, smo, hml, yda, gs, gm, gd, conv_w[l],
                       w_sc[l].astype(BF16), w_ml[l].astype(BF16), w_da[l].astype(BF16), w_o[l].astype(BF16),
                       ln_g[l, 1], ln_b[l, 1],
                       n_rows=n_rows, seq=seq, ctx_len=ctx_len, batch=batch, n_lat=n_lat, alpha=alpha)
        h = _ffn(h, mods3, ffn2_up[l].astype(BF16), ffn2_down[l].astype(BF16), ln_g[l, 2], ln_b[l, 2],
                 mod_base=6, n_rows=n_rows, seq=seq, batch=batch, alpha=alpha)
    return h[:n_lat].reshape(batch, seq, d)
```

```python
import functools
import math

import jax
import jax.numpy as jnp
from jax import lax
from jax.experimental import pallas as pl
from jax.experimental.pallas import tpu as pltpu

GRID_W = 64
N_MOD = 9
SC_WIDTH = 512
SC_KSIZE = 3
ML_HEADS = 4
ML_DQK = 128
ML_DV = 256
DA_HEADS = 4
DA_HALF = 64
DA_DV = 2 * DA_HALF
ROPE_BASE = 10000.0
LN_EPS = 1e-5

ML_QK_W = ML_HEADS * ML_DQK
ML_V_W = ML_HEADS * ML_DV
DA_QK_W = DA_HEADS * 2 * DA_HALF
DA_V_W = DA_HEADS * DA_DV
N_GATES = 4 * ML_HEADS

ML_BLOCK = 256
FFN_COLS = 256
LANES = 128
BF16_SUBLANES = 16
V7X_VMEM_BYTES = 64 * 1024 * 1024
VMEM_LIMIT = V7X_VMEM_BYTES - 8 * 1024 * 1024

F32 = jnp.float32
BF16 = jnp.bfloat16
NEG_BIG = -1e30
LOG2_E = 1.4426950408889634

assert ML_DQK == LANES and DA_DV == LANES and ML_BLOCK == 2 * LANES


def _dot(a, b, precision=None):
    return jnp.dot(a, b, preferred_element_type=F32, precision=precision)


def _dot_nt(a, b):
    return lax.dot_general(a, b, (((1,), (1,)), ((), ())), preferred_element_type=F32)


def _wide(x, width):
    return jnp.concatenate([x] * (width // LANES), axis=1)


def _layer_norm(y, g, b):
    mu = jnp.mean(y, axis=-1, keepdims=True)
    yc = y - mu
    var = jnp.mean(yc * yc, axis=-1, keepdims=True)
    return yc * lax.rsqrt(var + LN_EPS) * g + b


def _resident(shape):
    return pl.BlockSpec(shape, lambda *_: (0,) * len(shape), pipeline_mode=pl.Buffered(1))


def _params(n_axes):
    return pltpu.CompilerParams(dimension_semantics=("parallel",) * n_axes,
                                vmem_limit_bytes=VMEM_LIMIT)


def _ada_kernel(c_ref, w_ref, b_ref, o_ref):
    c = c_ref[...]
    a = (c * jax.nn.sigmoid(c)).astype(BF16)
    o_ref[0] = _dot(a, w_ref[0].astype(BF16)) + b_ref[0]


def _ada(cc, w_ada, b_ada):
    depth, d, n = w_ada.shape
    tn = d
    return pl.pallas_call(
        _ada_kernel,
        grid=(depth, n // tn),
        in_specs=[pl.BlockSpec(cc.shape, lambda l, j: (0, 0)),
                  pl.BlockSpec((1, d, tn), lambda l, j: (l, 0, j)),
                  pl.BlockSpec((1, 1, tn), lambda l, j: (l, 0, j))],
        out_specs=pl.BlockSpec((1, cc.shape[0], tn), lambda l, j: (l, 0, j)),
        out_shape=jax.ShapeDtypeStruct((depth, cc.shape[0], n), F32),
        compiler_params=_params(2),
        name="ada",
    )(cc, w_ada, b_ada.reshape(depth, 1, n))


def _two_source_specs(tm, width, lat_tiles):
    return [pl.BlockSpec((tm, width), lambda i: (jnp.minimum(i, lat_tiles - 1), 0)),
            pl.BlockSpec((tm, width), lambda i: (jnp.maximum(i - lat_tiles, 0), 0))]


def _ffn_kernel(*refs, mod_base, alpha, lat_tiles):
    if lat_tiles is None:
        h_ref, mod_ref, wup_ref, wdn_ref, lng_ref, lnb_ref, o_ref, g_scr = refs
        h = h_ref[...]
    else:
        h_ref, hc_ref, mod_ref, wup_ref, wdn_ref, lng_ref, lnb_ref, o_ref, g_scr = refs
        h = jnp.where(pl.program_id(0) < lat_tiles, h_ref[...], hc_ref[...])
    f = wdn_ref.shape[0]
    shift = mod_ref[0, mod_base:mod_base + 1, :]
    scale = mod_ref[0, mod_base + 1:mod_base + 2, :]
    gate = mod_ref[0, mod_base + 2:mod_base + 3, :]
    u = (h * (1.0 + scale) + shift).astype(BF16)
    for c in range(f // FFN_COLS):
        lo, hi = c * FFN_COLS, (c + 1) * FFN_COLS
        a = _dot(u, wup_ref[:, lo:hi])
        v = _dot(u, wup_ref[:, f + lo:f + hi])
        g_scr[:, lo:hi] = (a * jax.nn.sigmoid(a) * v).astype(BF16)
    d = _dot(g_scr[...], wdn_ref[...])
    o_ref[...] = _layer_norm(alpha * h + (0.5 * gate) * d, lng_ref[...], lnb_ref[...])


def _ffn(h, mods3, w_up, w_dn, ln_g, ln_b, *, mod_base, n_rows, seq, batch, alpha, h_ctx=None, tm=512):
    d = h.shape[1]
    f = w_dn.shape[0]
    tm = min(tm, seq)
    lat_tiles = None if h_ctx is None else h.shape[0] // tm
    kernel = functools.partial(_ffn_kernel, mod_base=mod_base, alpha=alpha, lat_tiles=lat_tiles)
    if h_ctx is None:
        sources, source_specs = [h], [pl.BlockSpec((tm, d), lambda i: (i, 0))]
    else:
        sources, source_specs = [h, h_ctx], _two_source_specs(tm, d, lat_tiles)
    return pl.pallas_call(
        kernel,
        grid=(n_rows // tm,),
        in_specs=source_specs + [
                  pl.BlockSpec((1, N_MOD, d), lambda i: (jnp.minimum(i * tm // seq, batch), 0, 0)),
                  _resident((d, 2 * f)),
                  _resident((f, d)),
                  pl.BlockSpec((1, d), lambda i: (0, 0)),
                  pl.BlockSpec((1, d), lambda i: (0, 0))],
        out_specs=pl.BlockSpec((tm, d), lambda i: (i, 0)),
        out_shape=jax.ShapeDtypeStruct((n_rows, d), F32),
        scratch_shapes=[pltpu.VMEM((tm, f), BF16)],
        compiler_params=_params(1),
        name="ffn",
    )(*sources, mods3, w_up, w_dn, ln_g.reshape(1, d), ln_b.reshape(1, d))


def _main_cols(d):
    sizes = (("sb", SC_WIDTH), ("sc", SC_WIDTH), ("sx", SC_WIDTH),
             ("mq", ML_QK_W), ("mk", ML_QK_W), ("mv", ML_V_W), ("mo", ML_V_W),
             ("dq", DA_QK_W), ("dk", DA_QK_W), ("dv", DA_V_W),
             ("gs", d), ("gm", d), ("gd", d))
    cols, off = {}, 0
    for name, w in sizes:
        cols[name] = (off, off + w)
        off += w
    return cols, off


def _mixer_in_kernel(h_ref, mod_ref, w_ref, b_ref, wg_ref, bg_ref, cos_ref, sin_ref,
                     sb_o, p_o, mq_o, mk_o, mv_o, smo_o, mg_o, dq_o, dk_o, dv_o, gs_o, gm_o, gd_o):
    d = h_ref.shape[1]
    tm = h_ref.shape[0]
    cols, _ = _main_cols(d)
    h = h_ref[...]
    shift = mod_ref[0, 3:4, :]
    scale = mod_ref[0, 4:5, :]
    u = (h * (1.0 + scale) + shift).astype(BF16)

    def proj(name):
        lo, hi = cols[name]
        return _dot(u, w_ref[:, lo:hi]) + b_ref[:, lo:hi]

    sb_o[...] = proj("sb").astype(BF16)
    p_o[...] = (proj("sc") * proj("sx")).astype(BF16)
    mq_o[...] = proj("mq").astype(BF16)
    mk_o[...] = (proj("mk") * (ML_DQK ** -0.5)).astype(BF16)
    mv_o[...] = proj("mv").astype(BF16)
    smo_o[...] = jax.nn.sigmoid(proj("mo")).astype(BF16)
    mg_o[...] = (_dot(u, wg_ref[...]) + bg_ref[...])[:, :N_GATES]

    cos_t = cos_ref[...]
    sin_t = sin_ref[...]
    lane = lax.broadcasted_iota(jnp.int32, cos_t.shape, 1)
    first_half = (lane % DA_HALF) < (DA_HALF // 2)

    def rope_store(z, out_ref, mult):
        for k in range(z.shape[1] // LANES):
            x = z[:, k * LANES:(k + 1) * LANES]
            partner = jnp.where(first_half,
                                pltpu.roll(x, LANES - DA_HALF // 2, 1),
                                pltpu.roll(x, DA_HALF // 2, 1))
            out_ref[:, k * LANES:(k + 1) * LANES] = ((x * cos_t + partner * sin_t) * mult).astype(BF16)

    rope_store(proj("dq"), dq_o, (DA_HALF ** -0.5) * LOG2_E)
    rope_store(proj("dk"), dk_o, 1.0)
    dv = proj("dv").astype(BF16)
    ones = jnp.ones((tm, DA_DV), BF16)
    for k in range(DA_HEADS):
        dv_o[:, 2 * k * DA_DV:(2 * k + 1) * DA_DV] = dv[:, k * DA_DV:(k + 1) * DA_DV]
        dv_o[:, (2 * k + 1) * DA_DV:(2 * k + 2) * DA_DV] = ones
    gs_o[...] = jax.nn.sigmoid(proj("gs")).astype(BF16)
    gm_o[...] = jax.nn.sigmoid(proj("gm")).astype(BF16)
    gd_o[...] = jax.nn.sigmoid(proj("gd")).astype(BF16)


def _mixer_in(h, mods3, w_main, b_main, w_gate, b_gate, cos_t, sin_t, *, seq, batch, n_lat, tm=256):
    rows, d = h.shape
    _, n_main = _main_cols(d)
    tm = min(tm, seq)
    lat_tiles = n_lat // tm
    rope_blocks = seq // tm

    def row_spec(width):
        return pl.BlockSpec((tm, width), lambda i: (i, 0))

    def rope_map(i):
        return (jnp.where(i < lat_tiles, i % rope_blocks, rope_blocks), 0)

    widths = (SC_WIDTH, SC_WIDTH, ML_QK_W, ML_QK_W, ML_V_W, ML_V_W, N_GATES,
              DA_QK_W, DA_QK_W, 2 * DA_V_W, d, d, d)
    dtypes = (BF16,) * 6 + (F32,) + (BF16,) * 6
    return pl.pallas_call(
        _mixer_in_kernel,
        grid=(rows // tm,),
        in_specs=[row_spec(d),
                  pl.BlockSpec((1, N_MOD, d), lambda i: (jnp.minimum(i * tm // seq, batch), 0, 0)),
                  _resident((d, n_main)),
                  _resident((1, n_main)),
                  _resident((d, LANES)),
                  _resident((1, LANES)),
                  pl.BlockSpec((tm, LANES), rope_map),
                  pl.BlockSpec((tm, LANES), rope_map)],
        out_specs=[row_spec(w) for w in widths],
        out_shape=[jax.ShapeDtypeStruct((rows, w), dt) for w, dt in zip(widths, dtypes)],
        compiler_params=_params(1),
        name="mixer_in",
    )(h, mods3, w_main, b_main, w_gate, b_gate, cos_t, sin_t)


def _log_sigmoid(x):
    return jnp.minimum(x, 0.0) - jnp.log(1.0 + jnp.exp(-jnp.abs(x)))


def _split3(x):
    pieces, rest = [], x
    for _ in range(3):
        piece = rest.astype(BF16)
        pieces.append(piece)
        rest = rest - piece.astype(F32)
    return pieces


def _mlstm_gate_kernel(g_ref, o_ref, o3_ref):
    blk = ML_BLOCK
    t_idx = lax.broadcasted_iota(jnp.int32, (blk, blk), 0)
    s_idx = lax.broadcasted_iota(jnp.int32, (blk, blk), 1)
    tril = (s_idx <= t_idx).astype(BF16)
    lane = lax.broadcasted_iota(jnp.int32, (blk, N_GATES), 1)
    src = lax.broadcasted_iota(jnp.int32, (N_GATES, LANES), 0)
    dst = lax.broadcasted_iota(jnp.int32, (N_GATES, LANES), 1)
    places = [(dst == src + term * N_GATES).astype(BF16) for term in range(3)]
    for j in range(g_ref.shape[0] // blk):
        sl = pl.ds(j * blk, blk)
        g = g_ref[sl, :]
        ls = _log_sigmoid(g)
        prefix = functools.reduce(jnp.add, [_dot(tril, piece) for piece in _split3(ls)])
        suffix = prefix[blk - 1:blk, :] - prefix + ls
        out = jnp.where(lane < 2 * ML_HEADS, g, jnp.where(lane < 3 * ML_HEADS, prefix, suffix))
        o_ref[sl, :] = out
        placed = functools.reduce(jnp.add, [_dot(piece, place) for piece, place in zip(_split3(out), places)])
        o3_ref[sl, :] = placed.astype(BF16)


def _mlstm_gates(mg):
    rows = mg.shape[0]
    step = math.gcd(rows, 8 * ML_BLOCK)
    return pl.pallas_call(
        _mlstm_gate_kernel,
        grid=(rows // step,),
        in_specs=[pl.BlockSpec((step, N_GATES), lambda i: (i, 0))],
        out_specs=[pl.BlockSpec((step, N_GATES), lambda i: (i, 0)),
                   pl.BlockSpec((step, LANES), lambda i: (i, 0))],
        out_shape=[jax.ShapeDtypeStruct((rows, N_GATES), F32),
                   jax.ShapeDtypeStruct((rows, LANES), BF16)],
        compiler_params=_params(1),
        name="mlstm_gates",
    )(mg)


def _mlstm_scan_kernel(ql_ref, qc_ref, kl_ref, kc_ref, ktl_ref, ktc_ref, vl_ref, vc_ref,
                       g3l_ref, g3c_ref, grl_ref, grc_ref, *rest, need_ctx):
    if need_ctx:
        hl_ref, hc_ref, c_in, n_in, m_in = rest
    else:
        hl_ref, c_in, n_in, m_in = rest
        hc_ref = None
    blk = ML_BLOCK
    n_lat_blk = ql_ref.shape[0] // blk
    n_ctx_blk = qc_ref.shape[0] // blk
    head = pl.program_id(1)

    t_idx = lax.broadcasted_iota(jnp.int32, (blk, blk), 0)
    s_idx = lax.broadcasted_iota(jnp.int32, (blk, blk), 1)
    lane_t = lax.broadcasted_iota(jnp.int32, (1, blk), 1)
    ones_k = jnp.ones((blk, LANES), BF16)
    sel_src = lax.broadcasted_iota(jnp.int32, (LANES, 2 * LANES), 0)
    sel_dst = lax.broadcasted_iota(jnp.int32, (LANES, 2 * LANES), 1)
    sel_col = jnp.where(sel_dst < LANES, 2 * ML_HEADS + head, 3 * ML_HEADS + head)
    sel = jnp.logical_and(sel_src % N_GATES == sel_col, sel_src < 3 * N_GATES).astype(BF16)

    def gate_rows(gr_ref, j, bwd):
        i_idx = head + (ML_HEADS if bwd else 0)
        f_idx = head + (3 * ML_HEADS if bwd else 2 * ML_HEADS)
        return gr_ref[j, pl.ds(f_idx, 1), :], gr_ref[j, pl.ds(i_idx, 1), :]

    def scan_direction(bwd):
        d = 1 if bwd else 0
        order = [(grc_ref, ktc_ref, vc_ref, j, j) for j in range(n_ctx_blk)]
        lat = [(grl_ref, ktl_ref, vl_ref, j, n_ctx_blk + j) for j in range(n_lat_blk)]
        order = (order[::-1] + lat[::-1]) if bwd else (order + lat)
        m_prev = jnp.zeros((1, 1), F32)
        c_run = jnp.zeros((ML_DQK, ML_DV), F32)
        n_run = jnp.zeros((ML_DQK, LANES), F32)
        for gr_ref, kt_ref, v_ref, j, slot in order:
            f_row, i_row = gate_rows(gr_ref, j, bwd)
            b_end = jnp.sum(jnp.where(lane_t == (0 if bwd else blk - 1), f_row, 0.0), axis=1, keepdims=True)
            g_log = b_end + (i_row - f_row)
            m_new = jnp.maximum(b_end + m_prev, jnp.max(g_log, axis=1, keepdims=True))
            a_prev = jnp.exp(b_end + m_prev - m_new)
            kw = (kt_ref[j].astype(F32) * jnp.exp(g_log - m_new)).astype(BF16)
            v = v_ref[j * blk:(j + 1) * blk, :]
            c_in[slot, d * ML_DQK:(d + 1) * ML_DQK, :] = c_run.astype(BF16)
            n_in[slot, :, d * LANES:(d + 1) * LANES] = n_run.astype(BF16)
            m_in[slot, :, d * LANES:(d + 1) * LANES] = jnp.broadcast_to(m_prev, (8, LANES))
            c_run = a_prev * c_run + _dot(kw, v)
            n_run = a_prev * n_run + _dot(kw, ones_k)
            m_prev = m_new

    def outputs(q, k, v, g3, rows, slot):
        qk = _dot_nt(q, k)
        f_both = _dot(g3, sel)
        qn_both = _dot(q, n_in[slot])
        m_prev_both = m_in[slot][0:1, :]
        qf = q.astype(F32)
        scaled, inter_w = [], []
        for d, bwd in enumerate((False, True)):
            f_row, i_row = rows[d]
            f_rep = f_both[:, d * LANES:(d + 1) * LANES]
            mask = (s_idx >= t_idx) if bwd else (s_idx <= t_idx)
            d_log = jnp.where(mask, _wide(f_rep, blk) + (i_row - f_row), NEG_BIG)
            m_loc = jnp.broadcast_to(jnp.max(d_log, axis=1, keepdims=True), (blk, LANES))
            s = qk * jnp.exp(d_log - _wide(m_loc, blk))
            den_loc = jnp.broadcast_to(jnp.sum(s, axis=1, keepdims=True), (blk, LANES))
            m_inter = f_rep + m_prev_both[:, d * LANES:(d + 1) * LANES]
            m_t = jnp.maximum(m_inter, m_loc)
            w_inter = jnp.exp(m_inter - m_t)
            w_loc = jnp.exp(m_loc - m_t)
            den = w_inter * qn_both[:, d * LANES:(d + 1) * LANES] + w_loc * den_loc
            inv = 1.0 / jnp.maximum(jnp.abs(den), jnp.exp(-m_t))
            scaled.append(s * _wide(w_loc * inv, blk))
            inter_w.append((qf * (w_inter * inv)).astype(BF16))
        x = jnp.concatenate(inter_w, axis=1)
        return _dot(x, c_in[slot]) + _dot((scaled[0] + scaled[1]).astype(BF16), v)

    def lat_rows(j):
        return pl.ds(pl.multiple_of(j * blk, blk), blk)

    scan_direction(False)
    scan_direction(True)

    if need_ctx:
        for j in range(n_ctx_blk):
            sl = pl.ds(j * blk, blk)
            rows = (gate_rows(grc_ref, j, False), gate_rows(grc_ref, j, True))
            hc_ref[sl, :] = outputs(qc_ref[sl, :], kc_ref[sl, :], vc_ref[sl, :], g3c_ref[sl, :], rows,
                                    j).astype(hc_ref.dtype)

    def out_body(j, carry):
        sl = lat_rows(j)
        rows = (gate_rows(grl_ref, j, False), gate_rows(grl_ref, j, True))
        hl_ref[sl, :] = outputs(ql_ref[sl, :], kl_ref[sl, :], vl_ref[sl, :], g3l_ref[sl, :], rows,
                                n_ctx_blk + j).astype(hl_ref.dtype)
        return carry

    lax.fori_loop(0, n_lat_blk, out_body, 0)


def _mlstm_scan(mq, mk, mv, mg, *, seq, ctx_len, batch, n_lat, need_ctx):
    blk = ML_BLOCK
    rows = mq.shape[0]
    ctx0 = n_lat // ctx_len
    n_blk = (seq + ctx_len) // blk
    gates, gates3 = _mlstm_gates(mg)
    gates_rows = gates.T.reshape(N_GATES, rows // blk, blk).transpose(1, 0, 2)
    mk_t = mk.reshape(rows // blk, blk, ML_QK_W).transpose(0, 2, 1)

    def lat(width):
        return pl.BlockSpec((seq, width), lambda b, h: (b, h))

    def ctx(width):
        return pl.BlockSpec((ctx_len, width), lambda b, h: (ctx0 + b, h))

    in_specs = [lat(ML_DQK), ctx(ML_DQK), lat(ML_DQK), ctx(ML_DQK),
                pl.BlockSpec((seq // blk, ML_DQK, blk), lambda b, h: (b, h, 0)),
                pl.BlockSpec((ctx_len // blk, ML_DQK, blk), lambda b, h: (ctx0 + b, h, 0)),
                lat(ML_DV), ctx(ML_DV),
                pl.BlockSpec((seq, LANES), lambda b, h: (b, 0)),
                pl.BlockSpec((ctx_len, LANES), lambda b, h: (ctx0 + b, 0)),
                pl.BlockSpec((seq // blk, N_GATES, blk), lambda b, h: (b, 0, 0)),
                pl.BlockSpec((ctx_len // blk, N_GATES, blk), lambda b, h: (ctx0 + b, 0, 0))]
    out_specs = [pl.BlockSpec((seq, ML_DV), lambda b, h: (b, h))]
    out_shape = [jax.ShapeDtypeStruct((n_lat, ML_V_W), BF16)]
    if need_ctx:
        out_specs.append(pl.BlockSpec((ctx_len, ML_DV), lambda b, h: (b, h)))
        out_shape.append(jax.ShapeDtypeStruct((batch * ctx_len, ML_V_W), BF16))
    scratch = [pltpu.VMEM((n_blk, 2 * ML_DQK, ML_DV), BF16),
               pltpu.VMEM((n_blk, ML_DQK, 2 * LANES), BF16),
               pltpu.VMEM((n_blk, 8, 2 * LANES), F32)]
    out = pl.pallas_call(
        functools.partial(_mlstm_scan_kernel, need_ctx=need_ctx),
        grid=(batch, ML_HEADS),
        in_specs=in_specs,
        out_specs=out_specs,
        out_shape=out_shape,
        scratch_shapes=scratch,
        compiler_params=_params(2),
        name="mlstm",
    )(mq, mq, mk, mk, mk_t, mk_t, mv, mv, gates3, gates3, gates_rows, gates_rows)
    return out if need_ctx else (out[0], None)


def _diffattn_kernel(lam_ref, g_ref, q_ref, *refs, n_seg, lam_init):
    k_refs = refs[:n_seg]
    v_refs = refs[n_seg:2 * n_seg]
    o_ref = refs[2 * n_seg]
    lv = lam_ref[...]
    lam = (jnp.exp(jnp.sum(lv[0:1] * lv[1:2], axis=1, keepdims=True))
           - jnp.exp(jnp.sum(lv[2:3] * lv[3:4], axis=1, keepdims=True)) + lam_init)
    q = q_ref[...]
    lane = lax.broadcasted_iota(jnp.int32, q.shape, 1)
    zero = jnp.zeros_like(q)
    halves = [jnp.where(lane < DA_HALF, q, zero), jnp.where(lane >= DA_HALF, q, zero)]
    scores = [[_dot_nt(qh, k_ref[...]) for k_ref in k_refs] for qh in halves]
    maxes = [functools.reduce(jnp.maximum, [jnp.max(s, axis=1, keepdims=True) for s in ss]) for ss in scores]
    probs = [[jnp.exp2((s - m).astype(BF16)) for s in ss] for ss, m in zip(scores, maxes)]
    heads = []
    for ps in probs:
        acc = None
        for p, v_ref in zip(ps, v_refs):
            part = _dot(p, v_ref[...])
            acc = part if acc is None else acc + part
        heads.append(acc[:, :DA_DV] * (1.0 / acc[:, DA_DV:]))
    o = heads[0] - lam * heads[1]
    ms = jnp.mean(o * o, axis=1, keepdims=True)
    o_ref[...] = (o * lax.rsqrt(ms + LN_EPS) * g_ref[...] * (1.0 - lam_init)).astype(o_ref.dtype)


def _diffattn(dq, dk, dv, lam_vecs, da_g, *, q_row0, q_len, segments, batch, lam_init, tq=512):
    tq = min(tq, q_len)
    nq = q_len // tq
    q0 = q_row0 // tq

    def seg_spec(row0, length, width):
        return pl.BlockSpec((length, width), lambda b, h, i: (row0 // length + b, h))

    return pl.pallas_call(
        functools.partial(_diffattn_kernel, n_seg=len(segments), lam_init=lam_init),
        grid=(batch, DA_HEADS, nq),
        in_specs=[pl.BlockSpec(lam_vecs.shape, lambda b, h, i: (0, 0)),
                  pl.BlockSpec((1, DA_DV), lambda b, h, i: (0, 0)),
                  pl.BlockSpec((tq, LANES), lambda b, h, i: (q0 + b * nq + i, h))]
                 + [seg_spec(r0, ln, LANES) for r0, ln in segments]
                 + [seg_spec(r0, ln, 2 * DA_DV) for r0, ln in segments],
        out_specs=pl.BlockSpec((tq, DA_DV), lambda b, h, i: (b * nq + i, h)),
        out_shape=jax.ShapeDtypeStruct((batch * q_len, DA_V_W), BF16),
        compiler_params=_params(3),
        name="diffattn",
    )(lam_vecs, da_g.reshape(1, DA_DV), dq, *([dk] * len(segments)), *([dv] * len(segments)))


def _mixer_out_kernel(h_ref, mod_ref, sb_ref, p_ref, pprev_ref, pnext_ref, smo_ref, *refs,
                      seq, ctx_len, n_lat, alpha, two_source):
    tm = h_ref.shape[0]
    r0 = pl.program_id(0) * tm
    is_lat = r0 < n_lat
    if two_source:
        hml_l, hml_c, yda_l, yda_c = refs[:4]
        refs = refs[4:]
        hml = jnp.where(is_lat, hml_l[...], hml_c[...])
        yda = jnp.where(is_lat, yda_l[...], yda_c[...])
    else:
        hml, yda = refs[0][...], refs[1][...]
        refs = refs[2:]
    gs_ref, gm_ref, gd_ref, convw_ref, wsc_ref, wml_ref, wda_ref, wo_ref, lng_ref, lnb_ref, o_ref = refs
    pos = jnp.where(is_lat, r0 % seq, (r0 - n_lat) % ctx_len)
    seq_len = jnp.where(is_lat, seq, ctx_len)
    keep_prev = (pos != 0).astype(F32)
    keep_next = (pos + tm != seq_len).astype(F32)

    p = p_ref[...].astype(F32)
    prev_row = pprev_ref[...].astype(F32)[BF16_SUBLANES - 1:BF16_SUBLANES, :] * keep_prev
    next_row = pnext_ref[...].astype(F32)[0:1, :] * keep_next
    row = lax.broadcasted_iota(jnp.int32, p.shape, 0)
    p_before = jnp.where(row == 0, prev_row, pltpu.roll(p, 1, 0))
    p_after = jnp.where(row == tm - 1, next_row, pltpu.roll(p, tm - 1, 0))
    cw = convw_ref[...]
    conv = cw[0:1, :] * p_before + cw[1:2, :] * p + cw[2:3, :] * p_after
    y_sc = (sb_ref[...].astype(F32) * conv).astype(BF16)
    y_ml = smo_ref[...] * hml
    y = (gs_ref[...].astype(F32) * _dot(y_sc, wsc_ref[...])
         + gm_ref[...].astype(F32) * _dot(y_ml, wml_ref[...])
         + gd_ref[...].astype(F32) * _dot(yda, wda_ref[...]))
    y = _dot(y.astype(BF16), wo_ref[...])
    h = h_ref[...]
    o_ref[...] = _layer_norm(alpha * h + mod_ref[0, 5:6, :] * y, lng_ref[...], lnb_ref[...])


def _mixer_out(h, mods3, sb, p, smo, hml, yda, gs, gm, gd, conv_w, w_sc, w_ml, w_da, w_o, ln_g, ln_b,
               *, n_rows, seq, ctx_len, batch, n_lat, alpha, hml_ctx=None, yda_ctx=None, tm=256):
    d = h.shape[1]
    tm = min(tm, ctx_len)
    halo = BF16_SUBLANES
    last_halo = p.shape[0] // halo - 1
    two_source = hml_ctx is not None

    def row_spec(width):
        return pl.BlockSpec((tm, width), lambda i: (i, 0))

    if two_source:
        branch = [hml, hml_ctx, yda, yda_ctx]
        branch_specs = (_two_source_specs(tm, ML_V_W, n_lat // tm) + _two_source_specs(tm, DA_V_W, n_lat // tm))
    else:
        branch = [hml, yda]
        branch_specs = [row_spec(ML_V_W), row_spec(DA_V_W)]
    kernel = functools.partial(_mixer_out_kernel, seq=seq, ctx_len=ctx_len, n_lat=n_lat, alpha=alpha,
                               two_source=two_source)
    return pl.pallas_call(
        kernel,
        grid=(n_rows // tm,),
        in_specs=[row_spec(d),
                  pl.BlockSpec((1, N_MOD, d), lambda i: (jnp.minimum(i * tm // seq, batch), 0, 0)),
                  row_spec(SC_WIDTH), row_spec(SC_WIDTH),
                  pl.BlockSpec((halo, SC_WIDTH), lambda i: (jnp.maximum(i * (tm // halo) - 1, 0), 0)),
                  pl.BlockSpec((halo, SC_WIDTH),
                               lambda i: (jnp.minimum((i + 1) * (tm // halo), last_halo), 0)),
                  row_spec(ML_V_W)] + branch_specs + [
                  row_spec(d), row_spec(d), row_spec(d),
                  _resident((SC_KSIZE, SC_WIDTH)),
                  _resident((SC_WIDTH, d)), _resident((ML_V_W, d)), _resident((DA_V_W, d)),
                  _resident((d, d)),
                  pl.BlockSpec((1, d), lambda i: (0, 0)),
                  pl.BlockSpec((1, d), lambda i: (0, 0))],
        out_specs=row_spec(d),
        out_shape=jax.ShapeDtypeStruct((n_rows, d), F32),
        compiler_params=_params(1),
        name="mixer_out",
    )(h, mods3, sb, p, p, p, smo, *branch, gs, gm, gd, conv_w, w_sc, w_ml, w_da, w_o,
      ln_g.reshape(1, d), ln_b.reshape(1, d))


def _rope_tables(seq, tile):
    n_freq = DA_HALF // 4
    t = jnp.arange(seq)
    row_ids = (t // GRID_W).astype(F32)
    col_ids = (t % GRID_W).astype(F32)
    inv = ROPE_BASE ** (-jnp.arange(n_freq, dtype=F32) / n_freq)
    ang = jnp.concatenate([row_ids[:, None] * inv, col_ids[:, None] * inv], axis=-1)
    cos, sin = jnp.cos(ang), jnp.sin(ang)
    reps = LANES // DA_HALF
    cos_t = jnp.tile(jnp.concatenate([cos, cos], axis=-1), (1, reps))
    sin_t = jnp.tile(jnp.concatenate([-sin, sin], axis=-1), (1, reps))
    cos_t = jnp.concatenate([cos_t, jnp.ones((tile, LANES), F32)], axis=0)
    sin_t = jnp.concatenate([sin_t, jnp.zeros((tile, LANES), F32)], axis=0)
    return cos_t, sin_t


def kernel(x, c, ctx, c_ctx, w_ada, b_ada, ln_g, ln_b, ffn1_up, ffn1_down, ffn2_up, ffn2_down, w_in, b_in,
           conv_w, w_sc, w_ml, w_da, w_o, lam_q1, lam_k1, lam_q2, lam_k2, da_norm_g):
    batch, seq, d = x.shape
    ctx_len = ctx.shape[1]
    depth = w_ada.shape[0]
    alpha = (2 * depth) ** 0.25
    n_lat = batch * seq
    n_ctx = batch * ctx_len
    mixer_in_tile = min(256, seq)

    n_cond = -(-(batch + 1) // 8) * 8
    cc = jnp.concatenate([c, c_ctx[None, :], jnp.zeros((n_cond - batch - 1, d), F32)], axis=0)
    mods = _ada(cc, w_ada, b_ada).reshape(depth, n_cond, N_MOD, d)

    cos_t, sin_t = _rope_tables(seq, mixer_in_tile)
    gate_lo = 3 * SC_WIDTH + 2 * ML_QK_W + 2 * ML_V_W
    gate_hi = gate_lo + N_GATES

    h, h_ctx = x.reshape(n_lat, d), ctx.reshape(n_ctx, d)
    for l in range(depth):
        last = l == depth - 1
        lam_init = 0.8 - 0.6 * math.exp(-0.3 * l)
        mods3 = mods[l]
        w_main = jnp.concatenate([w_in[l, :, :gate_lo], w_in[l, :, gate_hi:]], axis=1).astype(BF16)
        b_main = jnp.concatenate([b_in[l, :gate_lo], b_in[l, gate_hi:]])[None, :]
        w_gate = jnp.pad(w_in[l, :, gate_lo:gate_hi], ((0, 0), (0, LANES - N_GATES))).astype(BF16)
        b_gate = jnp.pad(b_in[l, gate_lo:gate_hi], (0, LANES - N_GATES))[None, :]
        lam_vecs = jnp.stack([lam_q1[l], lam_k1[l], lam_q2[l], lam_k2[l]]).astype(F32)

        h = _ffn(h, mods3, ffn1_up[l].astype(BF16), ffn1_down[l].astype(BF16), ln_g[l, 0], ln_b[l, 0],
                 mod_base=0, n_rows=n_lat + n_ctx, seq=seq, batch=batch, alpha=alpha, h_ctx=h_ctx)
        h_ctx = None

        (sb, p, mq, mk, mv, smo, mg, dq, dk, dv, gs, gm, gd) = _mixer_in(
            h, mods3, w_main, b_main, w_gate, b_gate, cos_t, sin_t,
            seq=seq, batch=batch, n_lat=n_lat, tm=mixer_in_tile)

        hml, hml_ctx = _mlstm_scan(mq, mk, mv, mg, seq=seq, ctx_len=ctx_len, batch=batch, n_lat=n_lat,
                                   need_ctx=not last)
        yda = _diffattn(dq, dk, dv, lam_vecs, da_norm_g[l], q_row0=0, q_len=seq,
                        segments=((0, seq), (n_lat, ctx_len)), batch=batch, lam_init=lam_init)
        yda_ctx = None
        if not last:
            yda_ctx = _diffattn(dq, dk, dv, lam_vecs, da_norm_g[l], q_row0=n_lat, q_len=ctx_len,
                                segments=((n_lat, ctx_len),), batch=batch, lam_init=lam_init)

        n_rows = n_lat if last else n_lat + n_ctx
        h = _mixer_out(h, mods3, sb, p, smo, hml, yda, gs, gm, gd, conv_w[l],
                       w_sc[l].astype(BF16), w_ml[l].astype(BF16), w_da[l].astype(BF16), w_o[l].astype(BF16),
                       ln_g[l, 1], ln_b[l, 1], hml_ctx=hml_ctx, yda_ctx=yda_ctx,
                       n_rows=n_rows, seq=seq, ctx_len=ctx_len, batch=batch, n_lat=n_lat, alpha=alpha)
        h = _ffn(h, mods3, ffn2_up[l].astype(BF16), ffn2_down[l].astype(BF16), ln_g[l, 2], ln_b[l, 2],
                 mod_base=6, n_rows=n_rows, seq=seq, batch=batch, alpha=alpha)
    return h[:n_lat].reshape(batch, seq, d)
```

```python
import functools
import math

import jax
import jax.numpy as jnp
from jax import lax
from jax.experimental import pallas as pl
from jax.experimental.pallas import tpu as pltpu

GRID_W = 64
N_MOD = 9
SC_WIDTH = 512
SC_KSIZE = 3
ML_HEADS = 4
ML_DQK = 128
ML_DV = 256
DA_HEADS = 4
DA_HALF = 64
DA_DV = 2 * DA_HALF
ROPE_BASE = 10000.0
LN_EPS = 1e-5

ML_QK_W = ML_HEADS * ML_DQK
ML_V_W = ML_HEADS * ML_DV
DA_QK_W = DA_HEADS * 2 * DA_HALF
DA_V_W = DA_HEADS * DA_DV
N_GATES = 4 * ML_HEADS

ML_BLOCK = 256
FFN_COLS = 256
LANES = 128
BF16_SUBLANES = 16
V7X_VMEM_BYTES = 64 * 1024 * 1024
VMEM_LIMIT = V7X_VMEM_BYTES - 8 * 1024 * 1024

F32 = jnp.float32
BF16 = jnp.bfloat16
NEG_BIG = -1e30
LOG2_E = 1.4426950408889634

assert ML_DQK == LANES and DA_DV == LANES and ML_BLOCK == 2 * LANES


def _dot(a, b, precision=None):
    return jnp.dot(a, b, preferred_element_type=F32, precision=precision)


def _dot_nt(a, b):
    return lax.dot_general(a, b, (((1,), (1,)), ((), ())), preferred_element_type=F32)


def _wide(x, width):
    return jnp.concatenate([x] * (width // LANES), axis=1)


def _layer_norm(y, g, b):
    mu = jnp.mean(y, axis=-1, keepdims=True)
    yc = y - mu
    var = jnp.mean(yc * yc, axis=-1, keepdims=True)
    return yc * lax.rsqrt(var + LN_EPS) * g + b


def _resident(shape):
    return pl.BlockSpec(shape, lambda *_: (0,) * len(shape), pipeline_mode=pl.Buffered(1))


def _resident_layer(stacked, layer):
    n = stacked.ndim - 1
    return pl.BlockSpec((None,) + stacked.shape[1:], lambda *_: (layer,) + (0,) * n,
                        pipeline_mode=pl.Buffered(1))


def _params(n_axes):
    return pltpu.CompilerParams(dimension_semantics=("parallel",) * n_axes,
                                vmem_limit_bytes=VMEM_LIMIT)


def _ada_kernel(c_ref, w_ref, b_ref, o_ref):
    c = c_ref[...]
    a = (c * jax.nn.sigmoid(c)).astype(BF16)
    o_ref[0] = _dot(a, w_ref[0].astype(BF16)) + b_ref[0]


def _ada(cc, w_ada, b_ada):
    depth, d, n = w_ada.shape
    tn = d
    return pl.pallas_call(
        _ada_kernel,
        grid=(depth, n // tn),
        in_specs=[pl.BlockSpec(cc.shape, lambda l, j: (0, 0)),
                  pl.BlockSpec((1, d, tn), lambda l, j: (l, 0, j)),
                  pl.BlockSpec((1, 1, tn), lambda l, j: (l, 0, j))],
        out_specs=pl.BlockSpec((1, cc.shape[0], tn), lambda l, j: (l, 0, j)),
        out_shape=jax.ShapeDtypeStruct((depth, cc.shape[0], n), F32),
        compiler_params=_params(2),
        name="ada",
    )(cc, w_ada, b_ada.reshape(depth, 1, n))


def _two_source_specs(tm, width, lat_tiles):
    return [pl.BlockSpec((tm, width), lambda i: (jnp.minimum(i, lat_tiles - 1), 0)),
            pl.BlockSpec((tm, width), lambda i: (jnp.maximum(i - lat_tiles, 0), 0))]


def _ffn_kernel(*refs, mod_base, alpha, lat_tiles):
    if lat_tiles is None:
        h_ref, mod_ref, wup_ref, wdn_ref, lng_ref, lnb_ref, o_ref, g_scr = refs
        h = h_ref[...]
    else:
        h_ref, hc_ref, mod_ref, wup_ref, wdn_ref, lng_ref, lnb_ref, o_ref, g_scr = refs
        h = jnp.where(pl.program_id(0) < lat_tiles, h_ref[...], hc_ref[...])
    f = wdn_ref.shape[0]
    shift = mod_ref[0, mod_base:mod_base + 1, :]
    scale = mod_ref[0, mod_base + 1:mod_base + 2, :]
    gate = mod_ref[0, mod_base + 2:mod_base + 3, :]
    u = (h * (1.0 + scale) + shift).astype(BF16)
    for c in range(f // FFN_COLS):
        lo, hi = c * FFN_COLS, (c + 1) * FFN_COLS
        a = _dot(u, wup_ref[:, lo:hi].astype(BF16))
        v = _dot(u, wup_ref[:, f + lo:f + hi].astype(BF16))
        g_scr[:, lo:hi] = (a * jax.nn.sigmoid(a) * v).astype(BF16)
    d = _dot(g_scr[...], wdn_ref[...].astype(BF16))
    o_ref[...] = _layer_norm(alpha * h + (0.5 * gate) * d, lng_ref[...], lnb_ref[...])


def _ffn(h, mods3, w_up, w_dn, ln_g, ln_b, *, layer, mod_base, n_rows, seq, batch, alpha, h_ctx=None, tm=512):
    d = h.shape[1]
    f = w_dn.shape[1]
    tm = min(tm, seq)
    lat_tiles = None if h_ctx is None else h.shape[0] // tm
    kernel = functools.partial(_ffn_kernel, mod_base=mod_base, alpha=alpha, lat_tiles=lat_tiles)
    if h_ctx is None:
        sources, source_specs = [h], [pl.BlockSpec((tm, d), lambda i: (i, 0))]
    else:
        sources, source_specs = [h, h_ctx], _two_source_specs(tm, d, lat_tiles)
    return pl.pallas_call(
        kernel,
        grid=(n_rows // tm,),
        in_specs=source_specs + [
                  pl.BlockSpec((1, N_MOD, d), lambda i: (jnp.minimum(i * tm // seq, batch), 0, 0)),
                  _resident_layer(w_up, layer),
                  _resident_layer(w_dn, layer),
                  pl.BlockSpec((1, d), lambda i: (0, 0)),
                  pl.BlockSpec((1, d), lambda i: (0, 0))],
        out_specs=pl.BlockSpec((tm, d), lambda i: (i, 0)),
        out_shape=jax.ShapeDtypeStruct((n_rows, d), F32),
        scratch_shapes=[pltpu.VMEM((tm, f), BF16)],
        compiler_params=_params(1),
        name="ffn",
    )(*sources, mods3, w_up, w_dn, ln_g.reshape(1, d), ln_b.reshape(1, d))


GATE_COL0 = 3 * SC_WIDTH + 2 * ML_QK_W + 2 * ML_V_W


def _proj_cols(d):
    parts = ((("sb", SC_WIDTH), ("sc", SC_WIDTH), ("sx", SC_WIDTH),
              ("mq", ML_QK_W), ("mk", ML_QK_W), ("mv", ML_V_W), ("mo", ML_V_W)),
             (("dq", DA_QK_W), ("dk", DA_QK_W), ("dv", DA_V_W), ("gs", d), ("gm", d), ("gd", d)))
    cols, widths = {}, []
    for part, sizes in enumerate(parts):
        off = 0
        for name, w in sizes:
            cols[name] = (part, off, off + w)
            off += w
        widths.append(off)
    return cols, widths


def _mixer_in_kernel(h_ref, mod_ref, wh_ref, bh_ref, wt_ref, bt_ref, wg_ref, bg_ref, cos_ref, sin_ref,
                     sb_o, p_o, mq_o, mk_o, mv_o, smo_o, mg_o, dq_o, dk_o, dv_o, gs_o, gm_o, gd_o):
    d = h_ref.shape[1]
    tm = h_ref.shape[0]
    cols, _ = _proj_cols(d)
    h = h_ref[...]
    shift = mod_ref[0, 3:4, :]
    scale = mod_ref[0, 4:5, :]
    u = (h * (1.0 + scale) + shift).astype(BF16)

    def proj(name):
        part, lo, hi = cols[name]
        w_ref, b_ref = ((wh_ref, bh_ref), (wt_ref, bt_ref))[part]
        return _dot(u, w_ref[:, lo:hi]) + b_ref[:, lo:hi]

    sb_o[...] = proj("sb").astype(BF16)
    p_o[...] = (proj("sc") * proj("sx")).astype(BF16)
    mq_o[...] = proj("mq").astype(BF16)
    mk_o[...] = (proj("mk") * (ML_DQK ** -0.5)).astype(BF16)
    mv_o[...] = proj("mv").astype(BF16)
    smo_o[...] = jax.nn.sigmoid(proj("mo")).astype(BF16)
    mg_o[...] = (_dot(u, wg_ref[...]) + bg_ref[...])[:, :N_GATES]

    cos_t = cos_ref[...]
    sin_t = sin_ref[...]
    lane = lax.broadcasted_iota(jnp.int32, cos_t.shape, 1)
    first_half = (lane % DA_HALF) < (DA_HALF // 2)

    def rope_store(z, out_ref, mult):
        for k in range(z.shape[1] // LANES):
            x = z[:, k * LANES:(k + 1) * LANES]
            partner = jnp.where(first_half,
                                pltpu.roll(x, LANES - DA_HALF // 2, 1),
                                pltpu.roll(x, DA_HALF // 2, 1))
            out_ref[:, k * LANES:(k + 1) * LANES] = ((x * cos_t + partner * sin_t) * mult).astype(BF16)

    rope_store(proj("dq"), dq_o, (DA_HALF ** -0.5) * LOG2_E)
    rope_store(proj("dk"), dk_o, 1.0)
    dv = proj("dv").astype(BF16)
    ones = jnp.ones((tm, DA_DV), BF16)
    for k in range(DA_HEADS):
        dv_o[:, 2 * k * DA_DV:(2 * k + 1) * DA_DV] = dv[:, k * DA_DV:(k + 1) * DA_DV]
        dv_o[:, (2 * k + 1) * DA_DV:(2 * k + 2) * DA_DV] = ones
    gs_o[...] = jax.nn.sigmoid(proj("gs")).astype(BF16)
    gm_o[...] = jax.nn.sigmoid(proj("gm")).astype(BF16)
    gd_o[...] = jax.nn.sigmoid(proj("gd")).astype(BF16)


def _mixer_in(h, mods3, w_head, b_head, w_tail, b_tail, w_gate, b_gate, cos_t, sin_t,
              *, seq, batch, n_lat, tm=256):
    rows, d = h.shape
    _, (n_head, n_tail) = _proj_cols(d)
    tm = min(tm, seq)
    lat_tiles = n_lat // tm
    rope_blocks = seq // tm

    def row_spec(width):
        return pl.BlockSpec((tm, width), lambda i: (i, 0))

    def rope_map(i):
        return (jnp.where(i < lat_tiles, i % rope_blocks, rope_blocks), 0)

    widths = (SC_WIDTH, SC_WIDTH, ML_QK_W, ML_QK_W, ML_V_W, ML_V_W, N_GATES,
              DA_QK_W, DA_QK_W, 2 * DA_V_W, d, d, d)
    dtypes = (BF16,) * 6 + (F32,) + (BF16,) * 6
    return pl.pallas_call(
        _mixer_in_kernel,
        grid=(rows // tm,),
        in_specs=[row_spec(d),
                  pl.BlockSpec((1, N_MOD, d), lambda i: (jnp.minimum(i * tm // seq, batch), 0, 0)),
                  _resident((d, n_head)),
                  _resident((1, n_head)),
                  _resident((d, n_tail)),
                  _resident((1, n_tail)),
                  _resident((d, LANES)),
                  _resident((1, LANES)),
                  pl.BlockSpec((tm, LANES), rope_map),
                  pl.BlockSpec((tm, LANES), rope_map)],
        out_specs=[row_spec(w) for w in widths],
        out_shape=[jax.ShapeDtypeStruct((rows, w), dt) for w, dt in zip(widths, dtypes)],
        compiler_params=_params(1),
        name="mixer_in",
    )(h, mods3, w_head, b_head, w_tail, b_tail, w_gate, b_gate, cos_t, sin_t)


def _log_sigmoid(x):
    return jnp.minimum(x, 0.0) - jnp.log(1.0 + jnp.exp(-jnp.abs(x)))


def _split3(x):
    pieces, rest = [], x
    for _ in range(3):
        piece = rest.astype(BF16)
        pieces.append(piece)
        rest = rest - piece.astype(F32)
    return pieces


def _mlstm_gate_kernel(g_ref, o_ref, o3_ref):
    blk = ML_BLOCK
    t_idx = lax.broadcasted_iota(jnp.int32, (blk, blk), 0)
    s_idx = lax.broadcasted_iota(jnp.int32, (blk, blk), 1)
    tril = (s_idx <= t_idx).astype(BF16)
    lane = lax.broadcasted_iota(jnp.int32, (blk, N_GATES), 1)
    src = lax.broadcasted_iota(jnp.int32, (N_GATES, LANES), 0)
    dst = lax.broadcasted_iota(jnp.int32, (N_GATES, LANES), 1)
    places = [(dst == src + term * N_GATES).astype(BF16) for term in range(3)]
    for j in range(g_ref.shape[0] // blk):
        sl = pl.ds(j * blk, blk)
        g = g_ref[sl, :]
        ls = _log_sigmoid(g)
        prefix = functools.reduce(jnp.add, [_dot(tril, piece) for piece in _split3(ls)])
        suffix = prefix[blk - 1:blk, :] - prefix + ls
        out = jnp.where(lane < 2 * ML_HEADS, g, jnp.where(lane < 3 * ML_HEADS, prefix, suffix))
        o_ref[sl, :] = out
        placed = functools.reduce(jnp.add, [_dot(piece, place) for piece, place in zip(_split3(out), places)])
        o3_ref[sl, :] = placed.astype(BF16)


def _mlstm_gates(mg):
    rows = mg.shape[0]
    step = math.gcd(rows, 8 * ML_BLOCK)
    return pl.pallas_call(
        _mlstm_gate_kernel,
        grid=(rows // step,),
        in_specs=[pl.BlockSpec((step, N_GATES), lambda i: (i, 0))],
        out_specs=[pl.BlockSpec((step, N_GATES), lambda i: (i, 0)),
                   pl.BlockSpec((step, LANES), lambda i: (i, 0))],
        out_shape=[jax.ShapeDtypeStruct((rows, N_GATES), F32),
                   jax.ShapeDtypeStruct((rows, LANES), BF16)],
        compiler_params=_params(1),
        name="mlstm_gates",
    )(mg)


def _mlstm_scan_kernel(ql_ref, qc_ref, kl_ref, kc_ref, ktl_ref, ktc_ref, vl_ref, vc_ref,
                       g3l_ref, g3c_ref, grl_ref, grc_ref, *rest, need_ctx):
    if need_ctx:
        hl_ref, hc_ref, c_in, n_in, m_in = rest
    else:
        hl_ref, c_in, n_in, m_in = rest
        hc_ref = None
    blk = ML_BLOCK
    n_lat_blk = ql_ref.shape[0] // blk
    n_ctx_blk = qc_ref.shape[0] // blk
    head = pl.program_id(1)

    t_idx = lax.broadcasted_iota(jnp.int32, (blk, blk), 0)
    s_idx = lax.broadcasted_iota(jnp.int32, (blk, blk), 1)
    lane_t = lax.broadcasted_iota(jnp.int32, (1, blk), 1)
    ones_k = jnp.ones((blk, LANES), BF16)
    sel_src = lax.broadcasted_iota(jnp.int32, (LANES, 2 * LANES), 0)
    sel_dst = lax.broadcasted_iota(jnp.int32, (LANES, 2 * LANES), 1)
    sel_col = jnp.where(sel_dst < LANES, 2 * ML_HEADS + head, 3 * ML_HEADS + head)
    sel = jnp.logical_and(sel_src % N_GATES == sel_col, sel_src < 3 * N_GATES).astype(BF16)

    def gate_rows(gr_ref, j, bwd):
        i_idx = head + (ML_HEADS if bwd else 0)
        f_idx = head + (3 * ML_HEADS if bwd else 2 * ML_HEADS)
        return gr_ref[j, pl.ds(f_idx, 1), :], gr_ref[j, pl.ds(i_idx, 1), :]

    def scan_direction(bwd):
        d = 1 if bwd else 0
        order = [(grc_ref, ktc_ref, vc_ref, j, j) for j in range(n_ctx_blk)]
        lat = [(grl_ref, ktl_ref, vl_ref, j, n_ctx_blk + j) for j in range(n_lat_blk)]
        order = (order[::-1] + lat[::-1]) if bwd else (order + lat)
        m_prev = jnp.zeros((1, 1), F32)
        c_run = jnp.zeros((ML_DQK, ML_DV), F32)
        n_run = jnp.zeros((ML_DQK, LANES), F32)
        for gr_ref, kt_ref, v_ref, j, slot in order:
            f_row, i_row = gate_rows(gr_ref, j, bwd)
            b_end = jnp.sum(jnp.where(lane_t == (0 if bwd else blk - 1), f_row, 0.0), axis=1, keepdims=True)
            g_log = b_end + (i_row - f_row)
            m_new = jnp.maximum(b_end + m_prev, jnp.max(g_log, axis=1, keepdims=True))
            a_prev = jnp.exp(b_end + m_prev - m_new)
            kw = (kt_ref[j].astype(F32) * jnp.exp(g_log - m_new)).astype(BF16)
            v = v_ref[j * blk:(j + 1) * blk, :]
            c_in[slot, d * ML_DQK:(d + 1) * ML_DQK, :] = c_run.astype(BF16)
            n_in[slot, :, d * LANES:(d + 1) * LANES] = n_run.astype(BF16)
            m_in[slot, :, d * LANES:(d + 1) * LANES] = jnp.broadcast_to(m_prev, (8, LANES))
            c_run = a_prev * c_run + _dot(kw, v)
            n_run = a_prev * n_run + _dot(kw, ones_k)
            m_prev = m_new

    def outputs(q, k, v, g3, rows, slot):
        qk = _dot_nt(q, k)
        f_both = _dot(g3, sel)
        qn_both = _dot(q, n_in[slot])
        m_prev_both = m_in[slot][0:1, :]
        qf = q.astype(F32)
        scaled, inter_w = [], []
        for d, bwd in enumerate((False, True)):
            f_row, i_row = rows[d]
            f_rep = f_both[:, d * LANES:(d + 1) * LANES]
            mask = (s_idx >= t_idx) if bwd else (s_idx <= t_idx)
            d_log = jnp.where(mask, _wide(f_rep, blk) + (i_row - f_row), NEG_BIG)
            m_loc = jnp.broadcast_to(jnp.max(d_log, axis=1, keepdims=True), (blk, LANES))
            s = qk * jnp.exp(d_log - _wide(m_loc, blk))
            den_loc = jnp.broadcast_to(jnp.sum(s, axis=1, keepdims=True), (blk, LANES))
            m_inter = f_rep + m_prev_both[:, d * LANES:(d + 1) * LANES]
            m_t = jnp.maximum(m_inter, m_loc)
            w_inter = jnp.exp(m_inter - m_t)
            w_loc = jnp.exp(m_loc - m_t)
            den = w_inter * qn_both[:, d * LANES:(d + 1) * LANES] + w_loc * den_loc
            inv = 1.0 / jnp.maximum(jnp.abs(den), jnp.exp(-m_t))
            scaled.append(s * _wide(w_loc * inv, blk))
            inter_w.append((qf * (w_inter * inv)).astype(BF16))
        x = jnp.concatenate(inter_w, axis=1)
        return _dot(x, c_in[slot]) + _dot((scaled[0] + scaled[1]).astype(BF16), v)

    def lat_rows(j):
        return pl.ds(pl.multiple_of(j * blk, blk), blk)

    scan_direction(False)
    scan_direction(True)

    if need_ctx:
        for j in range(n_ctx_blk):
            sl = pl.ds(j * blk, blk)
            rows = (gate_rows(grc_ref, j, False), gate_rows(grc_ref, j, True))
            hc_ref[sl, :] = outputs(qc_ref[sl, :], kc_ref[sl, :], vc_ref[sl, :], g3c_ref[sl, :], rows,
                                    j).astype(hc_ref.dtype)

    def out_body(j, carry):
        sl = lat_rows(j)
        rows = (gate_rows(grl_ref, j, False), gate_rows(grl_ref, j, True))
        hl_ref[sl, :] = outputs(ql_ref[sl, :], kl_ref[sl, :], vl_ref[sl, :], g3l_ref[sl, :], rows,
                                n_ctx_blk + j).astype(hl_ref.dtype)
        return carry

    lax.fori_loop(0, n_lat_blk, out_body, 0)


def _mlstm_scan(mq, mk, mv, mg, *, seq, ctx_len, batch, n_lat, need_ctx):
    blk = ML_BLOCK
    rows = mq.shape[0]
    ctx0 = n_lat // ctx_len
    n_blk = (seq + ctx_len) // blk
    gates, gates3 = _mlstm_gates(mg)
    gates_rows = gates.T.reshape(N_GATES, rows // blk, blk).transpose(1, 0, 2)
    mk_t = mk.reshape(rows // blk, blk, ML_QK_W).transpose(0, 2, 1)

    def lat(width):
        return pl.BlockSpec((seq, width), lambda b, h: (b, h))

    def ctx(width):
        return pl.BlockSpec((ctx_len, width), lambda b, h: (ctx0 + b, h))

    in_specs = [lat(ML_DQK), ctx(ML_DQK), lat(ML_DQK), ctx(ML_DQK),
                pl.BlockSpec((seq // blk, ML_DQK, blk), lambda b, h: (b, h, 0)),
                pl.BlockSpec((ctx_len // blk, ML_DQK, blk), lambda b, h: (ctx0 + b, h, 0)),
                lat(ML_DV), ctx(ML_DV),
                pl.BlockSpec((seq, LANES), lambda b, h: (b, 0)),
                pl.BlockSpec((ctx_len, LANES), lambda b, h: (ctx0 + b, 0)),
                pl.BlockSpec((seq // blk, N_GATES, blk), lambda b, h: (b, 0, 0)),
                pl.BlockSpec((ctx_len // blk, N_GATES, blk), lambda b, h: (ctx0 + b, 0, 0))]
    out_specs = [pl.BlockSpec((seq, ML_DV), lambda b, h: (b, h))]
    out_shape = [jax.ShapeDtypeStruct((n_lat, ML_V_W), BF16)]
    if need_ctx:
        out_specs.append(pl.BlockSpec((ctx_len, ML_DV), lambda b, h: (b, h)))
        out_shape.append(jax.ShapeDtypeStruct((batch * ctx_len, ML_V_W), BF16))
    scratch = [pltpu.VMEM((n_blk, 2 * ML_DQK, ML_DV), BF16),
               pltpu.VMEM((n_blk, ML_DQK, 2 * LANES), BF16),
               pltpu.VMEM((n_blk, 8, 2 * LANES), F32)]
    out = pl.pallas_call(
        functools.partial(_mlstm_scan_kernel, need_ctx=need_ctx),
        grid=(batch, ML_HEADS),
        in_specs=in_specs,
        out_specs=out_specs,
        out_shape=out_shape,
        scratch_shapes=scratch,
        compiler_params=_params(2),
        name="mlstm",
    )(mq, mq, mk, mk, mk_t, mk_t, mv, mv, gates3, gates3, gates_rows, gates_rows)
    return out if need_ctx else (out[0], None)


def _diffattn_kernel(lam_ref, g_ref, q_ref, *refs, n_seg, lam_init):
    k_refs = refs[:n_seg]
    v_refs = refs[n_seg:2 * n_seg]
    o_ref = refs[2 * n_seg]
    lv = lam_ref[...]
    lam = (jnp.exp(jnp.sum(lv[0:1] * lv[1:2], axis=1, keepdims=True))
           - jnp.exp(jnp.sum(lv[2:3] * lv[3:4], axis=1, keepdims=True)) + lam_init)
    q = q_ref[...]
    lane = lax.broadcasted_iota(jnp.int32, q.shape, 1)
    zero = jnp.zeros_like(q)
    halves = [jnp.where(lane < DA_HALF, q, zero), jnp.where(lane >= DA_HALF, q, zero)]
    scores = [[_dot_nt(qh, k_ref[...]) for k_ref in k_refs] for qh in halves]
    maxes = [functools.reduce(jnp.maximum, [jnp.max(s, axis=1, keepdims=True) for s in ss]) for ss in scores]
    probs = [[jnp.exp2((s - m).astype(BF16)) for s in ss] for ss, m in zip(scores, maxes)]
    heads = []
    for ps in probs:
        acc = None
        for p, v_ref in zip(ps, v_refs):
            part = _dot(p, v_ref[...])
            acc = part if acc is None else acc + part
        heads.append(acc[:, :DA_DV] * (1.0 / acc[:, DA_DV:]))
    o = heads[0] - lam * heads[1]
    ms = jnp.mean(o * o, axis=1, keepdims=True)
    o_ref[...] = (o * lax.rsqrt(ms + LN_EPS) * g_ref[...] * (1.0 - lam_init)).astype(o_ref.dtype)


def _diffattn(dq, dk, dv, lam_vecs, da_g, *, q_row0, q_len, segments, batch, lam_init, tq=1024):
    tq = min(tq, q_len)
    nq = q_len // tq
    q0 = q_row0 // tq

    def seg_spec(row0, length, width):
        return pl.BlockSpec((length, width), lambda b, h, i: (row0 // length + b, h))

    return pl.pallas_call(
        functools.partial(_diffattn_kernel, n_seg=len(segments), lam_init=lam_init),
        grid=(batch, DA_HEADS, nq),
        in_specs=[pl.BlockSpec(lam_vecs.shape, lambda b, h, i: (0, 0)),
                  pl.BlockSpec((1, DA_DV), lambda b, h, i: (0, 0)),
                  pl.BlockSpec((tq, LANES), lambda b, h, i: (q0 + b * nq + i, h))]
                 + [seg_spec(r0, ln, LANES) for r0, ln in segments]
                 + [seg_spec(r0, ln, 2 * DA_DV) for r0, ln in segments],
        out_specs=pl.BlockSpec((tq, DA_DV), lambda b, h, i: (b * nq + i, h)),
        out_shape=jax.ShapeDtypeStruct((batch * q_len, DA_V_W), BF16),
        compiler_params=_params(3),
        name="diffattn",
    )(lam_vecs, da_g.reshape(1, DA_DV), dq, *([dk] * len(segments)), *([dv] * len(segments)))


def _mixer_out_kernel(h_ref, mod_ref, sb_ref, p_ref, pprev_ref, pnext_ref, smo_ref, *refs,
                      seq, ctx_len, n_lat, alpha, two_source):
    tm = h_ref.shape[0]
    r0 = pl.program_id(0) * tm
    is_lat = r0 < n_lat
    if two_source:
        hml_l, hml_c, yda_l, yda_c = refs[:4]
        refs = refs[4:]
        hml = jnp.where(is_lat, hml_l[...], hml_c[...])
        yda = jnp.where(is_lat, yda_l[...], yda_c[...])
    else:
        hml, yda = refs[0][...], refs[1][...]
        refs = refs[2:]
    gs_ref, gm_ref, gd_ref, convw_ref, wsc_ref, wml_ref, wda_ref, wo_ref, lng_ref, lnb_ref, o_ref = refs
    pos = jnp.where(is_lat, r0 % seq, (r0 - n_lat) % ctx_len)
    seq_len = jnp.where(is_lat, seq, ctx_len)
    keep_prev = (pos != 0).astype(F32)
    keep_next = (pos + tm != seq_len).astype(F32)

    p = p_ref[...].astype(F32)
    prev_row = pprev_ref[...].astype(F32)[BF16_SUBLANES - 1:BF16_SUBLANES, :] * keep_prev
    next_row = pnext_ref[...].astype(F32)[0:1, :] * keep_next
    row = lax.broadcasted_iota(jnp.int32, p.shape, 0)
    p_before = jnp.where(row == 0, prev_row, pltpu.roll(p, 1, 0))
    p_after = jnp.where(row == tm - 1, next_row, pltpu.roll(p, tm - 1, 0))
    cw = convw_ref[...]
    conv = cw[0:1, :] * p_before + cw[1:2, :] * p + cw[2:3, :] * p_after
    y_sc = (sb_ref[...].astype(F32) * conv).astype(BF16)
    y_ml = smo_ref[...] * hml
    y = (gs_ref[...].astype(F32) * _dot(y_sc, wsc_ref[...].astype(BF16))
         + gm_ref[...].astype(F32) * _dot(y_ml, wml_ref[...].astype(BF16))
         + gd_ref[...].astype(F32) * _dot(yda, wda_ref[...].astype(BF16)))
    y = _dot(y.astype(BF16), wo_ref[...].astype(BF16))
    h = h_ref[...]
    o_ref[...] = _layer_norm(alpha * h + mod_ref[0, 5:6, :] * y, lng_ref[...], lnb_ref[...])


def _mixer_out(h, mods3, sb, p, smo, hml, yda, gs, gm, gd, conv_w, w_sc, w_ml, w_da, w_o, ln_g, ln_b,
               *, layer, n_rows, seq, ctx_len, batch, n_lat, alpha, hml_ctx=None, yda_ctx=None, tm=256):
    d = h.shape[1]
    tm = min(tm, ctx_len)
    halo = BF16_SUBLANES
    last_halo = p.shape[0] // halo - 1
    two_source = hml_ctx is not None

    def row_spec(width):
        return pl.BlockSpec((tm, width), lambda i: (i, 0))

    if two_source:
        branch = [hml, hml_ctx, yda, yda_ctx]
        branch_specs = (_two_source_specs(tm, ML_V_W, n_lat // tm) + _two_source_specs(tm, DA_V_W, n_lat // tm))
    else:
        branch = [hml, yda]
        branch_specs = [row_spec(ML_V_W), row_spec(DA_V_W)]
    kernel = functools.partial(_mixer_out_kernel, seq=seq, ctx_len=ctx_len, n_lat=n_lat, alpha=alpha,
                               two_source=two_source)
    return pl.pallas_call(
        kernel,
        grid=(n_rows // tm,),
        in_specs=[row_spec(d),
                  pl.BlockSpec((1, N_MOD, d), lambda i: (jnp.minimum(i * tm // seq, batch), 0, 0)),
                  row_spec(SC_WIDTH), row_spec(SC_WIDTH),
                  pl.BlockSpec((halo, SC_WIDTH), lambda i: (jnp.maximum(i * (tm // halo) - 1, 0), 0)),
                  pl.BlockSpec((halo, SC_WIDTH),
                               lambda i: (jnp.minimum((i + 1) * (tm // halo), last_halo), 0)),
                  row_spec(ML_V_W)] + branch_specs + [
                  row_spec(d), row_spec(d), row_spec(d),
                  _resident((SC_KSIZE, SC_WIDTH)),
                  _resident_layer(w_sc, layer), _resident_layer(w_ml, layer), _resident_layer(w_da, layer),
                  _resident_layer(w_o, layer),
                  pl.BlockSpec((1, d), lambda i: (0, 0)),
                  pl.BlockSpec((1, d), lambda i: (0, 0))],
        out_specs=row_spec(d),
        out_shape=jax.ShapeDtypeStruct((n_rows, d), F32),
        compiler_params=_params(1),
        name="mixer_out",
    )(h, mods3, sb, p, p, p, smo, *branch, gs, gm, gd, conv_w, w_sc, w_ml, w_da, w_o,
      ln_g.reshape(1, d), ln_b.reshape(1, d))


def _rope_tables(seq, tile):
    n_freq = DA_HALF // 4
    t = jnp.arange(seq)
    row_ids = (t // GRID_W).astype(F32)
    col_ids = (t % GRID_W).astype(F32)
    inv = ROPE_BASE ** (-jnp.arange(n_freq, dtype=F32) / n_freq)
    ang = jnp.concatenate([row_ids[:, None] * inv, col_ids[:, None] * inv], axis=-1)
    cos, sin = jnp.cos(ang), jnp.sin(ang)
    reps = LANES // DA_HALF
    cos_t = jnp.tile(jnp.concatenate([cos, cos], axis=-1), (1, reps))
    sin_t = jnp.tile(jnp.concatenate([-sin, sin], axis=-1), (1, reps))
    cos_t = jnp.concatenate([cos_t, jnp.ones((tile, LANES), F32)], axis=0)
    sin_t = jnp.concatenate([sin_t, jnp.zeros((tile, LANES), F32)], axis=0)
    return cos_t, sin_t


def kernel(x, c, ctx, c_ctx, w_ada, b_ada, ln_g, ln_b, ffn1_up, ffn1_down, ffn2_up, ffn2_down, w_in, b_in,
           conv_w, w_sc, w_ml, w_da, w_o, lam_q1, lam_k1, lam_q2, lam_k2, da_norm_g):
    batch, seq, d = x.shape
    ctx_len = ctx.shape[1]
    depth = w_ada.shape[0]
    alpha = (2 * depth) ** 0.25
    n_lat = batch * seq
    n_ctx = batch * ctx_len
    mixer_in_tile = min(256, seq)

    n_cond = -(-(batch + 1) // 8) * 8
    cc = jnp.concatenate([c, c_ctx[None, :], jnp.zeros((n_cond - batch - 1, d), F32)], axis=0)
    mods = _ada(cc, w_ada, b_ada).reshape(depth, n_cond, N_MOD, d)

    cos_t, sin_t = _rope_tables(seq, mixer_in_tile)
    gate_lo = 3 * SC_WIDTH + 2 * ML_QK_W + 2 * ML_V_W
    gate_hi = gate_lo + N_GATES

    h, h_ctx = x.reshape(n_lat, d), ctx.reshape(n_ctx, d)
    for l in range(depth):
        last = l == depth - 1
        lam_init = 0.8 - 0.6 * math.exp(-0.3 * l)
        mods3 = mods[l]
        w_head = w_in[l, :, :gate_lo].astype(BF16)
        w_tail = w_in[l, :, gate_hi:].astype(BF16)
        b_head = b_in[l, :gate_lo][None, :]
        b_tail = b_in[l, gate_hi:][None, :]
        w_gate = jnp.pad(w_in[l, :, gate_lo:gate_hi], ((0, 0), (0, LANES - N_GATES))).astype(BF16)
        b_gate = jnp.pad(b_in[l, gate_lo:gate_hi], (0, LANES - N_GATES))[None, :]
        lam_vecs = jnp.stack([lam_q1[l], lam_k1[l], lam_q2[l], lam_k2[l]]).astype(F32)

        h = _ffn(h, mods3, ffn1_up, ffn1_down, ln_g[l, 0], ln_b[l, 0], layer=l,
                 mod_base=0, n_rows=n_lat + n_ctx, seq=seq, batch=batch, alpha=alpha, h_ctx=h_ctx)
        h_ctx = None

        (sb, p, mq, mk, mv, smo, mg, dq, dk, dv, gs, gm, gd) = _mixer_in(
            h, mods3, w_head, b_head, w_tail, b_tail, w_gate, b_gate, cos_t, sin_t,
            seq=seq, batch=batch, n_lat=n_lat, tm=mixer_in_tile)

        hml, hml_ctx = _mlstm_scan(mq, mk, mv, mg, seq=seq, ctx_len=ctx_len, batch=batch, n_lat=n_lat,
                                   need_ctx=not last)
        yda = _diffattn(dq, dk, dv, lam_vecs, da_norm_g[l], q_row0=0, q_len=seq,
                        segments=((0, seq), (n_lat, ctx_len)), batch=batch, lam_init=lam_init)
        yda_ctx = None
        if not last:
            yda_ctx = _diffattn(dq, dk, dv, lam_vecs, da_norm_g[l], q_row0=n_lat, q_len=ctx_len,
                                segments=((n_lat, ctx_len),), batch=batch, lam_init=lam_init)

        n_rows = n_lat if last else n_lat + n_ctx
        h = _mixer_out(h, mods3, sb, p, smo, hml, yda, gs, gm, gd, conv_w[l], w_sc, w_ml, w_da, w_o,
                       ln_g[l, 1], ln_b[l, 1], hml_ctx=hml_ctx, yda_ctx=yda_ctx, layer=l,
                       n_rows=n_rows, seq=seq, ctx_len=ctx_len, batch=batch, n_lat=n_lat, alpha=alpha)
        h = _ffn(h, mods3, ffn2_up, ffn2_down, ln_g[l, 2], ln_b[l, 2], layer=l,
                 mod_base=6, n_rows=n_rows, seq=seq, batch=batch, alpha=alpha)
    return h[:n_lat].reshape(batch, seq, d)
```

```python
import functools
import math

import jax
import jax.numpy as jnp
from jax import lax
from jax.experimental import pallas as pl
from jax.experimental.pallas import tpu as pltpu

GRID_W = 64
N_MOD = 9
SC_WIDTH = 512
SC_KSIZE = 3
ML_HEADS = 4
ML_DQK = 128
ML_DV = 256
DA_HEADS = 4
DA_HALF = 64
DA_DV = 2 * DA_HALF
ROPE_BASE = 10000.0
LN_EPS = 1e-5

ML_QK_W = ML_HEADS * ML_DQK
ML_V_W = ML_HEADS * ML_DV
DA_QK_W = DA_HEADS * 2 * DA_HALF
DA_V_W = DA_HEADS * DA_DV
N_GATES = 4 * ML_HEADS

ML_BLOCK = 256
FFN_COLS = 256
LANES = 128
BF16_SUBLANES = 16
V7X_VMEM_BYTES = 64 * 1024 * 1024
VMEM_LIMIT = V7X_VMEM_BYTES - 8 * 1024 * 1024

F32 = jnp.float32
BF16 = jnp.bfloat16
NEG_BIG = -1e30
LOG2_E = 1.4426950408889634

assert ML_DQK == LANES and DA_DV == LANES and ML_BLOCK == 2 * LANES


def _dot(a, b, precision=None):
    return jnp.dot(a, b, preferred_element_type=F32, precision=precision)


def _dot_nt(a, b):
    return lax.dot_general(a, b, (((1,), (1,)), ((), ())), preferred_element_type=F32)


def _wide(x, width):
    return jnp.concatenate([x] * (width // LANES), axis=1)


def _layer_norm(y, g, b):
    mu = jnp.mean(y, axis=-1, keepdims=True)
    yc = y - mu
    var = jnp.mean(yc * yc, axis=-1, keepdims=True)
    return yc * lax.rsqrt(var + LN_EPS) * g + b


def _resident(shape):
    return pl.BlockSpec(shape, lambda *_: (0,) * len(shape), pipeline_mode=pl.Buffered(1))


def _resident_layer(stacked, layer):
    n = stacked.ndim - 1
    return pl.BlockSpec((None,) + stacked.shape[1:], lambda *_: (layer,) + (0,) * n,
                        pipeline_mode=pl.Buffered(1))


def _params(n_axes):
    return pltpu.CompilerParams(dimension_semantics=("parallel",) * n_axes,
                                vmem_limit_bytes=VMEM_LIMIT)


def _ada_kernel(c_ref, w_ref, b_ref, o_ref):
    c = c_ref[...]
    a = (c * jax.nn.sigmoid(c)).astype(BF16)
    o_ref[0] = _dot(a, w_ref[0].astype(BF16)) + b_ref[0]


def _ada(cc, w_ada, b_ada):
    depth, d, n = w_ada.shape
    tn = d
    return pl.pallas_call(
        _ada_kernel,
        grid=(depth, n // tn),
        in_specs=[pl.BlockSpec(cc.shape, lambda l, j: (0, 0)),
                  pl.BlockSpec((1, d, tn), lambda l, j: (l, 0, j)),
                  pl.BlockSpec((1, 1, tn), lambda l, j: (l, 0, j))],
        out_specs=pl.BlockSpec((1, cc.shape[0], tn), lambda l, j: (l, 0, j)),
        out_shape=jax.ShapeDtypeStruct((depth, cc.shape[0], n), F32),
        compiler_params=_params(2),
        name="ada",
    )(cc, w_ada, b_ada.reshape(depth, 1, n))


def _two_source_specs(tm, width, lat_tiles):
    return [pl.BlockSpec((tm, width), lambda i: (jnp.minimum(i, lat_tiles - 1), 0)),
            pl.BlockSpec((tm, width), lambda i: (jnp.maximum(i - lat_tiles, 0), 0))]


def _ffn_kernel(*refs, mod_base, alpha, lat_tiles):
    if lat_tiles is None:
        h_ref, mod_ref, wup_ref, wdn_ref, lng_ref, lnb_ref, o_ref, g_scr = refs
        h = h_ref[...]
    else:
        h_ref, hc_ref, mod_ref, wup_ref, wdn_ref, lng_ref, lnb_ref, o_ref, g_scr = refs
        h = jnp.where(pl.program_id(0) < lat_tiles, h_ref[...], hc_ref[...])
    f = wdn_ref.shape[0]
    shift = mod_ref[0, mod_base:mod_base + 1, :]
    scale = mod_ref[0, mod_base + 1:mod_base + 2, :]
    gate = mod_ref[0, mod_base + 2:mod_base + 3, :]
    u = (h * (1.0 + scale) + shift).astype(BF16)
    for c in range(f // FFN_COLS):
        lo, hi = c * FFN_COLS, (c + 1) * FFN_COLS
        a = _dot(u, wup_ref[:, lo:hi].astype(BF16))
        v = _dot(u, wup_ref[:, f + lo:f + hi].astype(BF16))
        g_scr[:, lo:hi] = (a * jax.nn.sigmoid(a) * v).astype(BF16)
    d = _dot(g_scr[...], wdn_ref[...].astype(BF16))
    o_ref[...] = _layer_norm(alpha * h + (0.5 * gate) * d, lng_ref[...], lnb_ref[...])


def _ffn(h, mods3, w_up, w_dn, ln_g, ln_b, *, layer, mod_base, n_rows, seq, batch, alpha, h_ctx=None, tm=512):
    d = h.shape[1]
    f = w_dn.shape[1]
    tm = min(tm, seq)
    lat_tiles = None if h_ctx is None else h.shape[0] // tm
    kernel = functools.partial(_ffn_kernel, mod_base=mod_base, alpha=alpha, lat_tiles=lat_tiles)
    if h_ctx is None:
        sources, source_specs = [h], [pl.BlockSpec((tm, d), lambda i: (i, 0))]
    else:
        sources, source_specs = [h, h_ctx], _two_source_specs(tm, d, lat_tiles)
    return pl.pallas_call(
        kernel,
        grid=(n_rows // tm,),
        in_specs=source_specs + [
                  pl.BlockSpec((1, N_MOD, d), lambda i: (jnp.minimum(i * tm // seq, batch), 0, 0)),
                  _resident_layer(w_up, layer),
                  _resident_layer(w_dn, layer),
                  pl.BlockSpec((1, d), lambda i: (0, 0)),
                  pl.BlockSpec((1, d), lambda i: (0, 0))],
        out_specs=pl.BlockSpec((tm, d), lambda i: (i, 0)),
        out_shape=jax.ShapeDtypeStruct((n_rows, d), F32),
        scratch_shapes=[pltpu.VMEM((tm, f), BF16)],
        compiler_params=_params(1),
        name="ffn",
    )(*sources, mods3, w_up, w_dn, ln_g.reshape(1, d), ln_b.reshape(1, d))


GATE_COL0 = 3 * SC_WIDTH + 2 * ML_QK_W + 2 * ML_V_W


def _proj_cols(d):
    parts = ((("sb", SC_WIDTH), ("sc", SC_WIDTH), ("sx", SC_WIDTH),
              ("mq", ML_QK_W), ("mk", ML_QK_W), ("mv", ML_V_W), ("mo", ML_V_W)),
             (("dq", DA_QK_W), ("dk", DA_QK_W), ("dv", DA_V_W), ("gs", d), ("gm", d), ("gd", d)))
    cols, widths = {}, []
    for part, sizes in enumerate(parts):
        off = 0
        for name, w in sizes:
            cols[name] = (part, off, off + w)
            off += w
        widths.append(off)
    return cols, widths


def _proj_weight_kernel(w_ref, head_o, tail_o, gate_o):
    n_head = head_o.shape[1]
    n_tail = tail_o.shape[1]
    head_o[...] = w_ref[:, :n_head].astype(BF16)
    tail_o[...] = w_ref[:, n_head + N_GATES:n_head + N_GATES + n_tail].astype(BF16)
    g = w_ref[:, n_head:n_head + LANES]
    lane = lax.broadcasted_iota(jnp.int32, g.shape, 1)
    gate_o[...] = jnp.where(lane < N_GATES, g, 0.0).astype(BF16)


def _proj_weights(w_in, d):
    depth, _, n_in = w_in.shape
    _, (n_head, n_tail) = _proj_cols(d)
    tr = math.gcd(d, LANES)

    def spec(width):
        return pl.BlockSpec((None, tr, width), lambda l, i: (l, i, 0))

    return pl.pallas_call(
        _proj_weight_kernel,
        grid=(depth, d // tr),
        in_specs=[spec(n_in)],
        out_specs=[spec(n_head), spec(n_tail), spec(LANES)],
        out_shape=[jax.ShapeDtypeStruct((depth, d, w), BF16) for w in (n_head, n_tail, LANES)],
        compiler_params=_params(2),
        name="proj_weights",
    )(w_in)


def _mixer_in_kernel(h_ref, mod_ref, wh_ref, bh_ref, wt_ref, bt_ref, wg_ref, bg_ref, cos_ref, sin_ref,
                     sb_o, p_o, mq_o, mk_o, mv_o, smo_o, mg_o, dq_o, dk_o, dv_o, gs_o, gm_o, gd_o):
    d = h_ref.shape[1]
    tm = h_ref.shape[0]
    cols, _ = _proj_cols(d)
    h = h_ref[...]
    shift = mod_ref[0, 3:4, :]
    scale = mod_ref[0, 4:5, :]
    u = (h * (1.0 + scale) + shift).astype(BF16)

    def proj(name):
        part, lo, hi = cols[name]
        w_ref, b_ref = ((wh_ref, bh_ref), (wt_ref, bt_ref))[part]
        return _dot(u, w_ref[:, lo:hi]) + b_ref[:, lo:hi]

    sb_o[...] = proj("sb").astype(BF16)
    p_o[...] = (proj("sc") * proj("sx")).astype(BF16)
    mq_o[...] = proj("mq").astype(BF16)
    mk_o[...] = (proj("mk") * (ML_DQK ** -0.5)).astype(BF16)
    mv_o[...] = proj("mv").astype(BF16)
    smo_o[...] = jax.nn.sigmoid(proj("mo")).astype(BF16)
    mg_o[...] = (_dot(u, wg_ref[...]) + bg_ref[...])[:, :N_GATES]

    cos_t = cos_ref[...]
    sin_t = sin_ref[...]
    lane = lax.broadcasted_iota(jnp.int32, cos_t.shape, 1)
    first_half = (lane % DA_HALF) < (DA_HALF // 2)

    def rope_store(z, out_ref, mult):
        for k in range(z.shape[1] // LANES):
            x = z[:, k * LANES:(k + 1) * LANES]
            partner = jnp.where(first_half,
                                pltpu.roll(x, LANES - DA_HALF // 2, 1),
                                pltpu.roll(x, DA_HALF // 2, 1))
            out_ref[:, k * LANES:(k + 1) * LANES] = ((x * cos_t + partner * sin_t) * mult).astype(BF16)

    rope_store(proj("dq"), dq_o, (DA_HALF ** -0.5) * LOG2_E)
    rope_store(proj("dk"), dk_o, 1.0)
    dv = proj("dv").astype(BF16)
    ones = jnp.ones((tm, DA_DV), BF16)
    for k in range(DA_HEADS):
        dv_o[:, 2 * k * DA_DV:(2 * k + 1) * DA_DV] = dv[:, k * DA_DV:(k + 1) * DA_DV]
        dv_o[:, (2 * k + 1) * DA_DV:(2 * k + 2) * DA_DV] = ones
    gs_o[...] = jax.nn.sigmoid(proj("gs")).astype(BF16)
    gm_o[...] = jax.nn.sigmoid(proj("gm")).astype(BF16)
    gd_o[...] = jax.nn.sigmoid(proj("gd")).astype(BF16)


def _mixer_in(h, mods3, w_head, b_head, w_tail, b_tail, w_gate, b_gate, cos_t, sin_t,
              *, layer, seq, batch, n_lat, tm=256):
    rows, d = h.shape
    _, (n_head, n_tail) = _proj_cols(d)
    tm = min(tm, seq)
    lat_tiles = n_lat // tm
    rope_blocks = seq // tm

    def row_spec(width):
        return pl.BlockSpec((tm, width), lambda i: (i, 0))

    def rope_map(i):
        return (jnp.where(i < lat_tiles, i % rope_blocks, rope_blocks), 0)

    widths = (SC_WIDTH, SC_WIDTH, ML_QK_W, ML_QK_W, ML_V_W, ML_V_W, N_GATES,
              DA_QK_W, DA_QK_W, 2 * DA_V_W, d, d, d)
    dtypes = (BF16,) * 6 + (F32,) + (BF16,) * 6
    return pl.pallas_call(
        _mixer_in_kernel,
        grid=(rows // tm,),
        in_specs=[row_spec(d),
                  pl.BlockSpec((1, N_MOD, d), lambda i: (jnp.minimum(i * tm // seq, batch), 0, 0)),
                  _resident_layer(w_head, layer),
                  _resident((1, n_head)),
                  _resident_layer(w_tail, layer),
                  _resident((1, n_tail)),
                  _resident_layer(w_gate, layer),
                  _resident((1, LANES)),
                  pl.BlockSpec((tm, LANES), rope_map),
                  pl.BlockSpec((tm, LANES), rope_map)],
        out_specs=[row_spec(w) for w in widths],
        out_shape=[jax.ShapeDtypeStruct((rows, w), dt) for w, dt in zip(widths, dtypes)],
        compiler_params=_params(1),
        name="mixer_in",
    )(h, mods3, w_head, b_head, w_tail, b_tail, w_gate, b_gate, cos_t, sin_t)


def _log_sigmoid(x):
    return jnp.minimum(x, 0.0) - jnp.log(1.0 + jnp.exp(-jnp.abs(x)))


def _split3(x):
    pieces, rest = [], x
    for _ in range(3):
        piece = rest.astype(BF16)
        pieces.append(piece)
        rest = rest - piece.astype(F32)
    return pieces


def _mlstm_gate_kernel(g_ref, o_ref, o3_ref):
    blk = ML_BLOCK
    t_idx = lax.broadcasted_iota(jnp.int32, (blk, blk), 0)
    s_idx = lax.broadcasted_iota(jnp.int32, (blk, blk), 1)
    tril = (s_idx <= t_idx).astype(BF16)
    lane = lax.broadcasted_iota(jnp.int32, (blk, N_GATES), 1)
    src = lax.broadcasted_iota(jnp.int32, (N_GATES, LANES), 0)
    dst = lax.broadcasted_iota(jnp.int32, (N_GATES, LANES), 1)
    places = [(dst == src + term * N_GATES).astype(BF16) for term in range(3)]
    for j in range(g_ref.shape[0] // blk):
        sl = pl.ds(j * blk, blk)
        g = g_ref[sl, :]
        ls = _log_sigmoid(g)
        prefix = functools.reduce(jnp.add, [_dot(tril, piece) for piece in _split3(ls)])
        suffix = prefix[blk - 1:blk, :] - prefix + ls
        out = jnp.where(lane < 2 * ML_HEADS, g, jnp.where(lane < 3 * ML_HEADS, prefix, suffix))
        o_ref[sl, :] = out
        placed = functools.reduce(jnp.add, [_dot(piece, place) for piece, place in zip(_split3(out), places)])
        o3_ref[sl, :] = placed.astype(BF16)


def _mlstm_gates(mg):
    rows = mg.shape[0]
    step = math.gcd(rows, 8 * ML_BLOCK)
    return pl.pallas_call(
        _mlstm_gate_kernel,
        grid=(rows // step,),
        in_specs=[pl.BlockSpec((step, N_GATES), lambda i: (i, 0))],
        out_specs=[pl.BlockSpec((step, N_GATES), lambda i: (i, 0)),
                   pl.BlockSpec((step, LANES), lambda i: (i, 0))],
        out_shape=[jax.ShapeDtypeStruct((rows, N_GATES), F32),
                   jax.ShapeDtypeStruct((rows, LANES), BF16)],
        compiler_params=_params(1),
        name="mlstm_gates",
    )(mg)


N_ROWS_PAD = BF16_SUBLANES
ST_ROWS = 2 * N_ROWS_PAD + 2 * ML_DV


def _mlstm_scan_kernel(ql_ref, qc_ref, kl_ref, kc_ref, ktl_ref, ktc_ref, vtl_ref, vtc_ref,
                       g3l_ref, g3c_ref, grl_ref, grc_ref, *rest, need_ctx):
    if need_ctx:
        hl_ref, hc_ref, st_in, m_in = rest
    else:
        hl_ref, st_in, m_in = rest
        hc_ref = None
    blk = ML_BLOCK
    n_lat_blk = ql_ref.shape[0] // blk
    n_ctx_blk = qc_ref.shape[0] // blk
    head = pl.program_id(1)

    s_idx = lax.broadcasted_iota(jnp.int32, (blk, blk), 0)
    t_idx = lax.broadcasted_iota(jnp.int32, (blk, blk), 1)
    lane_t = lax.broadcasted_iota(jnp.int32, (1, blk), 1)
    ones_n = jnp.ones((N_ROWS_PAD, blk), BF16)
    sel_src = lax.broadcasted_iota(jnp.int32, (LANES, 2 * LANES), 0)
    sel_dst = lax.broadcasted_iota(jnp.int32, (LANES, 2 * LANES), 1)
    sel_bwd = sel_dst >= LANES
    col = sel_src % N_GATES
    in_terms = sel_src < 3 * N_GATES
    plus = jnp.logical_and(col == head + jnp.where(sel_bwd, ML_HEADS, 0), in_terms)
    minus = jnp.logical_and(col == head + jnp.where(sel_bwd, 3 * ML_HEADS, 2 * ML_HEADS), in_terms)
    sel = (plus.astype(F32) - minus.astype(F32)).astype(BF16)

    def gate_rows(gr_ref, j, bwd):
        i_idx = head + (ML_HEADS if bwd else 0)
        f_idx = head + (3 * ML_HEADS if bwd else 2 * ML_HEADS)
        return gr_ref[j, pl.ds(f_idx, 1), :], gr_ref[j, pl.ds(i_idx, 1), :]

    def scan_states():
        ctx_blocks = [(grc_ref, ktc_ref, vtc_ref, j, j) for j in range(n_ctx_blk)]
        lat_blocks = [(grl_ref, ktl_ref, vtl_ref, j, n_ctx_blk + j) for j in range(n_lat_blk)]
        orders = (ctx_blocks + lat_blocks, ctx_blocks[::-1] + lat_blocks[::-1])
        m_prev = [jnp.zeros((1, 1), F32)] * 2
        ct_run = [jnp.zeros((ML_DV, ML_DQK), F32)] * 2
        n_run = [jnp.zeros((N_ROWS_PAD, ML_DQK), F32)] * 2
        for step in range(len(orders[0])):
            for d, bwd in enumerate((False, True)):
                gr_ref, kt_ref, vt_ref, j, slot = orders[d][step]
                c_row0 = 2 * N_ROWS_PAD + d * ML_DV
                f_row, i_row = gate_rows(gr_ref, j, bwd)
                b_end = jnp.sum(jnp.where(lane_t == (0 if bwd else blk - 1), f_row, 0.0), axis=1,
                                keepdims=True)
                g_log = b_end + (i_row - f_row)
                m_new = jnp.maximum(b_end + m_prev[d], jnp.max(g_log, axis=1, keepdims=True))
                a_prev = jnp.exp(b_end + m_prev[d] - m_new)
                kw = (kt_ref[j].astype(F32) * jnp.exp(g_log - m_new)).astype(BF16)
                st_in[slot, d * N_ROWS_PAD:(d + 1) * N_ROWS_PAD, :] = n_run[d].astype(BF16)
                st_in[slot, c_row0:c_row0 + ML_DV, :] = ct_run[d].astype(BF16)
                m_in[slot, :, d * LANES:(d + 1) * LANES] = jnp.broadcast_to(m_prev[d], (8, LANES))
                ct_run[d] = a_prev * ct_run[d] + _dot_nt(vt_ref[j], kw)
                n_run[d] = a_prev * n_run[d] + _dot_nt(ones_n, kw)
                m_prev[d] = m_new

    def outputs(q, k, vt, g3, rows, slot):
        qk_t = _dot_nt(k, q)
        state = _dot_nt(st_in[slot], q)
        r_both = _dot(g3, sel)
        m_prev_both = m_in[slot][0:1, :]
        scaled, inter = [], []
        for d, bwd in enumerate((False, True)):
            f_row, _ = rows[d]
            r_rep = r_both[:, d * LANES:(d + 1) * LANES]
            mask = (s_idx >= t_idx) if bwd else (s_idx <= t_idx)
            d_log = jnp.where(mask, _wide(r_rep, blk) + f_row, NEG_BIG)
            m_loc = jnp.max(d_log, axis=0, keepdims=True)
            s = qk_t * jnp.exp(d_log - m_loc)
            den_loc = jnp.sum(s, axis=0, keepdims=True)
            m_inter = f_row + m_prev_both[:, d * LANES:d * LANES + 1]
            m_t = jnp.maximum(m_inter, m_loc)
            w_inter = jnp.exp(m_inter - m_t)
            w_loc = jnp.exp(m_loc - m_t)
            qn = state[d * N_ROWS_PAD:d * N_ROWS_PAD + 1, :]
            den = w_inter * qn + w_loc * den_loc
            inv = 1.0 / jnp.maximum(jnp.abs(den), jnp.exp(-m_t))
            scaled.append(s * (w_loc * inv))
            c_row0 = 2 * N_ROWS_PAD + d * ML_DV
            inter.append(state[c_row0:c_row0 + ML_DV, :] * (w_inter * inv))
        return _dot(vt, (scaled[0] + scaled[1]).astype(BF16)) + inter[0] + inter[1]

    def lat_rows(j):
        return pl.ds(pl.multiple_of(j * blk, blk), blk)

    scan_states()

    if need_ctx:
        for j in range(n_ctx_blk):
            sl = pl.ds(j * blk, blk)
            rows = (gate_rows(grc_ref, j, False), gate_rows(grc_ref, j, True))
            out_t = outputs(qc_ref[sl, :], kc_ref[sl, :], vtc_ref[j], g3c_ref[sl, :], rows, j)
            hc_ref[sl, :] = out_t.T.astype(hc_ref.dtype)

    def out_body(j, carry):
        sl = lat_rows(j)
        rows = (gate_rows(grl_ref, j, False), gate_rows(grl_ref, j, True))
        out_t = outputs(ql_ref[sl, :], kl_ref[sl, :], vtl_ref[j], g3l_ref[sl, :], rows, n_ctx_blk + j)
        hl_ref[sl, :] = out_t.T.astype(hl_ref.dtype)
        return carry

    lax.fori_loop(0, n_lat_blk, out_body, 0, unroll=4)


def _mlstm_scan(mq, mk, mv, mg, *, seq, ctx_len, batch, n_lat, need_ctx):
    blk = ML_BLOCK
    rows = mq.shape[0]
    ctx0 = n_lat // ctx_len
    n_blk = (seq + ctx_len) // blk
    gates, gates3 = _mlstm_gates(mg)
    gates_rows = gates.T.reshape(N_GATES, rows // blk, blk).transpose(1, 0, 2)
    mk_t = mk.reshape(rows // blk, blk, ML_QK_W).transpose(0, 2, 1)
    mv_t = mv.reshape(rows // blk, blk, ML_V_W).transpose(0, 2, 1)

    def lat(width):
        return pl.BlockSpec((seq, width), lambda b, h: (b, h))

    def ctx(width):
        return pl.BlockSpec((ctx_len, width), lambda b, h: (ctx0 + b, h))

    def lat_t(width):
        return pl.BlockSpec((seq // blk, width, blk), lambda b, h: (b, h, 0))

    def ctx_t(width):
        return pl.BlockSpec((ctx_len // blk, width, blk), lambda b, h: (ctx0 + b, h, 0))

    in_specs = [lat(ML_DQK), ctx(ML_DQK), lat(ML_DQK), ctx(ML_DQK),
                lat_t(ML_DQK), ctx_t(ML_DQK), lat_t(ML_DV), ctx_t(ML_DV),
                pl.BlockSpec((seq, LANES), lambda b, h: (b, 0)),
                pl.BlockSpec((ctx_len, LANES), lambda b, h: (ctx0 + b, 0)),
                pl.BlockSpec((seq // blk, N_GATES, blk), lambda b, h: (b, 0, 0)),
                pl.BlockSpec((ctx_len // blk, N_GATES, blk), lambda b, h: (ctx0 + b, 0, 0))]
    out_specs = [pl.BlockSpec((seq, ML_DV), lambda b, h: (b, h))]
    out_shape = [jax.ShapeDtypeStruct((n_lat, ML_V_W), BF16)]
    if need_ctx:
        out_specs.append(pl.BlockSpec((ctx_len, ML_DV), lambda b, h: (b, h)))
        out_shape.append(jax.ShapeDtypeStruct((batch * ctx_len, ML_V_W), BF16))
    scratch = [pltpu.VMEM((n_blk, ST_ROWS, ML_DQK), BF16),
               pltpu.VMEM((n_blk, 8, 2 * LANES), F32)]
    out = pl.pallas_call(
        functools.partial(_mlstm_scan_kernel, need_ctx=need_ctx),
        grid=(batch, ML_HEADS),
        in_specs=in_specs,
        out_specs=out_specs,
        out_shape=out_shape,
        scratch_shapes=scratch,
        compiler_params=_params(2),
        name="mlstm",
    )(mq, mq, mk, mk, mk_t, mk_t, mv_t, mv_t, gates3, gates3, gates_rows, gates_rows)
    return out if need_ctx else (out[0], None)


def _diffattn_kernel(lam_ref, g_ref, q_ref, *refs, n_seg, lam_init):
    k_refs = refs[:n_seg]
    v_refs = refs[n_seg:2 * n_seg]
    o_ref = refs[2 * n_seg]
    lv = lam_ref[...]
    lam = (jnp.exp(jnp.sum(lv[0:1] * lv[1:2], axis=1, keepdims=True))
           - jnp.exp(jnp.sum(lv[2:3] * lv[3:4], axis=1, keepdims=True)) + lam_init)
    q = q_ref[...]
    lane = lax.broadcasted_iota(jnp.int32, q.shape, 1)
    zero = jnp.zeros_like(q)
    halves = [jnp.where(lane < DA_HALF, q, zero), jnp.where(lane >= DA_HALF, q, zero)]
    scores = [[_dot_nt(qh, k_ref[...]) for k_ref in k_refs] for qh in halves]
    maxes = [functools.reduce(jnp.maximum, [jnp.max(s, axis=1, keepdims=True) for s in ss]) for ss in scores]
    probs = [[jnp.exp2((s - m).astype(BF16)) for s in ss] for ss, m in zip(scores, maxes)]
    heads = []
    for ps in probs:
        acc = None
        for p, v_ref in zip(ps, v_refs):
            part = _dot(p, v_ref[...])
            acc = part if acc is None else acc + part
        heads.append(acc[:, :DA_DV] * (1.0 / acc[:, DA_DV:]))
    o = heads[0] - lam * heads[1]
    ms = jnp.mean(o * o, axis=1, keepdims=True)
    o_ref[...] = (o * lax.rsqrt(ms + LN_EPS) * g_ref[...] * (1.0 - lam_init)).astype(o_ref.dtype)


def _diffattn(dq, dk, dv, lam_vecs, da_g, *, q_row0, q_len, segments, batch, lam_init, tq=1024):
    tq = min(tq, q_len)
    nq = q_len // tq
    q0 = q_row0 // tq

    def seg_spec(row0, length, width):
        return pl.BlockSpec((length, width), lambda b, h, i: (row0 // length + b, h))

    return pl.pallas_call(
        functools.partial(_diffattn_kernel, n_seg=len(segments), lam_init=lam_init),
        grid=(batch, DA_HEADS, nq),
        in_specs=[pl.BlockSpec(lam_vecs.shape, lambda b, h, i: (0, 0)),
                  pl.BlockSpec((1, DA_DV), lambda b, h, i: (0, 0)),
                  pl.BlockSpec((tq, LANES), lambda b, h, i: (q0 + b * nq + i, h))]
                 + [seg_spec(r0, ln, LANES) for r0, ln in segments]
                 + [seg_spec(r0, ln, 2 * DA_DV) for r0, ln in segments],
        out_specs=pl.BlockSpec((tq, DA_DV), lambda b, h, i: (b * nq + i, h)),
        out_shape=jax.ShapeDtypeStruct((batch * q_len, DA_V_W), BF16),
        compiler_params=_params(3),
        name="diffattn",
    )(lam_vecs, da_g.reshape(1, DA_DV), dq, *([dk] * len(segments)), *([dv] * len(segments)))


def _mixer_out_kernel(h_ref, mod_ref, sb_ref, p_ref, pprev_ref, pnext_ref, smo_ref, *refs,
                      seq, ctx_len, n_lat, alpha, two_source):
    tm = h_ref.shape[0]
    r0 = pl.program_id(0) * tm
    is_lat = r0 < n_lat
    if two_source:
        hml_l, hml_c, yda_l, yda_c = refs[:4]
        refs = refs[4:]
        hml = jnp.where(is_lat, hml_l[...], hml_c[...])
        yda = jnp.where(is_lat, yda_l[...], yda_c[...])
    else:
        hml, yda = refs[0][...], refs[1][...]
        refs = refs[2:]
    gs_ref, gm_ref, gd_ref, convw_ref, wsc_ref, wml_ref, wda_ref, wo_ref, lng_ref, lnb_ref, o_ref = refs
    row = lax.broadcasted_iota(jnp.int32, (tm, 1), 0)
    pos = jnp.where(is_lat, (r0 + row) % seq, (r0 - n_lat + row) % ctx_len)
    seq_len = jnp.where(is_lat, seq, ctx_len)
    first = pos == 0
    last = pos == seq_len - 1

    p = p_ref[...].astype(F32)
    prev_row = pprev_ref[...].astype(F32)[BF16_SUBLANES - 1:BF16_SUBLANES, :]
    next_row = pnext_ref[...].astype(F32)[0:1, :]
    p_before = jnp.where(first, 0.0, jnp.where(row == 0, prev_row, pltpu.roll(p, 1, 0)))
    p_after = jnp.where(last, 0.0, jnp.where(row == tm - 1, next_row, pltpu.roll(p, tm - 1, 0)))
    cw = convw_ref[...]
    conv = cw[0:1, :] * p_before + cw[1:2, :] * p + cw[2:3, :] * p_after
    y_sc = (sb_ref[...].astype(F32) * conv).astype(BF16)
    y_ml = smo_ref[...] * hml
    y = (gs_ref[...].astype(F32) * _dot(y_sc, wsc_ref[...].astype(BF16))
         + gm_ref[...].astype(F32) * _dot(y_ml, wml_ref[...].astype(BF16))
         + gd_ref[...].astype(F32) * _dot(yda, wda_ref[...].astype(BF16)))
    y = _dot(y.astype(BF16), wo_ref[...].astype(BF16))
    h = h_ref[...]
    o_ref[...] = _layer_norm(alpha * h + mod_ref[0, 5:6, :] * y, lng_ref[...], lnb_ref[...])


def _mixer_out(h, mods3, sb, p, smo, hml, yda, gs, gm, gd, conv_w, w_sc, w_ml, w_da, w_o, ln_g, ln_b,
               *, layer, n_rows, seq, ctx_len, batch, n_lat, alpha, hml_ctx=None, yda_ctx=None, tm=512):
    d = h.shape[1]
    tm = math.gcd(math.gcd(tm, seq), batch * ctx_len)
    halo = BF16_SUBLANES
    last_halo = p.shape[0] // halo - 1
    two_source = hml_ctx is not None

    def row_spec(width):
        return pl.BlockSpec((tm, width), lambda i: (i, 0))

    if two_source:
        branch = [hml, hml_ctx, yda, yda_ctx]
        branch_specs = (_two_source_specs(tm, ML_V_W, n_lat // tm) + _two_source_specs(tm, DA_V_W, n_lat // tm))
    else:
        branch = [hml, yda]
        branch_specs = [row_spec(ML_V_W), row_spec(DA_V_W)]
    kernel = functools.partial(_mixer_out_kernel, seq=seq, ctx_len=ctx_len, n_lat=n_lat, alpha=alpha,
                               two_source=two_source)
    return pl.pallas_call(
        kernel,
        grid=(n_rows // tm,),
        in_specs=[row_spec(d),
                  pl.BlockSpec((1, N_MOD, d), lambda i: (jnp.minimum(i * tm // seq, batch), 0, 0)),
                  row_spec(SC_WIDTH), row_spec(SC_WIDTH),
                  pl.BlockSpec((halo, SC_WIDTH), lambda i: (jnp.maximum(i * (tm // halo) - 1, 0), 0)),
                  pl.BlockSpec((halo, SC_WIDTH),
                               lambda i: (jnp.minimum((i + 1) * (tm // halo), last_halo), 0)),
                  row_spec(ML_V_W)] + branch_specs + [
                  row_spec(d), row_spec(d), row_spec(d),
                  _resident((SC_KSIZE, SC_WIDTH)),
                  _resident_layer(w_sc, layer), _resident_layer(w_ml, layer), _resident_layer(w_da, layer),
                  _resident_layer(w_o, layer),
                  pl.BlockSpec((1, d), lambda i: (0, 0)),
                  pl.BlockSpec((1, d), lambda i: (0, 0))],
        out_specs=row_spec(d),
        out_shape=jax.ShapeDtypeStruct((n_rows, d), F32),
        compiler_params=_params(1),
        name="mixer_out",
    )(h, mods3, sb, p, p, p, smo, *branch, gs, gm, gd, conv_w, w_sc, w_ml, w_da, w_o,
      ln_g.reshape(1, d), ln_b.reshape(1, d))


def _rope_tables(seq, tile):
    n_freq = DA_HALF // 4
    t = jnp.arange(seq)
    row_ids = (t // GRID_W).astype(F32)
    col_ids = (t % GRID_W).astype(F32)
    inv = ROPE_BASE ** (-jnp.arange(n_freq, dtype=F32) / n_freq)
    ang = jnp.concatenate([row_ids[:, None] * inv, col_ids[:, None] * inv], axis=-1)
    cos, sin = jnp.cos(ang), jnp.sin(ang)
    reps = LANES // DA_HALF
    cos_t = jnp.tile(jnp.concatenate([cos, cos], axis=-1), (1, reps))
    sin_t = jnp.tile(jnp.concatenate([-sin, sin], axis=-1), (1, reps))
    cos_t = jnp.concatenate([cos_t, jnp.ones((tile, LANES), F32)], axis=0)
    sin_t = jnp.concatenate([sin_t, jnp.zeros((tile, LANES), F32)], axis=0)
    return cos_t, sin_t


def kernel(x, c, ctx, c_ctx, w_ada, b_ada, ln_g, ln_b, ffn1_up, ffn1_down, ffn2_up, ffn2_down, w_in, b_in,
           conv_w, w_sc, w_ml, w_da, w_o, lam_q1, lam_k1, lam_q2, lam_k2, da_norm_g):
    batch, seq, d = x.shape
    ctx_len = ctx.shape[1]
    depth = w_ada.shape[0]
    alpha = (2 * depth) ** 0.25
    n_lat = batch * seq
    n_ctx = batch * ctx_len
    mixer_in_tile = min(512, seq)

    n_cond = -(-(batch + 1) // 8) * 8
    cc = jnp.concatenate([c, c_ctx[None, :], jnp.zeros((n_cond - batch - 1, d), F32)], axis=0)
    mods = _ada(cc, w_ada, b_ada).reshape(depth, n_cond, N_MOD, d)

    cos_t, sin_t = _rope_tables(seq, mixer_in_tile)
    gate_lo = GATE_COL0
    gate_hi = gate_lo + N_GATES
    w_head, w_tail, w_gate = _proj_weights(w_in, d)

    h, h_ctx = x.reshape(n_lat, d), ctx.reshape(n_ctx, d)
    for l in range(depth):
        last = l == depth - 1
        lam_init = 0.8 - 0.6 * math.exp(-0.3 * l)
        mods3 = mods[l]
        b_head = b_in[l, :gate_lo][None, :]
        b_tail = b_in[l, gate_hi:][None, :]
        b_gate = jnp.pad(b_in[l, gate_lo:gate_hi], (0, LANES - N_GATES))[None, :]
        lam_vecs = jnp.stack([lam_q1[l], lam_k1[l], lam_q2[l], lam_k2[l]]).astype(F32)

        h = _ffn(h, mods3, ffn1_up, ffn1_down, ln_g[l, 0], ln_b[l, 0], layer=l,
                 mod_base=0, n_rows=n_lat + n_ctx, seq=seq, batch=batch, alpha=alpha, h_ctx=h_ctx)
        h_ctx = None

        (sb, p, mq, mk, mv, smo, mg, dq, dk, dv, gs, gm, gd) = _mixer_in(
            h, mods3, w_head, b_head, w_tail, b_tail, w_gate, b_gate, cos_t, sin_t,
            layer=l, seq=seq, batch=batch, n_lat=n_lat, tm=mixer_in_tile)

        hml, hml_ctx = _mlstm_scan(mq, mk, mv, mg, seq=seq, ctx_len=ctx_len, batch=batch, n_lat=n_lat,
                                   need_ctx=not last)
        yda = _diffattn(dq, dk, dv, lam_vecs, da_norm_g[l], q_row0=0, q_len=seq,
                        segments=((0, seq), (n_lat, ctx_len)), batch=batch, lam_init=lam_init)
        yda_ctx = None
        if not last:
            yda_ctx = _diffattn(dq, dk, dv, lam_vecs, da_norm_g[l], q_row0=n_lat, q_len=ctx_len,
                                segments=((n_lat, ctx_len),), batch=batch, lam_init=lam_init)

        n_rows = n_lat if last else n_lat + n_ctx
        h = _mixer_out(h, mods3, sb, p, smo, hml, yda, gs, gm, gd, conv_w[l], w_sc, w_ml, w_da, w_o,
                       ln_g[l, 1], ln_b[l, 1], hml_ctx=hml_ctx, yda_ctx=yda_ctx, layer=l,
                       n_rows=n_rows, seq=seq, ctx_len=ctx_len, batch=batch, n_lat=n_lat, alpha=alpha)
        h = _ffn(h, mods3, ffn2_up, ffn2_down, ln_g[l, 2], ln_b[l, 2], layer=l,
                 mod_base=6, n_rows=n_rows, seq=seq, batch=batch, alpha=alpha)
    return h[:n_lat].reshape(batch, seq, d)
```

```python
import functools
import math

import jax
import jax.numpy as jnp
from jax import lax
from jax.experimental import pallas as pl
from jax.experimental.pallas import tpu as pltpu

GRID_W = 64
N_MOD = 9
SC_WIDTH = 512
SC_KSIZE = 3
ML_HEADS = 4
ML_DQK = 128
ML_DV = 256
DA_HEADS = 4
DA_HALF = 64
DA_DV = 2 * DA_HALF
ROPE_BASE = 10000.0
LN_EPS = 1e-5

ML_QK_W = ML_HEADS * ML_DQK
ML_V_W = ML_HEADS * ML_DV
DA_QK_W = DA_HEADS * 2 * DA_HALF
DA_V_W = DA_HEADS * DA_DV
N_GATES = 4 * ML_HEADS

ML_BLOCK = 256
FFN_COLS = 256
LANES = 128
BF16_SUBLANES = 16
V7X_VMEM_BYTES = 64 * 1024 * 1024
VMEM_LIMIT = V7X_VMEM_BYTES - 8 * 1024 * 1024

F32 = jnp.float32
BF16 = jnp.bfloat16
NEG_BIG = -1e30
LOG2_E = 1.4426950408889634

assert ML_DQK == LANES and DA_DV == LANES and ML_BLOCK == 2 * LANES


def _dot(a, b, precision=None):
    return jnp.dot(a, b, preferred_element_type=F32, precision=precision)


def _dot_nt(a, b):
    return lax.dot_general(a, b, (((1,), (1,)), ((), ())), preferred_element_type=F32)


def _wide(x, width):
    return jnp.concatenate([x] * (width // LANES), axis=1)


def _layer_norm(y, g, b):
    mu = jnp.mean(y, axis=-1, keepdims=True)
    yc = y - mu
    var = jnp.mean(yc * yc, axis=-1, keepdims=True)
    return yc * lax.rsqrt(var + LN_EPS) * g + b


def _resident(shape):
    return pl.BlockSpec(shape, lambda *_: (0,) * len(shape), pipeline_mode=pl.Buffered(1))


def _resident_layer(stacked, layer):
    n = stacked.ndim - 1
    return pl.BlockSpec((None,) + stacked.shape[1:], lambda *_: (layer,) + (0,) * n,
                        pipeline_mode=pl.Buffered(1))


def _params(n_axes):
    return pltpu.CompilerParams(dimension_semantics=("parallel",) * n_axes,
                                vmem_limit_bytes=VMEM_LIMIT)


def _ada_kernel(c_ref, w_ref, b_ref, o_ref):
    c = c_ref[...]
    a = (c * jax.nn.sigmoid(c)).astype(BF16)
    o_ref[0] = _dot(a, w_ref[0].astype(BF16)) + b_ref[0]


def _ada(cc, w_ada, b_ada):
    depth, d, n = w_ada.shape
    tn = d
    return pl.pallas_call(
        _ada_kernel,
        grid=(depth, n // tn),
        in_specs=[pl.BlockSpec(cc.shape, lambda l, j: (0, 0)),
                  pl.BlockSpec((1, d, tn), lambda l, j: (l, 0, j)),
                  pl.BlockSpec((1, 1, tn), lambda l, j: (l, 0, j))],
        out_specs=pl.BlockSpec((1, cc.shape[0], tn), lambda l, j: (l, 0, j)),
        out_shape=jax.ShapeDtypeStruct((depth, cc.shape[0], n), F32),
        compiler_params=_params(2),
        name="ada",
    )(cc, w_ada, b_ada.reshape(depth, 1, n))


def _two_source_specs(tm, width, lat_tiles):
    return [pl.BlockSpec((tm, width), lambda i: (jnp.minimum(i, lat_tiles - 1), 0)),
            pl.BlockSpec((tm, width), lambda i: (jnp.maximum(i - lat_tiles, 0), 0))]


def _ffn_kernel(*refs, mod_base, alpha, lat_tiles):
    if lat_tiles is None:
        h_ref, mod_ref, wup_ref, wdn_ref, lng_ref, lnb_ref, o_ref, g_scr = refs
        h = h_ref[...]
    else:
        h_ref, hc_ref, mod_ref, wup_ref, wdn_ref, lng_ref, lnb_ref, o_ref, g_scr = refs
        h = jnp.where(pl.program_id(0) < lat_tiles, h_ref[...], hc_ref[...])
    f = wdn_ref.shape[0]
    shift = mod_ref[0, mod_base:mod_base + 1, :]
    scale = mod_ref[0, mod_base + 1:mod_base + 2, :]
    gate = mod_ref[0, mod_base + 2:mod_base + 3, :]
    u = (h * (1.0 + scale) + shift).astype(BF16)
    for c in range(f // FFN_COLS):
        lo, hi = c * FFN_COLS, (c + 1) * FFN_COLS
        a = _dot(u, wup_ref[:, lo:hi].astype(BF16))
        v = _dot(u, wup_ref[:, f + lo:f + hi].astype(BF16))
        g_scr[:, lo:hi] = (a * jax.nn.sigmoid(a) * v).astype(BF16)
    d = _dot(g_scr[...], wdn_ref[...].astype(BF16))
    o_ref[...] = _layer_norm(alpha * h + (0.5 * gate) * d, lng_ref[...], lnb_ref[...])


def _ffn(h, mods3, w_up, w_dn, ln_g, ln_b, *, layer, mod_base, n_rows, seq, batch, alpha, h_ctx=None, tm=512):
    d = h.shape[1]
    f = w_dn.shape[1]
    tm = min(tm, seq)
    lat_tiles = None if h_ctx is None else h.shape[0] // tm
    kernel = functools.partial(_ffn_kernel, mod_base=mod_base, alpha=alpha, lat_tiles=lat_tiles)
    if h_ctx is None:
        sources, source_specs = [h], [pl.BlockSpec((tm, d), lambda i: (i, 0))]
    else:
        sources, source_specs = [h, h_ctx], _two_source_specs(tm, d, lat_tiles)
    return pl.pallas_call(
        kernel,
        grid=(n_rows // tm,),
        in_specs=source_specs + [
                  pl.BlockSpec((1, N_MOD, d), lambda i: (jnp.minimum(i * tm // seq, batch), 0, 0)),
                  _resident_layer(w_up, layer),
                  _resident_layer(w_dn, layer),
                  pl.BlockSpec((1, d), lambda i: (0, 0)),
                  pl.BlockSpec((1, d), lambda i: (0, 0))],
        out_specs=pl.BlockSpec((tm, d), lambda i: (i, 0)),
        out_shape=jax.ShapeDtypeStruct((n_rows, d), F32),
        scratch_shapes=[pltpu.VMEM((tm, f), BF16)],
        compiler_params=_params(1),
        name="ffn",
    )(*sources, mods3, w_up, w_dn, ln_g.reshape(1, d), ln_b.reshape(1, d))


GATE_COL0 = 3 * SC_WIDTH + 2 * ML_QK_W + 2 * ML_V_W


def _proj_cols(d):
    parts = ((("sb", SC_WIDTH), ("sc", SC_WIDTH), ("sx", SC_WIDTH),
              ("mq", ML_QK_W), ("mk", ML_QK_W), ("mv", ML_V_W), ("mo", ML_V_W)),
             (("dq", DA_QK_W), ("dk", DA_QK_W), ("dv", DA_V_W), ("gs", d), ("gm", d), ("gd", d)))
    cols, widths = {}, []
    for part, sizes in enumerate(parts):
        off = 0
        for name, w in sizes:
            cols[name] = (part, off, off + w)
            off += w
        widths.append(off)
    return cols, widths


PROJ_W_ROWS = 1024


def _proj_weight_kernel(w_ref, o_ref):
    o_ref[...] = w_ref[...].astype(BF16)


def _proj_weights(w_in):
    w_t = jnp.swapaxes(w_in, 1, 2)
    depth, n_in, d = w_t.shape
    spec = pl.BlockSpec((None, PROJ_W_ROWS, d), lambda l, i: (l, i, 0))
    return pl.pallas_call(
        _proj_weight_kernel,
        grid=(depth, pl.cdiv(n_in, PROJ_W_ROWS)),
        in_specs=[spec],
        out_specs=spec,
        out_shape=jax.ShapeDtypeStruct(w_t.shape, BF16),
        compiler_params=_params(2),
        name="proj_weights",
    )(w_t)


def _mixer_in_kernel(h_ref, mod_ref, w_ref, bh_ref, bt_ref, bg_ref, cos_ref, sin_ref,
                     sb_o, p_o, mq_o, mk_o, mv_o, smo_o, mg_o, dq_o, dk_o, dv_o, gs_o, gm_o, gd_o):
    d = h_ref.shape[1]
    tm = h_ref.shape[0]
    cols, _ = _proj_cols(d)
    h = h_ref[...]
    shift = mod_ref[0, 3:4, :]
    scale = mod_ref[0, 4:5, :]
    u = (h * (1.0 + scale) + shift).astype(BF16)

    def proj(name):
        part, lo, hi = cols[name]
        b_ref = (bh_ref, bt_ref)[part]
        row0 = 0 if part == 0 else GATE_COL0 + N_GATES
        return _dot_nt(u, w_ref[row0 + lo:row0 + hi, :]) + b_ref[:, lo:hi]

    sb_o[...] = proj("sb").astype(BF16)
    p_o[...] = (proj("sc") * proj("sx")).astype(BF16)
    mq_o[...] = proj("mq").astype(BF16)
    mk_o[...] = (proj("mk") * (ML_DQK ** -0.5)).astype(BF16)
    mv_o[...] = proj("mv").astype(BF16)
    smo_o[...] = jax.nn.sigmoid(proj("mo")).astype(BF16)
    mg_o[...] = _dot_nt(u, w_ref[GATE_COL0:GATE_COL0 + N_GATES, :]) + bg_ref[...]

    cos_t = cos_ref[...]
    sin_t = sin_ref[...]
    lane = lax.broadcasted_iota(jnp.int32, cos_t.shape, 1)
    first_half = (lane % DA_HALF) < (DA_HALF // 2)

    def rope_store(z, out_ref, mult):
        for k in range(z.shape[1] // LANES):
            x = z[:, k * LANES:(k + 1) * LANES]
            partner = jnp.where(first_half,
                                pltpu.roll(x, LANES - DA_HALF // 2, 1),
                                pltpu.roll(x, DA_HALF // 2, 1))
            out_ref[:, k * LANES:(k + 1) * LANES] = ((x * cos_t + partner * sin_t) * mult).astype(BF16)

    rope_store(proj("dq"), dq_o, (DA_HALF ** -0.5) * LOG2_E)
    rope_store(proj("dk"), dk_o, 1.0)
    dv = proj("dv").astype(BF16)
    ones = jnp.ones((tm, DA_DV), BF16)
    for k in range(DA_HEADS):
        dv_o[:, 2 * k * DA_DV:(2 * k + 1) * DA_DV] = dv[:, k * DA_DV:(k + 1) * DA_DV]
        dv_o[:, (2 * k + 1) * DA_DV:(2 * k + 2) * DA_DV] = ones
    gs_o[...] = jax.nn.sigmoid(proj("gs")).astype(BF16)
    gm_o[...] = jax.nn.sigmoid(proj("gm")).astype(BF16)
    gd_o[...] = jax.nn.sigmoid(proj("gd")).astype(BF16)


def _mixer_in(h, mods3, w_t, b_head, b_tail, b_gate, cos_t, sin_t, *, layer, seq, batch, n_lat, tm=256):
    rows, d = h.shape
    _, (n_head, n_tail) = _proj_cols(d)
    tm = min(tm, seq)
    lat_tiles = n_lat // tm
    rope_blocks = seq // tm

    def row_spec(width):
        return pl.BlockSpec((tm, width), lambda i: (i, 0))

    def rope_map(i):
        return (jnp.where(i < lat_tiles, i % rope_blocks, rope_blocks), 0)

    widths = (SC_WIDTH, SC_WIDTH, ML_QK_W, ML_QK_W, ML_V_W, ML_V_W, N_GATES,
              DA_QK_W, DA_QK_W, 2 * DA_V_W, d, d, d)
    dtypes = (BF16,) * 6 + (F32,) + (BF16,) * 6
    return pl.pallas_call(
        _mixer_in_kernel,
        grid=(rows // tm,),
        in_specs=[row_spec(d),
                  pl.BlockSpec((1, N_MOD, d), lambda i: (jnp.minimum(i * tm // seq, batch), 0, 0)),
                  _resident_layer(w_t, layer),
                  _resident((1, n_head)),
                  _resident((1, n_tail)),
                  _resident((1, N_GATES)),
                  pl.BlockSpec((tm, LANES), rope_map),
                  pl.BlockSpec((tm, LANES), rope_map)],
        out_specs=[row_spec(w) for w in widths],
        out_shape=[jax.ShapeDtypeStruct((rows, w), dt) for w, dt in zip(widths, dtypes)],
        compiler_params=_params(1),
        name="mixer_in",
    )(h, mods3, w_t, b_head, b_tail, b_gate, cos_t, sin_t)


def _log_sigmoid(x):
    return jnp.minimum(x, 0.0) - jnp.log(1.0 + jnp.exp(-jnp.abs(x)))


def _split3(x):
    pieces, rest = [], x
    for _ in range(3):
        piece = rest.astype(BF16)
        pieces.append(piece)
        rest = rest - piece.astype(F32)
    return pieces


def _mlstm_gate_kernel(g_ref, o_ref, o3_ref):
    blk = ML_BLOCK
    t_idx = lax.broadcasted_iota(jnp.int32, (blk, blk), 0)
    s_idx = lax.broadcasted_iota(jnp.int32, (blk, blk), 1)
    tril = (s_idx <= t_idx).astype(BF16)
    lane = lax.broadcasted_iota(jnp.int32, (blk, N_GATES), 1)
    src = lax.broadcasted_iota(jnp.int32, (N_GATES, LANES), 0)
    dst = lax.broadcasted_iota(jnp.int32, (N_GATES, LANES), 1)
    places = [(dst == src + term * N_GATES).astype(BF16) for term in range(3)]
    for j in range(g_ref.shape[0] // blk):
        sl = pl.ds(j * blk, blk)
        g = g_ref[sl, :]
        ls = _log_sigmoid(g)
        prefix = functools.reduce(jnp.add, [_dot(tril, piece) for piece in _split3(ls)])
        suffix = prefix[blk - 1:blk, :] - prefix + ls
        out = jnp.where(lane < 2 * ML_HEADS, g, jnp.where(lane < 3 * ML_HEADS, prefix, suffix))
        o_ref[sl, :] = out
        placed = functools.reduce(jnp.add, [_dot(piece, place) for piece, place in zip(_split3(out), places)])
        o3_ref[sl, :] = placed.astype(BF16)


def _mlstm_gates(mg):
    rows = mg.shape[0]
    step = math.gcd(rows, 8 * ML_BLOCK)
    return pl.pallas_call(
        _mlstm_gate_kernel,
        grid=(rows // step,),
        in_specs=[pl.BlockSpec((step, N_GATES), lambda i: (i, 0))],
        out_specs=[pl.BlockSpec((step, N_GATES), lambda i: (i, 0)),
                   pl.BlockSpec((step, LANES), lambda i: (i, 0))],
        out_shape=[jax.ShapeDtypeStruct((rows, N_GATES), F32),
                   jax.ShapeDtypeStruct((rows, LANES), BF16)],
        compiler_params=_params(1),
        name="mlstm_gates",
    )(mg)


N_ROWS_PAD = BF16_SUBLANES
ST_ROWS = 2 * N_ROWS_PAD + 2 * ML_DV


def _mlstm_scan_kernel(ql_ref, qc_ref, kl_ref, kc_ref, ktl_ref, ktc_ref, vtl_ref, vtc_ref,
                       g3l_ref, g3c_ref, grl_ref, grc_ref, *rest, need_ctx):
    if need_ctx:
        hl_ref, hc_ref, st_in, m_in = rest
    else:
        hl_ref, st_in, m_in = rest
        hc_ref = None
    blk = ML_BLOCK
    n_lat_blk = ql_ref.shape[0] // blk
    n_ctx_blk = qc_ref.shape[0] // blk
    head = pl.program_id(1)

    s_idx = lax.broadcasted_iota(jnp.int32, (blk, blk), 0)
    t_idx = lax.broadcasted_iota(jnp.int32, (blk, blk), 1)
    lane_t = lax.broadcasted_iota(jnp.int32, (1, blk), 1)
    ones_n = jnp.ones((N_ROWS_PAD, blk), BF16)
    sel_src = lax.broadcasted_iota(jnp.int32, (LANES, 2 * LANES), 0)
    sel_dst = lax.broadcasted_iota(jnp.int32, (LANES, 2 * LANES), 1)
    sel_bwd = sel_dst >= LANES
    col = sel_src % N_GATES
    in_terms = sel_src < 3 * N_GATES
    plus = jnp.logical_and(col == head + jnp.where(sel_bwd, ML_HEADS, 0), in_terms)
    minus = jnp.logical_and(col == head + jnp.where(sel_bwd, 3 * ML_HEADS, 2 * ML_HEADS), in_terms)
    sel = (plus.astype(F32) - minus.astype(F32)).astype(BF16)

    def gate_rows(gr_ref, j, bwd):
        i_idx = head + (ML_HEADS if bwd else 0)
        f_idx = head + (3 * ML_HEADS if bwd else 2 * ML_HEADS)
        return gr_ref[j, pl.ds(f_idx, 1), :], gr_ref[j, pl.ds(i_idx, 1), :]

    def scan_states():
        ctx_blocks = [(grc_ref, ktc_ref, vtc_ref, j, j) for j in range(n_ctx_blk)]
        lat_blocks = [(grl_ref, ktl_ref, vtl_ref, j, n_ctx_blk + j) for j in range(n_lat_blk)]
        orders = (ctx_blocks + lat_blocks, ctx_blocks[::-1] + lat_blocks[::-1])
        m_prev = [jnp.zeros((1, 1), F32)] * 2
        ct_run = [jnp.zeros((ML_DV, ML_DQK), F32)] * 2
        n_run = [jnp.zeros((N_ROWS_PAD, ML_DQK), F32)] * 2
        for step in range(len(orders[0])):
            for d, bwd in enumerate((False, True)):
                gr_ref, kt_ref, vt_ref, j, slot = orders[d][step]
                c_row0 = 2 * N_ROWS_PAD + d * ML_DV
                f_row, i_row = gate_rows(gr_ref, j, bwd)
                b_end = jnp.sum(jnp.where(lane_t == (0 if bwd else blk - 1), f_row, 0.0), axis=1,
                                keepdims=True)
                g_log = b_end + (i_row - f_row)
                m_new = jnp.maximum(b_end + m_prev[d], jnp.max(g_log, axis=1, keepdims=True))
                a_prev = jnp.exp(b_end + m_prev[d] - m_new)
                kw = (kt_ref[j].astype(F32) * jnp.exp(g_log - m_new)).astype(BF16)
                st_in[slot, d * N_ROWS_PAD:(d + 1) * N_ROWS_PAD, :] = n_run[d].astype(BF16)
                st_in[slot, c_row0:c_row0 + ML_DV, :] = ct_run[d].astype(BF16)
                m_in[slot, :, d * LANES:(d + 1) * LANES] = jnp.broadcast_to(m_prev[d], (8, LANES))
                ct_run[d] = a_prev * ct_run[d] + _dot_nt(vt_ref[j], kw)
                n_run[d] = a_prev * n_run[d] + _dot_nt(ones_n, kw)
                m_prev[d] = m_new

    def outputs(q, k, vt, g3, rows, slot):
        qk_t = _dot_nt(k, q)
        state = _dot_nt(st_in[slot], q)
        r_both = _dot(g3, sel)
        m_prev_both = m_in[slot][0:1, :]
        scaled, inter = [], []
        for d, bwd in enumerate((False, True)):
            f_row, _ = rows[d]
            r_rep = r_both[:, d * LANES:(d + 1) * LANES]
            mask = (s_idx >= t_idx) if bwd else (s_idx <= t_idx)
            d_log = jnp.where(mask, _wide(r_rep, blk) + f_row, NEG_BIG)
            m_loc = jnp.max(d_log, axis=0, keepdims=True)
            s = qk_t * jnp.exp(d_log - m_loc)
            den_loc = jnp.sum(s, axis=0, keepdims=True)
            m_inter = f_row + m_prev_both[:, d * LANES:d * LANES + 1]
            m_t = jnp.maximum(m_inter, m_loc)
            w_inter = jnp.exp(m_inter - m_t)
            w_loc = jnp.exp(m_loc - m_t)
            qn = state[d * N_ROWS_PAD:d * N_ROWS_PAD + 1, :]
            den = w_inter * qn + w_loc * den_loc
            inv = 1.0 / jnp.maximum(jnp.abs(den), jnp.exp(-m_t))
            scaled.append(s * (w_loc * inv))
            c_row0 = 2 * N_ROWS_PAD + d * ML_DV
            inter.append(state[c_row0:c_row0 + ML_DV, :] * (w_inter * inv))
        return _dot(vt, (scaled[0] + scaled[1]).astype(BF16)) + inter[0] + inter[1]

    def lat_rows(j):
        return pl.ds(pl.multiple_of(j * blk, blk), blk)

    scan_states()

    if need_ctx:
        for j in range(n_ctx_blk):
            sl = pl.ds(j * blk, blk)
            rows = (gate_rows(grc_ref, j, False), gate_rows(grc_ref, j, True))
            out_t = outputs(qc_ref[sl, :], kc_ref[sl, :], vtc_ref[j], g3c_ref[sl, :], rows, j)
            hc_ref[sl, :] = out_t.T.astype(hc_ref.dtype)

    def out_body(j, carry):
        sl = lat_rows(j)
        rows = (gate_rows(grl_ref, j, False), gate_rows(grl_ref, j, True))
        out_t = outputs(ql_ref[sl, :], kl_ref[sl, :], vtl_ref[j], g3l_ref[sl, :], rows, n_ctx_blk + j)
        hl_ref[sl, :] = out_t.T.astype(hl_ref.dtype)
        return carry

    lax.fori_loop(0, n_lat_blk, out_body, 0, unroll=4)


def _mlstm_scan(mq, mk, mv, mg, *, seq, ctx_len, batch, n_lat, need_ctx):
    blk = ML_BLOCK
    rows = mq.shape[0]
    ctx0 = n_lat // ctx_len
    n_blk = (seq + ctx_len) // blk
    gates, gates3 = _mlstm_gates(mg)
    gates_rows = gates.T.reshape(N_GATES, rows // blk, blk).transpose(1, 0, 2)
    mk_t = mk.reshape(rows // blk, blk, ML_QK_W).transpose(0, 2, 1)
    mv_t = mv.reshape(rows // blk, blk, ML_V_W).transpose(0, 2, 1)

    def lat(width):
        return pl.BlockSpec((seq, width), lambda b, h: (b, h))

    def ctx(width):
        return pl.BlockSpec((ctx_len, width), lambda b, h: (ctx0 + b, h))

    def lat_t(width):
        return pl.BlockSpec((seq // blk, width, blk), lambda b, h: (b, h, 0))

    def ctx_t(width):
        return pl.BlockSpec((ctx_len // blk, width, blk), lambda b, h: (ctx0 + b, h, 0))

    in_specs = [lat(ML_DQK), ctx(ML_DQK), lat(ML_DQK), ctx(ML_DQK),
                lat_t(ML_DQK), ctx_t(ML_DQK), lat_t(ML_DV), ctx_t(ML_DV),
                pl.BlockSpec((seq, LANES), lambda b, h: (b, 0)),
                pl.BlockSpec((ctx_len, LANES), lambda b, h: (ctx0 + b, 0)),
                pl.BlockSpec((seq // blk, N_GATES, blk), lambda b, h: (b, 0, 0)),
                pl.BlockSpec((ctx_len // blk, N_GATES, blk), lambda b, h: (ctx0 + b, 0, 0))]
    out_specs = [pl.BlockSpec((seq, ML_DV), lambda b, h: (b, h))]
    out_shape = [jax.ShapeDtypeStruct((n_lat, ML_V_W), BF16)]
    if need_ctx:
        out_specs.append(pl.BlockSpec((ctx_len, ML_DV), lambda b, h: (b, h)))
        out_shape.append(jax.ShapeDtypeStruct((batch * ctx_len, ML_V_W), BF16))
    scratch = [pltpu.VMEM((n_blk, ST_ROWS, ML_DQK), BF16),
               pltpu.VMEM((n_blk, 8, 2 * LANES), F32)]
    out = pl.pallas_call(
        functools.partial(_mlstm_scan_kernel, need_ctx=need_ctx),
        grid=(batch, ML_HEADS),
        in_specs=in_specs,
        out_specs=out_specs,
        out_shape=out_shape,
        scratch_shapes=scratch,
        compiler_params=_params(2),
        name="mlstm",
    )(mq, mq, mk, mk, mk_t, mk_t, mv_t, mv_t, gates3, gates3, gates_rows, gates_rows)
    return out if need_ctx else (out[0], None)


def _diffattn_kernel(lam_ref, g_ref, q_ref, *refs, n_seg, lam_init):
    k_refs = refs[:n_seg]
    v_refs = refs[n_seg:2 * n_seg]
    o_ref = refs[2 * n_seg]
    lv = lam_ref[...]
    lam = (jnp.exp(jnp.sum(lv[0:1] * lv[1:2], axis=1, keepdims=True))
           - jnp.exp(jnp.sum(lv[2:3] * lv[3:4], axis=1, keepdims=True)) + lam_init)
    for hh in range(q_ref.shape[1] // LANES):
        q = q_ref[:, hh * LANES:(hh + 1) * LANES]
        ks = [k_ref[:, hh * LANES:(hh + 1) * LANES] for k_ref in k_refs]
        vs = [v_ref[:, 2 * hh * DA_DV:2 * (hh + 1) * DA_DV] for v_ref in v_refs]
        lane = lax.broadcasted_iota(jnp.int32, q.shape, 1)
        zero = jnp.zeros_like(q)
        halves = [jnp.where(lane < DA_HALF, q, zero), jnp.where(lane >= DA_HALF, q, zero)]
        scores = [[_dot_nt(qh, k) for k in ks] for qh in halves]
        maxes = [functools.reduce(jnp.maximum, [jnp.max(s, axis=1, keepdims=True) for s in ss])
                 for ss in scores]
        probs = [[jnp.exp2((s - m).astype(BF16)) for s in ss] for ss, m in zip(scores, maxes)]
        heads = []
        for ps in probs:
            acc = None
            for p, v in zip(ps, vs):
                part = _dot(p, v)
                acc = part if acc is None else acc + part
            heads.append(acc[:, :DA_DV] * (1.0 / acc[:, DA_DV:]))
        o = heads[0] - lam * heads[1]
        ms = jnp.mean(o * o, axis=1, keepdims=True)
        o_ref[:, hh * DA_DV:(hh + 1) * DA_DV] = (o * lax.rsqrt(ms + LN_EPS) * g_ref[...]
                                                 * (1.0 - lam_init)).astype(o_ref.dtype)


def _diffattn(dq, dk, dv, lam_vecs, da_g, *, q_row0, q_len, segments, batch, lam_init, tq=1024,
              heads_per_step=1):
    tq = min(tq, q_len)
    nq = q_len // tq
    q0 = q_row0 // tq
    hps = heads_per_step

    def seg_spec(row0, length, width):
        return pl.BlockSpec((length, hps * width), lambda b, h, i: (row0 // length + b, h))

    return pl.pallas_call(
        functools.partial(_diffattn_kernel, n_seg=len(segments), lam_init=lam_init),
        grid=(batch, DA_HEADS // hps, nq),
        in_specs=[pl.BlockSpec(lam_vecs.shape, lambda b, h, i: (0, 0)),
                  pl.BlockSpec((1, DA_DV), lambda b, h, i: (0, 0)),
                  pl.BlockSpec((tq, hps * LANES), lambda b, h, i: (q0 + b * nq + i, h))]
                 + [seg_spec(r0, ln, LANES) for r0, ln in segments]
                 + [seg_spec(r0, ln, 2 * DA_DV) for r0, ln in segments],
        out_specs=pl.BlockSpec((tq, hps * DA_DV), lambda b, h, i: (b * nq + i, h)),
        out_shape=jax.ShapeDtypeStruct((batch * q_len, DA_V_W), BF16),
        compiler_params=_params(3),
        name="diffattn",
    )(lam_vecs, da_g.reshape(1, DA_DV), dq, *([dk] * len(segments)), *([dv] * len(segments)))


def _mixer_out_kernel(h_ref, mod_ref, sb_ref, p_ref, pprev_ref, pnext_ref, smo_ref, *refs,
                      seq, ctx_len, n_lat, alpha, two_source):
    tm = h_ref.shape[0]
    r0 = pl.program_id(0) * tm
    is_lat = r0 < n_lat
    if two_source:
        hml_l, hml_c, yda_l, yda_c = refs[:4]
        refs = refs[4:]
        hml = jnp.where(is_lat, hml_l[...], hml_c[...])
        yda = jnp.where(is_lat, yda_l[...], yda_c[...])
    else:
        hml, yda = refs[0][...], refs[1][...]
        refs = refs[2:]
    gs_ref, gm_ref, gd_ref, convw_ref, wsc_ref, wml_ref, wda_ref, wo_ref, lng_ref, lnb_ref, o_ref = refs
    row = lax.broadcasted_iota(jnp.int32, (tm, 1), 0)
    pos = jnp.where(is_lat, (r0 + row) % seq, (r0 - n_lat + row) % ctx_len)
    seq_len = jnp.where(is_lat, seq, ctx_len)
    first = pos == 0
    last = pos == seq_len - 1

    p = p_ref[...].astype(F32)
    prev_row = pprev_ref[...].astype(F32)[BF16_SUBLANES - 1:BF16_SUBLANES, :]
    next_row = pnext_ref[...].astype(F32)[0:1, :]
    p_before = jnp.where(first, 0.0, jnp.where(row == 0, prev_row, pltpu.roll(p, 1, 0)))
    p_after = jnp.where(last, 0.0, jnp.where(row == tm - 1, next_row, pltpu.roll(p, tm - 1, 0)))
    cw = convw_ref[...]
    conv = cw[0:1, :] * p_before + cw[1:2, :] * p + cw[2:3, :] * p_after
    y_sc = (sb_ref[...].astype(F32) * conv).astype(BF16)
    y_ml = smo_ref[...] * hml
    y = (gs_ref[...].astype(F32) * _dot(y_sc, wsc_ref[...].astype(BF16))
         + gm_ref[...].astype(F32) * _dot(y_ml, wml_ref[...].astype(BF16))
         + gd_ref[...].astype(F32) * _dot(yda, wda_ref[...].astype(BF16)))
    y = _dot(y.astype(BF16), wo_ref[...].astype(BF16))
    h = h_ref[...]
    o_ref[...] = _layer_norm(alpha * h + mod_ref[0, 5:6, :] * y, lng_ref[...], lnb_ref[...])


def _mixer_out(h, mods3, sb, p, smo, hml, yda, gs, gm, gd, conv_w, w_sc, w_ml, w_da, w_o, ln_g, ln_b,
               *, layer, n_rows, seq, ctx_len, batch, n_lat, alpha, hml_ctx=None, yda_ctx=None, tm=512):
    d = h.shape[1]
    tm = math.gcd(math.gcd(tm, seq), batch * ctx_len)
    halo = BF16_SUBLANES
    last_halo = p.shape[0] // halo - 1
    two_source = hml_ctx is not None

    def row_spec(width):
        return pl.BlockSpec((tm, width), lambda i: (i, 0))

    if two_source:
        branch = [hml, hml_ctx, yda, yda_ctx]
        branch_specs = (_two_source_specs(tm, ML_V_W, n_lat // tm) + _two_source_specs(tm, DA_V_W, n_lat // tm))
    else:
        branch = [hml, yda]
        branch_specs = [row_spec(ML_V_W), row_spec(DA_V_W)]
    kernel = functools.partial(_mixer_out_kernel, seq=seq, ctx_len=ctx_len, n_lat=n_lat, alpha=alpha,
                               two_source=two_source)
    return pl.pallas_call(
        kernel,
        grid=(n_rows // tm,),
        in_specs=[row_spec(d),
                  pl.BlockSpec((1, N_MOD, d), lambda i: (jnp.minimum(i * tm // seq, batch), 0, 0)),
                  row_spec(SC_WIDTH), row_spec(SC_WIDTH),
                  pl.BlockSpec((halo, SC_WIDTH), lambda i: (jnp.maximum(i * (tm // halo) - 1, 0), 0)),
                  pl.BlockSpec((halo, SC_WIDTH),
                               lambda i: (jnp.minimum((i + 1) * (tm // halo), last_halo), 0)),
                  row_spec(ML_V_W)] + branch_specs + [
                  row_spec(d), row_spec(d), row_spec(d),
                  _resident((SC_KSIZE, SC_WIDTH)),
                  _resident_layer(w_sc, layer), _resident_layer(w_ml, layer), _resident_layer(w_da, layer),
                  _resident_layer(w_o, layer),
                  pl.BlockSpec((1, d), lambda i: (0, 0)),
                  pl.BlockSpec((1, d), lambda i: (0, 0))],
        out_specs=row_spec(d),
        out_shape=jax.ShapeDtypeStruct((n_rows, d), F32),
        compiler_params=_params(1),
        name="mixer_out",
    )(h, mods3, sb, p, p, p, smo, *branch, gs, gm, gd, conv_w, w_sc, w_ml, w_da, w_o,
      ln_g.reshape(1, d), ln_b.reshape(1, d))


def _rope_tables(seq, tile):
    n_freq = DA_HALF // 4
    t = jnp.arange(seq)
    row_ids = (t // GRID_W).astype(F32)
    col_ids = (t % GRID_W).astype(F32)
    inv = ROPE_BASE ** (-jnp.arange(n_freq, dtype=F32) / n_freq)
    ang = jnp.concatenate([row_ids[:, None] * inv, col_ids[:, None] * inv], axis=-1)
    cos, sin = jnp.cos(ang), jnp.sin(ang)
    reps = LANES // DA_HALF
    cos_t = jnp.tile(jnp.concatenate([cos, cos], axis=-1), (1, reps))
    sin_t = jnp.tile(jnp.concatenate([-sin, sin], axis=-1), (1, reps))
    cos_t = jnp.concatenate([cos_t, jnp.ones((tile, LANES), F32)], axis=0)
    sin_t = jnp.concatenate([sin_t, jnp.zeros((tile, LANES), F32)], axis=0)
    return cos_t, sin_t


def kernel(x, c, ctx, c_ctx, w_ada, b_ada, ln_g, ln_b, ffn1_up, ffn1_down, ffn2_up, ffn2_down, w_in, b_in,
           conv_w, w_sc, w_ml, w_da, w_o, lam_q1, lam_k1, lam_q2, lam_k2, da_norm_g):
    batch, seq, d = x.shape
    ctx_len = ctx.shape[1]
    depth = w_ada.shape[0]
    alpha = (2 * depth) ** 0.25
    n_lat = batch * seq
    n_ctx = batch * ctx_len
    mixer_in_tile = min(256, seq)

    n_cond = -(-(batch + 1) // 8) * 8
    cc = jnp.concatenate([c, c_ctx[None, :], jnp.zeros((n_cond - batch - 1, d), F32)], axis=0)
    mods = _ada(cc, w_ada, b_ada).reshape(depth, n_cond, N_MOD, d)

    cos_t, sin_t = _rope_tables(seq, mixer_in_tile)
    gate_lo = GATE_COL0
    gate_hi = gate_lo + N_GATES
    w_proj_t = _proj_weights(w_in)

    h, h_ctx = x.reshape(n_lat, d), ctx.reshape(n_ctx, d)
    for l in range(depth):
        last = l == depth - 1
        lam_init = 0.8 - 0.6 * math.exp(-0.3 * l)
        mods3 = mods[l]
        b_head = b_in[l, :gate_lo][None, :]
        b_tail = b_in[l, gate_hi:][None, :]
        b_gate = b_in[l, gate_lo:gate_hi][None, :]
        lam_vecs = jnp.stack([lam_q1[l], lam_k1[l], lam_q2[l], lam_k2[l]]).astype(F32)

        h = _ffn(h, mods3, ffn1_up, ffn1_down, ln_g[l, 0], ln_b[l, 0], layer=l,
                 mod_base=0, n_rows=n_lat + n_ctx, seq=seq, batch=batch, alpha=alpha, h_ctx=h_ctx)
        h_ctx = None

        (sb, p, mq, mk, mv, smo, mg, dq, dk, dv, gs, gm, gd) = _mixer_in(
            h, mods3, w_proj_t, b_head, b_tail, b_gate, cos_t, sin_t,
            layer=l, seq=seq, batch=batch, n_lat=n_lat, tm=mixer_in_tile)

        hml, hml_ctx = _mlstm_scan(mq, mk, mv, mg, seq=seq, ctx_len=ctx_len, batch=batch, n_lat=n_lat,
                                   need_ctx=not last)
        yda = _diffattn(dq, dk, dv, lam_vecs, da_norm_g[l], q_row0=0, q_len=seq,
                        segments=((0, seq), (n_lat, ctx_len)), batch=batch, lam_init=lam_init)
        yda_ctx = None
        if not last:
            yda_ctx = _diffattn(dq, dk, dv, lam_vecs, da_norm_g[l], q_row0=n_lat, q_len=ctx_len,
                                segments=((n_lat, ctx_len),), batch=batch, lam_init=lam_init,
                                heads_per_step=DA_HEADS)

        n_rows = n_lat if last else n_lat + n_ctx
        h = _mixer_out(h, mods3, sb, p, smo, hml, yda, gs, gm, gd, conv_w[l], w_sc, w_ml, w_da, w_o,
                       ln_g[l, 1], ln_b[l, 1], hml_ctx=hml_ctx, yda_ctx=yda_ctx, layer=l,
                       n_rows=n_rows, seq=seq, ctx_len=ctx_len, batch=batch, n_lat=n_lat, alpha=alpha)
        h = _ffn(h, mods3, ffn2_up, ffn2_down, ln_g[l, 2], ln_b[l, 2], layer=l,
                 mod_base=6, n_rows=n_rows, seq=seq, batch=batch, alpha=alpha)
    return h[:n_lat].reshape(batch, seq, d)
```

```python
import functools
import math

import jax
import jax.numpy as jnp
from jax import lax
from jax.experimental import pallas as pl
from jax.experimental.pallas import tpu as pltpu

GRID_W = 64
N_MOD = 9
SC_WIDTH = 512
SC_KSIZE = 3
ML_HEADS = 4
ML_DQK = 128
ML_DV = 256
DA_HEADS = 4
DA_HALF = 64
DA_DV = 2 * DA_HALF
ROPE_BASE = 10000.0
LN_EPS = 1e-5

ML_QK_W = ML_HEADS * ML_DQK
ML_V_W = ML_HEADS * ML_DV
DA_QK_W = DA_HEADS * 2 * DA_HALF
DA_V_W = DA_HEADS * DA_DV
N_GATES = 4 * ML_HEADS

ML_BLOCK = 256
FFN_COLS = 256
LANES = 128
BF16_SUBLANES = 16
V7X_VMEM_BYTES = 64 * 1024 * 1024
VMEM_LIMIT = V7X_VMEM_BYTES - 8 * 1024 * 1024

F32 = jnp.float32
BF16 = jnp.bfloat16
NEG_BIG = -1e30
LOG2_E = 1.4426950408889634

assert ML_DQK == LANES and DA_DV == LANES and ML_BLOCK == 2 * LANES


def _dot(a, b, precision=None):
    return jnp.dot(a, b, preferred_element_type=F32, precision=precision)


def _dot_nt(a, b):
    return lax.dot_general(a, b, (((1,), (1,)), ((), ())), preferred_element_type=F32)


def _wide(x, width):
    return jnp.concatenate([x] * (width // LANES), axis=1)


def _layer_norm(y, g, b):
    mu = jnp.mean(y, axis=-1, keepdims=True)
    yc = y - mu
    var = jnp.mean(yc * yc, axis=-1, keepdims=True)
    return yc * lax.rsqrt(var + LN_EPS) * g + b


def _resident(shape):
    return pl.BlockSpec(shape, lambda *_: (0,) * len(shape), pipeline_mode=pl.Buffered(1))


def _resident_layer(stacked, layer):
    n = stacked.ndim - 1
    return pl.BlockSpec((None,) + stacked.shape[1:], lambda *_: (layer,) + (0,) * n,
                        pipeline_mode=pl.Buffered(1))


def _params(n_axes):
    return pltpu.CompilerParams(dimension_semantics=("parallel",) * n_axes,
                                vmem_limit_bytes=VMEM_LIMIT)


def _ada_kernel(c_ref, w_ref, b_ref, o_ref):
    c = c_ref[...]
    a = (c * jax.nn.sigmoid(c)).astype(BF16)
    o_ref[0] = _dot(a, w_ref[0].astype(BF16)) + b_ref[0]


def _ada(cc, w_ada, b_ada):
    depth, d, n = w_ada.shape
    tn = d
    return pl.pallas_call(
        _ada_kernel,
        grid=(depth, n // tn),
        in_specs=[pl.BlockSpec(cc.shape, lambda l, j: (0, 0)),
                  pl.BlockSpec((1, d, tn), lambda l, j: (l, 0, j)),
                  pl.BlockSpec((1, 1, tn), lambda l, j: (l, 0, j))],
        out_specs=pl.BlockSpec((1, cc.shape[0], tn), lambda l, j: (l, 0, j)),
        out_shape=jax.ShapeDtypeStruct((depth, cc.shape[0], n), F32),
        compiler_params=_params(2),
        name="ada",
    )(cc, w_ada, b_ada.reshape(depth, 1, n))


def _two_source_specs(tm, width, lat_tiles):
    return [pl.BlockSpec((tm, width), lambda i: (jnp.minimum(i, lat_tiles - 1), 0)),
            pl.BlockSpec((tm, width), lambda i: (jnp.maximum(i - lat_tiles, 0), 0))]


def _ffn_kernel(*refs, mod_base, alpha, lat_tiles, cast_extra):
    if lat_tiles is None:
        h_ref, hc_ref = refs[0], None
        refs = refs[1:]
    else:
        h_ref, hc_ref = refs[:2]
        refs = refs[2:]
    if cast_extra:
        mod_ref, wup_ref, wdn_ref, lng_ref, lnb_ref, extra_ref, o_ref, extra_o, g_scr = refs
        extra_o[...] = extra_ref[...].astype(BF16)
    else:
        mod_ref, wup_ref, wdn_ref, lng_ref, lnb_ref, o_ref, g_scr = refs
    h = h_ref[...] if hc_ref is None else jnp.where(pl.program_id(0) < lat_tiles, h_ref[...], hc_ref[...])
    f = wdn_ref.shape[0]
    shift = mod_ref[0, mod_base:mod_base + 1, :]
    scale = mod_ref[0, mod_base + 1:mod_base + 2, :]
    gate = mod_ref[0, mod_base + 2:mod_base + 3, :]
    u = (h * (1.0 + scale) + shift).astype(BF16)
    for c in range(f // FFN_COLS):
        lo, hi = c * FFN_COLS, (c + 1) * FFN_COLS
        a = _dot(u, wup_ref[:, lo:hi].astype(BF16))
        v = _dot(u, wup_ref[:, f + lo:f + hi].astype(BF16))
        g_scr[:, lo:hi] = (a * jax.nn.sigmoid(a) * v).astype(BF16)
    d = _dot(g_scr[...], wdn_ref[...].astype(BF16))
    o_ref[...] = _layer_norm(alpha * h + (0.5 * gate) * d, lng_ref[...], lnb_ref[...])


def _ffn(h, mods3, w_up, w_dn, ln_g, ln_b, *, layer, mod_base, n_rows, seq, batch, alpha, h_ctx=None,
         cast_stack=None, tm=512):
    d = h.shape[1]
    f = w_dn.shape[1]
    tm = min(tm, seq)
    steps = n_rows // tm
    lat_tiles = None if h_ctx is None else h.shape[0] // tm
    kernel = functools.partial(_ffn_kernel, mod_base=mod_base, alpha=alpha, lat_tiles=lat_tiles,
                               cast_extra=cast_stack is not None)
    if h_ctx is None:
        sources, source_specs = [h], [pl.BlockSpec((tm, d), lambda i: (i, 0))]
    else:
        sources, source_specs = [h, h_ctx], _two_source_specs(tm, d, lat_tiles)
    extra, extra_specs = [], []
    out_specs = [pl.BlockSpec((tm, d), lambda i: (i, 0))]
    out_shape = [jax.ShapeDtypeStruct((n_rows, d), F32)]
    if cast_stack is not None:
        n_extra = cast_stack.shape[1]
        rows_blk = -(-pl.cdiv(n_extra, steps) // BF16_SUBLANES) * BF16_SUBLANES
        last_blk = pl.cdiv(n_extra, rows_blk) - 1
        extra = [cast_stack]
        extra_specs = [pl.BlockSpec((None, rows_blk, d), lambda i: (layer, jnp.minimum(i, last_blk), 0))]
        out_specs.append(pl.BlockSpec((rows_blk, d), lambda i: (jnp.minimum(i, last_blk), 0)))
        out_shape.append(jax.ShapeDtypeStruct((n_extra, d), BF16))
    out = pl.pallas_call(
        kernel,
        grid=(steps,),
        in_specs=source_specs + [
                  pl.BlockSpec((1, N_MOD, d), lambda i: (jnp.minimum(i * tm // seq, batch), 0, 0)),
                  _resident_layer(w_up, layer),
                  _resident_layer(w_dn, layer),
                  pl.BlockSpec((1, d), lambda i: (0, 0)),
                  pl.BlockSpec((1, d), lambda i: (0, 0))] + extra_specs,
        out_specs=out_specs,
        out_shape=out_shape,
        scratch_shapes=[pltpu.VMEM((tm, f), BF16)],
        compiler_params=pltpu.CompilerParams(
            dimension_semantics=("arbitrary" if cast_stack is not None else "parallel",),
            vmem_limit_bytes=VMEM_LIMIT),
        name="ffn",
    )(*sources, mods3, w_up, w_dn, ln_g.reshape(1, d), ln_b.reshape(1, d), *extra)
    return out if cast_stack is not None else out[0]


GATE_COL0 = 3 * SC_WIDTH + 2 * ML_QK_W + 2 * ML_V_W


def _proj_cols(d):
    parts = ((("sb", SC_WIDTH), ("sc", SC_WIDTH), ("sx", SC_WIDTH),
              ("mq", ML_QK_W), ("mk", ML_QK_W), ("mv", ML_V_W), ("mo", ML_V_W)),
             (("dq", DA_QK_W), ("dk", DA_QK_W), ("dv", DA_V_W), ("gs", d), ("gm", d), ("gd", d)))
    cols, widths = {}, []
    for part, sizes in enumerate(parts):
        off = 0
        for name, w in sizes:
            cols[name] = (part, off, off + w)
            off += w
        widths.append(off)
    return cols, widths


def _mixer_in_kernel(h_ref, mod_ref, w_ref, bh_ref, bt_ref, bg_ref, cos_ref, sin_ref,
                     sb_o, p_o, mq_o, mk_o, mv_o, smo_o, mg_o, dq_o, dk_o, dv_o, gs_o, gm_o, gd_o):
    d = h_ref.shape[1]
    tm = h_ref.shape[0]
    cols, _ = _proj_cols(d)
    h = h_ref[...]
    shift = mod_ref[0, 3:4, :]
    scale = mod_ref[0, 4:5, :]
    u = (h * (1.0 + scale) + shift).astype(BF16)

    def proj(name):
        part, lo, hi = cols[name]
        b_ref = (bh_ref, bt_ref)[part]
        row0 = 0 if part == 0 else GATE_COL0 + N_GATES
        return _dot_nt(u, w_ref[row0 + lo:row0 + hi, :]) + b_ref[:, lo:hi]

    sb_o[...] = proj("sb").astype(BF16)
    p_o[...] = (proj("sc") * proj("sx")).astype(BF16)
    mq_o[...] = proj("mq").astype(BF16)
    mk_o[...] = (proj("mk") * (ML_DQK ** -0.5)).astype(BF16)
    mv_o[...] = proj("mv").astype(BF16)
    smo_o[...] = jax.nn.sigmoid(proj("mo")).astype(BF16)
    mg_o[...] = _dot_nt(u, w_ref[GATE_COL0:GATE_COL0 + N_GATES, :]) + bg_ref[...]

    cos_t = cos_ref[...]
    sin_t = sin_ref[...]
    lane = lax.broadcasted_iota(jnp.int32, cos_t.shape, 1)
    first_half = (lane % DA_HALF) < (DA_HALF // 2)

    def rope_store(z, out_ref, mult):
        for k in range(z.shape[1] // LANES):
            x = z[:, k * LANES:(k + 1) * LANES]
            partner = jnp.where(first_half,
                                pltpu.roll(x, LANES - DA_HALF // 2, 1),
                                pltpu.roll(x, DA_HALF // 2, 1))
            out_ref[:, k * LANES:(k + 1) * LANES] = ((x * cos_t + partner * sin_t) * mult).astype(BF16)

    rope_store(proj("dq"), dq_o, (DA_HALF ** -0.5) * LOG2_E)
    rope_store(proj("dk"), dk_o, 1.0)
    dv = proj("dv").astype(BF16)
    ones = jnp.ones((tm, DA_DV), BF16)
    for k in range(DA_HEADS):
        dv_o[:, 2 * k * DA_DV:(2 * k + 1) * DA_DV] = dv[:, k * DA_DV:(k + 1) * DA_DV]
        dv_o[:, (2 * k + 1) * DA_DV:(2 * k + 2) * DA_DV] = ones
    gs_o[...] = jax.nn.sigmoid(proj("gs")).astype(BF16)
    gm_o[...] = jax.nn.sigmoid(proj("gm")).astype(BF16)
    gd_o[...] = jax.nn.sigmoid(proj("gd")).astype(BF16)


def _mixer_in(h, mods3, w_t, b_head, b_tail, b_gate, cos_t, sin_t, *, seq, batch, n_lat, tm=256):
    rows, d = h.shape
    _, (n_head, n_tail) = _proj_cols(d)
    tm = min(tm, seq)
    lat_tiles = n_lat // tm
    rope_blocks = seq // tm

    def row_spec(width):
        return pl.BlockSpec((tm, width), lambda i: (i, 0))

    def rope_map(i):
        return (jnp.where(i < lat_tiles, i % rope_blocks, rope_blocks), 0)

    widths = (SC_WIDTH, SC_WIDTH, ML_QK_W, ML_QK_W, ML_V_W, ML_V_W, N_GATES,
              DA_QK_W, DA_QK_W, 2 * DA_V_W, d, d, d)
    dtypes = (BF16,) * 6 + (F32,) + (BF16,) * 6
    return pl.pallas_call(
        _mixer_in_kernel,
        grid=(rows // tm,),
        in_specs=[row_spec(d),
                  pl.BlockSpec((1, N_MOD, d), lambda i: (jnp.minimum(i * tm // seq, batch), 0, 0)),
                  _resident(w_t.shape),
                  _resident((1, n_head)),
                  _resident((1, n_tail)),
                  _resident((1, N_GATES)),
                  pl.BlockSpec((tm, LANES), rope_map),
                  pl.BlockSpec((tm, LANES), rope_map)],
        out_specs=[row_spec(w) for w in widths],
        out_shape=[jax.ShapeDtypeStruct((rows, w), dt) for w, dt in zip(widths, dtypes)],
        compiler_params=_params(1),
        name="mixer_in",
    )(h, mods3, w_t, b_head, b_tail, b_gate, cos_t, sin_t)


def _log_sigmoid(x):
    return jnp.minimum(x, 0.0) - jnp.log(1.0 + jnp.exp(-jnp.abs(x)))


def _split3(x):
    pieces, rest = [], x
    for _ in range(3):
        piece = rest.astype(BF16)
        pieces.append(piece)
        rest = rest - piece.astype(F32)
    return pieces


def _mlstm_gate_kernel(g_ref, o_ref, o3_ref):
    blk = ML_BLOCK
    t_idx = lax.broadcasted_iota(jnp.int32, (blk, blk), 0)
    s_idx = lax.broadcasted_iota(jnp.int32, (blk, blk), 1)
    tril = (s_idx <= t_idx).astype(BF16)
    lane = lax.broadcasted_iota(jnp.int32, (blk, N_GATES), 1)
    src = lax.broadcasted_iota(jnp.int32, (N_GATES, LANES), 0)
    dst = lax.broadcasted_iota(jnp.int32, (N_GATES, LANES), 1)
    places = [(dst == src + term * N_GATES).astype(BF16) for term in range(3)]
    for j in range(g_ref.shape[0] // blk):
        sl = pl.ds(j * blk, blk)
        g = g_ref[sl, :]
        ls = _log_sigmoid(g)
        prefix = functools.reduce(jnp.add, [_dot(tril, piece) for piece in _split3(ls)])
        suffix = prefix[blk - 1:blk, :] - prefix + ls
        out = jnp.where(lane < 2 * ML_HEADS, g, jnp.where(lane < 3 * ML_HEADS, prefix, suffix))
        o_ref[sl, :] = out
        placed = functools.reduce(jnp.add, [_dot(piece, place) for piece, place in zip(_split3(out), places)])
        o3_ref[sl, :] = placed.astype(BF16)


def _mlstm_gates(mg):
    rows = mg.shape[0]
    step = math.gcd(rows, 8 * ML_BLOCK)
    return pl.pallas_call(
        _mlstm_gate_kernel,
        grid=(rows // step,),
        in_specs=[pl.BlockSpec((step, N_GATES), lambda i: (i, 0))],
        out_specs=[pl.BlockSpec((step, N_GATES), lambda i: (i, 0)),
                   pl.BlockSpec((step, LANES), lambda i: (i, 0))],
        out_shape=[jax.ShapeDtypeStruct((rows, N_GATES), F32),
                   jax.ShapeDtypeStruct((rows, LANES), BF16)],
        compiler_params=_params(1),
        name="mlstm_gates",
    )(mg)


N_ROWS_PAD = BF16_SUBLANES
ST_ROWS = 2 * N_ROWS_PAD + 2 * ML_DV


def _mlstm_scan_kernel(ql_ref, qc_ref, kl_ref, kc_ref, ktl_ref, ktc_ref, vtl_ref, vtc_ref,
                       g3l_ref, g3c_ref, grl_ref, grc_ref, *rest, need_ctx):
    if need_ctx:
        hl_ref, hc_ref, st_in, m_in = rest
    else:
        hl_ref, st_in, m_in = rest
        hc_ref = None
    blk = ML_BLOCK
    n_lat_blk = ql_ref.shape[0] // blk
    n_ctx_blk = qc_ref.shape[0] // blk
    head = pl.program_id(1)

    s_idx = lax.broadcasted_iota(jnp.int32, (blk, blk), 0)
    t_idx = lax.broadcasted_iota(jnp.int32, (blk, blk), 1)
    lane_t = lax.broadcasted_iota(jnp.int32, (1, blk), 1)
    ones_n = jnp.ones((N_ROWS_PAD, blk), BF16)
    sel_src = lax.broadcasted_iota(jnp.int32, (LANES, 2 * LANES), 0)
    sel_dst = lax.broadcasted_iota(jnp.int32, (LANES, 2 * LANES), 1)
    sel_bwd = sel_dst >= LANES
    col = sel_src % N_GATES
    in_terms = sel_src < 3 * N_GATES
    plus = jnp.logical_and(col == head + jnp.where(sel_bwd, ML_HEADS, 0), in_terms)
    minus = jnp.logical_and(col == head + jnp.where(sel_bwd, 3 * ML_HEADS, 2 * ML_HEADS), in_terms)
    sel = (plus.astype(F32) - minus.astype(F32)).astype(BF16)

    def gate_rows(gr_ref, j, bwd):
        i_idx = head + (ML_HEADS if bwd else 0)
        f_idx = head + (3 * ML_HEADS if bwd else 2 * ML_HEADS)
        return gr_ref[j, pl.ds(f_idx, 1), :], gr_ref[j, pl.ds(i_idx, 1), :]

    def scan_states():
        ctx_blocks = [(grc_ref, ktc_ref, vtc_ref, j, j) for j in range(n_ctx_blk)]
        lat_blocks = [(grl_ref, ktl_ref, vtl_ref, j, n_ctx_blk + j) for j in range(n_lat_blk)]
        orders = (ctx_blocks + lat_blocks, ctx_blocks[::-1] + lat_blocks[::-1])
        m_prev = [jnp.zeros((1, 1), F32)] * 2
        ct_run = [jnp.zeros((ML_DV, ML_DQK), F32)] * 2
        n_run = [jnp.zeros((N_ROWS_PAD, ML_DQK), F32)] * 2
        for step in range(len(orders[0])):
            for d, bwd in enumerate((False, True)):
                gr_ref, kt_ref, vt_ref, j, slot = orders[d][step]
                c_row0 = 2 * N_ROWS_PAD + d * ML_DV
                f_row, i_row = gate_rows(gr_ref, j, bwd)
                b_end = jnp.sum(jnp.where(lane_t == (0 if bwd else blk - 1), f_row, 0.0), axis=1,
                                keepdims=True)
                g_log = b_end + (i_row - f_row)
                m_new = jnp.maximum(b_end + m_prev[d], jnp.max(g_log, axis=1, keepdims=True))
                a_prev = jnp.exp(b_end + m_prev[d] - m_new)
                kw = (kt_ref[j].astype(F32) * jnp.exp(g_log - m_new)).astype(BF16)
                st_in[slot, d * N_ROWS_PAD:(d + 1) * N_ROWS_PAD, :] = n_run[d].astype(BF16)
                st_in[slot, c_row0:c_row0 + ML_DV, :] = ct_run[d].astype(BF16)
                m_in[slot, :, d * LANES:(d + 1) * LANES] = jnp.broadcast_to(m_prev[d], (8, LANES))
                ct_run[d] = a_prev * ct_run[d] + _dot_nt(vt_ref[j], kw)
                n_run[d] = a_prev * n_run[d] + _dot_nt(ones_n, kw)
                m_prev[d] = m_new

    def outputs(q, k, vt, g3, rows, slot):
        qk_t = _dot_nt(k, q)
        state = _dot_nt(st_in[slot], q)
        r_both = _dot(g3, sel)
        m_prev_both = m_in[slot][0:1, :]
        scaled, inter = [], []
        for d, bwd in enumerate((False, True)):
            f_row, _ = rows[d]
            r_rep = r_both[:, d * LANES:(d + 1) * LANES]
            mask = (s_idx >= t_idx) if bwd else (s_idx <= t_idx)
            d_log = jnp.where(mask, _wide(r_rep, blk) + f_row, NEG_BIG)
            m_loc = jnp.max(d_log, axis=0, keepdims=True)
            s = qk_t * jnp.exp(d_log - m_loc)
            den_loc = jnp.sum(s, axis=0, keepdims=True)
            m_inter = f_row + m_prev_both[:, d * LANES:d * LANES + 1]
            m_t = jnp.maximum(m_inter, m_loc)
            w_inter = jnp.exp(m_inter - m_t)
            w_loc = jnp.exp(m_loc - m_t)
            qn = state[d * N_ROWS_PAD:d * N_ROWS_PAD + 1, :]
            den = w_inter * qn + w_loc * den_loc
            inv = 1.0 / jnp.maximum(jnp.abs(den), jnp.exp(-m_t))
            scaled.append(s * (w_loc * inv))
            c_row0 = 2 * N_ROWS_PAD + d * ML_DV
            inter.append(state[c_row0:c_row0 + ML_DV, :] * (w_inter * inv))
        return _dot(vt, (scaled[0] + scaled[1]).astype(BF16)) + inter[0] + inter[1]

    def lat_rows(j):
        return pl.ds(pl.multiple_of(j * blk, blk), blk)

    scan_states()

    if need_ctx:
        for j in range(n_ctx_blk):
            sl = pl.ds(j * blk, blk)
            rows = (gate_rows(grc_ref, j, False), gate_rows(grc_ref, j, True))
            out_t = outputs(qc_ref[sl, :], kc_ref[sl, :], vtc_ref[j], g3c_ref[sl, :], rows, j)
            hc_ref[sl, :] = out_t.T.astype(hc_ref.dtype)

    def out_body(j, carry):
        sl = lat_rows(j)
        rows = (gate_rows(grl_ref, j, False), gate_rows(grl_ref, j, True))
        out_t = outputs(ql_ref[sl, :], kl_ref[sl, :], vtl_ref[j], g3l_ref[sl, :], rows, n_ctx_blk + j)
        hl_ref[sl, :] = out_t.T.astype(hl_ref.dtype)
        return carry

    lax.fori_loop(0, n_lat_blk, out_body, 0, unroll=True)


def _mlstm_scan(mq, mk, mv, mg, *, seq, ctx_len, batch, n_lat, need_ctx):
    blk = ML_BLOCK
    rows = mq.shape[0]
    ctx0 = n_lat // ctx_len
    n_blk = (seq + ctx_len) // blk
    gates, gates3 = _mlstm_gates(mg)
    gates_rows = gates.T.reshape(N_GATES, rows // blk, blk).transpose(1, 0, 2)
    mk_t = mk.reshape(rows // blk, blk, ML_QK_W).transpose(0, 2, 1)
    mv_t = mv.reshape(rows // blk, blk, ML_V_W).transpose(0, 2, 1)

    def lat(width):
        return pl.BlockSpec((seq, width), lambda b, h: (b, h))

    def ctx(width):
        return pl.BlockSpec((ctx_len, width), lambda b, h: (ctx0 + b, h))

    def lat_t(width):
        return pl.BlockSpec((seq // blk, width, blk), lambda b, h: (b, h, 0))

    def ctx_t(width):
        return pl.BlockSpec((ctx_len // blk, width, blk), lambda b, h: (ctx0 + b, h, 0))

    in_specs = [lat(ML_DQK), ctx(ML_DQK), lat(ML_DQK), ctx(ML_DQK),
                lat_t(ML_DQK), ctx_t(ML_DQK), lat_t(ML_DV), ctx_t(ML_DV),
                pl.BlockSpec((seq, LANES), lambda b, h: (b, 0)),
                pl.BlockSpec((ctx_len, LANES), lambda b, h: (ctx0 + b, 0)),
                pl.BlockSpec((seq // blk, N_GATES, blk), lambda b, h: (b, 0, 0)),
                pl.BlockSpec((ctx_len // blk, N_GATES, blk), lambda b, h: (ctx0 + b, 0, 0))]
    out_specs = [pl.BlockSpec((seq, ML_DV), lambda b, h: (b, h))]
    out_shape = [jax.ShapeDtypeStruct((n_lat, ML_V_W), BF16)]
    if need_ctx:
        out_specs.append(pl.BlockSpec((ctx_len, ML_DV), lambda b, h: (b, h)))
        out_shape.append(jax.ShapeDtypeStruct((batch * ctx_len, ML_V_W), BF16))
    scratch = [pltpu.VMEM((n_blk, ST_ROWS, ML_DQK), BF16),
               pltpu.VMEM((n_blk, 8, 2 * LANES), F32)]
    out = pl.pallas_call(
        functools.partial(_mlstm_scan_kernel, need_ctx=need_ctx),
        grid=(batch, ML_HEADS),
        in_specs=in_specs,
        out_specs=out_specs,
        out_shape=out_shape,
        scratch_shapes=scratch,
        compiler_params=_params(2),
        name="mlstm",
    )(mq, mq, mk, mk, mk_t, mk_t, mv_t, mv_t, gates3, gates3, gates_rows, gates_rows)
    return out if need_ctx else (out[0], None)


def _diffattn_kernel(lam_ref, g_ref, q_ref, *refs, n_seg, lam_init):
    k_refs = refs[:n_seg]
    v_refs = refs[n_seg:2 * n_seg]
    o_ref = refs[2 * n_seg]
    lv = lam_ref[...]
    lam = (jnp.exp(jnp.sum(lv[0:1] * lv[1:2], axis=1, keepdims=True))
           - jnp.exp(jnp.sum(lv[2:3] * lv[3:4], axis=1, keepdims=True)) + lam_init)
    for hh in range(q_ref.shape[1] // LANES):
        q = q_ref[:, hh * LANES:(hh + 1) * LANES]
        ks = [k_ref[:, hh * LANES:(hh + 1) * LANES] for k_ref in k_refs]
        vs = [v_ref[:, 2 * hh * DA_DV:2 * (hh + 1) * DA_DV] for v_ref in v_refs]
        lane = lax.broadcasted_iota(jnp.int32, q.shape, 1)
        zero = jnp.zeros_like(q)
        halves = [jnp.where(lane < DA_HALF, q, zero), jnp.where(lane >= DA_HALF, q, zero)]
        scores = [[_dot_nt(qh, k) for k in ks] for qh in halves]
        maxes = [functools.reduce(jnp.maximum, [jnp.max(s, axis=1, keepdims=True) for s in ss])
                 for ss in scores]
        probs = [[jnp.exp2((s - m).astype(BF16)) for s in ss] for ss, m in zip(scores, maxes)]
        heads = []
        for ps in probs:
            acc = None
            for p, v in zip(ps, vs):
                part = _dot(p, v)
                acc = part if acc is None else acc + part
            heads.append(acc[:, :DA_DV] * (1.0 / acc[:, DA_DV:]))
        o = heads[0] - lam * heads[1]
        ms = jnp.mean(o * o, axis=1, keepdims=True)
        o_ref[:, hh * DA_DV:(hh + 1) * DA_DV] = (o * lax.rsqrt(ms + LN_EPS) * g_ref[...]
                                                 * (1.0 - lam_init)).astype(o_ref.dtype)


def _diffattn(dq, dk, dv, lam_vecs, da_g, *, q_row0, q_len, segments, batch, lam_init, tq=1024,
              heads_per_step=1):
    tq = min(tq, q_len)
    nq = q_len // tq
    q0 = q_row0 // tq
    hps = heads_per_step

    def seg_spec(row0, length, width):
        return pl.BlockSpec((length, hps * width), lambda b, h, i: (row0 // length + b, h))

    return pl.pallas_call(
        functools.partial(_diffattn_kernel, n_seg=len(segments), lam_init=lam_init),
        grid=(batch, DA_HEADS // hps, nq),
        in_specs=[pl.BlockSpec(lam_vecs.shape, lambda b, h, i: (0, 0)),
                  pl.BlockSpec((1, DA_DV), lambda b, h, i: (0, 0)),
                  pl.BlockSpec((tq, hps * LANES), lambda b, h, i: (q0 + b * nq + i, h))]
                 + [seg_spec(r0, ln, LANES) for r0, ln in segments]
                 + [seg_spec(r0, ln, 2 * DA_DV) for r0, ln in segments],
        out_specs=pl.BlockSpec((tq, hps * DA_DV), lambda b, h, i: (b * nq + i, h)),
        out_shape=jax.ShapeDtypeStruct((batch * q_len, DA_V_W), BF16),
        compiler_params=_params(3),
        name="diffattn",
    )(lam_vecs, da_g.reshape(1, DA_DV), dq, *([dk] * len(segments)), *([dv] * len(segments)))


def _mixer_out_kernel(h_ref, mod_ref, sb_ref, p_ref, pprev_ref, pnext_ref, smo_ref, *refs,
                      seq, ctx_len, n_lat, alpha, two_source):
    tm = h_ref.shape[0]
    r0 = pl.program_id(0) * tm
    is_lat = r0 < n_lat
    if two_source:
        hml_l, hml_c, yda_l, yda_c = refs[:4]
        refs = refs[4:]
        hml = jnp.where(is_lat, hml_l[...], hml_c[...])
        yda = jnp.where(is_lat, yda_l[...], yda_c[...])
    else:
        hml, yda = refs[0][...], refs[1][...]
        refs = refs[2:]
    gs_ref, gm_ref, gd_ref, convw_ref, wsc_ref, wml_ref, wda_ref, wo_ref, lng_ref, lnb_ref, o_ref = refs
    row = lax.broadcasted_iota(jnp.int32, (tm, 1), 0)
    pos = jnp.where(is_lat, (r0 + row) % seq, (r0 - n_lat + row) % ctx_len)
    seq_len = jnp.where(is_lat, seq, ctx_len)
    first = pos == 0
    last = pos == seq_len - 1

    p = p_ref[...].astype(F32)
    prev_row = pprev_ref[...].astype(F32)[BF16_SUBLANES - 1:BF16_SUBLANES, :]
    next_row = pnext_ref[...].astype(F32)[0:1, :]
    p_before = jnp.where(first, 0.0, jnp.where(row == 0, prev_row, pltpu.roll(p, 1, 0)))
    p_after = jnp.where(last, 0.0, jnp.where(row == tm - 1, next_row, pltpu.roll(p, tm - 1, 0)))
    cw = convw_ref[...]
    conv = cw[0:1, :] * p_before + cw[1:2, :] * p + cw[2:3, :] * p_after
    y_sc = (sb_ref[...].astype(F32) * conv).astype(BF16)
    y_ml = smo_ref[...] * hml
    y = (gs_ref[...].astype(F32) * _dot(y_sc, wsc_ref[...].astype(BF16))
         + gm_ref[...].astype(F32) * _dot(y_ml, wml_ref[...].astype(BF16))
         + gd_ref[...].astype(F32) * _dot(yda, wda_ref[...].astype(BF16)))
    y = _dot(y.astype(BF16), wo_ref[...].astype(BF16))
    h = h_ref[...]
    o_ref[...] = _layer_norm(alpha * h + mod_ref[0, 5:6, :] * y, lng_ref[...], lnb_ref[...])


def _mixer_out(h, mods3, sb, p, smo, hml, yda, gs, gm, gd, conv_w, w_sc, w_ml, w_da, w_o, ln_g, ln_b,
               *, layer, n_rows, seq, ctx_len, batch, n_lat, alpha, hml_ctx=None, yda_ctx=None, tm=512):
    d = h.shape[1]
    tm = math.gcd(math.gcd(tm, seq), batch * ctx_len)
    halo = BF16_SUBLANES
    last_halo = p.shape[0] // halo - 1
    two_source = hml_ctx is not None

    def row_spec(width):
        return pl.BlockSpec((tm, width), lambda i: (i, 0))

    if two_source:
        branch = [hml, hml_ctx, yda, yda_ctx]
        branch_specs = (_two_source_specs(tm, ML_V_W, n_lat // tm) + _two_source_specs(tm, DA_V_W, n_lat // tm))
    else:
        branch = [hml, yda]
        branch_specs = [row_spec(ML_V_W), row_spec(DA_V_W)]
    kernel = functools.partial(_mixer_out_kernel, seq=seq, ctx_len=ctx_len, n_lat=n_lat, alpha=alpha,
                               two_source=two_source)
    return pl.pallas_call(
        kernel,
        grid=(n_rows // tm,),
        in_specs=[row_spec(d),
                  pl.BlockSpec((1, N_MOD, d), lambda i: (jnp.minimum(i * tm // seq, batch), 0, 0)),
                  row_spec(SC_WIDTH), row_spec(SC_WIDTH),
                  pl.BlockSpec((halo, SC_WIDTH), lambda i: (jnp.maximum(i * (tm // halo) - 1, 0), 0)),
                  pl.BlockSpec((halo, SC_WIDTH),
                               lambda i: (jnp.minimum((i + 1) * (tm // halo), last_halo), 0)),
                  row_spec(ML_V_W)] + branch_specs + [
                  row_spec(d), row_spec(d), row_spec(d),
                  _resident((SC_KSIZE, SC_WIDTH)),
                  _resident_layer(w_sc, layer), _resident_layer(w_ml, layer), _resident_layer(w_da, layer),
                  _resident_layer(w_o, layer),
                  pl.BlockSpec((1, d), lambda i: (0, 0)),
                  pl.BlockSpec((1, d), lambda i: (0, 0))],
        out_specs=row_spec(d),
        out_shape=jax.ShapeDtypeStruct((n_rows, d), F32),
        compiler_params=_params(1),
        name="mixer_out",
    )(h, mods3, sb, p, p, p, smo, *branch, gs, gm, gd, conv_w, w_sc, w_ml, w_da, w_o,
      ln_g.reshape(1, d), ln_b.reshape(1, d))


def _rope_tables(seq, tile):
    n_freq = DA_HALF // 4
    t = jnp.arange(seq)
    row_ids = (t // GRID_W).astype(F32)
    col_ids = (t % GRID_W).astype(F32)
    inv = ROPE_BASE ** (-jnp.arange(n_freq, dtype=F32) / n_freq)
    ang = jnp.concatenate([row_ids[:, None] * inv, col_ids[:, None] * inv], axis=-1)
    cos, sin = jnp.cos(ang), jnp.sin(ang)
    reps = LANES // DA_HALF
    cos_t = jnp.tile(jnp.concatenate([cos, cos], axis=-1), (1, reps))
    sin_t = jnp.tile(jnp.concatenate([-sin, sin], axis=-1), (1, reps))
    cos_t = jnp.concatenate([cos_t, jnp.ones((tile, LANES), F32)], axis=0)
    sin_t = jnp.concatenate([sin_t, jnp.zeros((tile, LANES), F32)], axis=0)
    return cos_t, sin_t


def kernel(x, c, ctx, c_ctx, w_ada, b_ada, ln_g, ln_b, ffn1_up, ffn1_down, ffn2_up, ffn2_down, w_in, b_in,
           conv_w, w_sc, w_ml, w_da, w_o, lam_q1, lam_k1, lam_q2, lam_k2, da_norm_g):
    batch, seq, d = x.shape
    ctx_len = ctx.shape[1]
    depth = w_ada.shape[0]
    alpha = (2 * depth) ** 0.25
    n_lat = batch * seq
    n_ctx = batch * ctx_len
    mixer_in_tile = min(256, seq)

    n_cond = -(-(batch + 1) // 8) * 8
    cc = jnp.concatenate([c, c_ctx[None, :], jnp.zeros((n_cond - batch - 1, d), F32)], axis=0)
    mods = _ada(cc, w_ada, b_ada).reshape(depth, n_cond, N_MOD, d)

    cos_t, sin_t = _rope_tables(seq, mixer_in_tile)
    gate_lo = GATE_COL0
    gate_hi = gate_lo + N_GATES
    w_in_t = jnp.swapaxes(w_in, 1, 2)

    h, h_ctx = x.reshape(n_lat, d), ctx.reshape(n_ctx, d)
    for l in range(depth):
        last = l == depth - 1
        lam_init = 0.8 - 0.6 * math.exp(-0.3 * l)
        mods3 = mods[l]
        b_head = b_in[l, :gate_lo][None, :]
        b_tail = b_in[l, gate_hi:][None, :]
        b_gate = b_in[l, gate_lo:gate_hi][None, :]
        lam_vecs = jnp.stack([lam_q1[l], lam_k1[l], lam_q2[l], lam_k2[l]]).astype(F32)

        h, w_proj_t = _ffn(h, mods3, ffn1_up, ffn1_down, ln_g[l, 0], ln_b[l, 0], layer=l, mod_base=0,
                           n_rows=n_lat + n_ctx, seq=seq, batch=batch, alpha=alpha, h_ctx=h_ctx,
                           cast_stack=w_in_t)
        h_ctx = None

        (sb, p, mq, mk, mv, smo, mg, dq, dk, dv, gs, gm, gd) = _mixer_in(
            h, mods3, w_proj_t, b_head, b_tail, b_gate, cos_t, sin_t,
            seq=seq, batch=batch, n_lat=n_lat, tm=mixer_in_tile)

        hml, hml_ctx = _mlstm_scan(mq, mk, mv, mg, seq=seq, ctx_len=ctx_len, batch=batch, n_lat=n_lat,
                                   need_ctx=not last)
        yda = _diffattn(dq, dk, dv, lam_vecs, da_norm_g[l], q_row0=0, q_len=seq,
                        segments=((0, seq), (n_lat, ctx_len)), batch=batch, lam_init=lam_init)
        yda_ctx = None
        if not last:
            yda_ctx = _diffattn(dq, dk, dv, lam_vecs, da_norm_g[l], q_row0=n_lat, q_len=ctx_len,
                                segments=((n_lat, ctx_len),), batch=batch, lam_init=lam_init,
                                heads_per_step=DA_HEADS)

        n_rows = n_lat if last else n_lat + n_ctx
        h = _mixer_out(h, mods3, sb, p, smo, hml, yda, gs, gm, gd, conv_w[l], w_sc, w_ml, w_da, w_o,
                       ln_g[l, 1], ln_b[l, 1], hml_ctx=hml_ctx, yda_ctx=yda_ctx, layer=l,
                       n_rows=n_rows, seq=seq, ctx_len=ctx_len, batch=batch, n_lat=n_lat, alpha=alpha)
        h = _ffn(h, mods3, ffn2_up, ffn2_down, ln_g[l, 2], ln_b[l, 2], layer=l,
                 mod_base=6, n_rows=n_rows, seq=seq, batch=batch, alpha=alpha)
    return h[:n_lat].reshape(batch, seq, d)
```

```python
import functools
import math

import jax
import jax.numpy as jnp
from jax import lax
from jax.experimental import pallas as pl
from jax.experimental.pallas import tpu as pltpu

GRID_W = 64
N_MOD = 9
SC_WIDTH = 512
SC_KSIZE = 3
ML_HEADS = 4
ML_DQK = 128
ML_DV = 256
DA_HEADS = 4
DA_HALF = 64
DA_DV = 2 * DA_HALF
ROPE_BASE = 10000.0
LN_EPS = 1e-5

ML_QK_W = ML_HEADS * ML_DQK
ML_V_W = ML_HEADS * ML_DV
DA_QK_W = DA_HEADS * 2 * DA_HALF
DA_V_W = DA_HEADS * DA_DV
N_GATES = 4 * ML_HEADS

ML_BLOCK = 256
FFN_COLS = 256
LANES = 128
BF16_SUBLANES = 16
V7X_VMEM_BYTES = 64 * 1024 * 1024
VMEM_LIMIT = V7X_VMEM_BYTES - 8 * 1024 * 1024

F32 = jnp.float32
BF16 = jnp.bfloat16
NEG_BIG = -1e30
LOG2_E = 1.4426950408889634

assert ML_DQK == LANES and DA_DV == LANES and ML_BLOCK == 2 * LANES


def _dot(a, b, precision=None):
    return jnp.dot(a, b, preferred_element_type=F32, precision=precision)


def _dot_nt(a, b):
    return lax.dot_general(a, b, (((1,), (1,)), ((), ())), preferred_element_type=F32)


def _wide(x, width):
    return jnp.concatenate([x] * (width // LANES), axis=1)


def _layer_norm(y, g, b):
    mu = jnp.mean(y, axis=-1, keepdims=True)
    yc = y - mu
    var = jnp.mean(yc * yc, axis=-1, keepdims=True)
    return yc * lax.rsqrt(var + LN_EPS) * g + b


def _resident(shape):
    return pl.BlockSpec(shape, lambda *_: (0,) * len(shape), pipeline_mode=pl.Buffered(1))


def _resident_layer(stacked, layer):
    n = stacked.ndim - 1
    return pl.BlockSpec((None,) + stacked.shape[1:], lambda *_: (layer,) + (0,) * n,
                        pipeline_mode=pl.Buffered(1))


def _params(n_axes):
    return pltpu.CompilerParams(dimension_semantics=("parallel",) * n_axes,
                                vmem_limit_bytes=VMEM_LIMIT)


def _ada_kernel(c_ref, w_ref, b_ref, o_ref):
    c = c_ref[...]
    a = (c * jax.nn.sigmoid(c)).astype(BF16)
    o_ref[0] = _dot(a, w_ref[0].astype(BF16)) + b_ref[0]


def _ada(cc, w_ada, b_ada):
    depth, d, n = w_ada.shape
    tn = d
    return pl.pallas_call(
        _ada_kernel,
        grid=(depth, n // tn),
        in_specs=[pl.BlockSpec(cc.shape, lambda l, j: (0, 0)),
                  pl.BlockSpec((1, d, tn), lambda l, j: (l, 0, j)),
                  pl.BlockSpec((1, 1, tn), lambda l, j: (l, 0, j))],
        out_specs=pl.BlockSpec((1, cc.shape[0], tn), lambda l, j: (l, 0, j)),
        out_shape=jax.ShapeDtypeStruct((depth, cc.shape[0], n), F32),
        compiler_params=_params(2),
        name="ada",
    )(cc, w_ada, b_ada.reshape(depth, 1, n))


def _two_source_specs(tm, width, lat_tiles):
    return [pl.BlockSpec((tm, width), lambda i: (jnp.minimum(i, lat_tiles - 1), 0)),
            pl.BlockSpec((tm, width), lambda i: (jnp.maximum(i - lat_tiles, 0), 0))]


def _ffn_kernel(*refs, mod_base, alpha, lat_tiles, cast_extra):
    if lat_tiles is None:
        h_ref, hc_ref = refs[0], None
        refs = refs[1:]
    else:
        h_ref, hc_ref = refs[:2]
        refs = refs[2:]
    if cast_extra:
        mod_ref, wup_ref, wdn_ref, lng_ref, lnb_ref, extra_ref, o_ref, extra_o, g_scr = refs
        extra_o[...] = extra_ref[...].astype(BF16)
    else:
        mod_ref, wup_ref, wdn_ref, lng_ref, lnb_ref, o_ref, g_scr = refs
    h = h_ref[...] if hc_ref is None else jnp.where(pl.program_id(0) < lat_tiles, h_ref[...], hc_ref[...])
    f = wdn_ref.shape[0]
    shift = mod_ref[0, mod_base:mod_base + 1, :]
    scale = mod_ref[0, mod_base + 1:mod_base + 2, :]
    gate = mod_ref[0, mod_base + 2:mod_base + 3, :]
    u = (h * (1.0 + scale) + shift).astype(BF16)
    for c in range(f // FFN_COLS):
        lo, hi = c * FFN_COLS, (c + 1) * FFN_COLS
        a = _dot(u, wup_ref[:, lo:hi].astype(BF16))
        v = _dot(u, wup_ref[:, f + lo:f + hi].astype(BF16))
        g_scr[:, lo:hi] = (a * jax.nn.sigmoid(a) * v).astype(BF16)
    d = _dot(g_scr[...], wdn_ref[...].astype(BF16))
    o_ref[...] = _layer_norm(alpha * h + (0.5 * gate) * d, lng_ref[...], lnb_ref[...])


def _ffn(h, mods3, w_up, w_dn, ln_g, ln_b, *, layer, mod_base, n_rows, seq, batch, alpha, h_ctx=None,
         cast_stack=None, tm=512):
    d = h.shape[1]
    f = w_dn.shape[1]
    tm = min(tm, seq)
    steps = n_rows // tm
    lat_tiles = None if h_ctx is None else h.shape[0] // tm
    kernel = functools.partial(_ffn_kernel, mod_base=mod_base, alpha=alpha, lat_tiles=lat_tiles,
                               cast_extra=cast_stack is not None)
    if h_ctx is None:
        sources, source_specs = [h], [pl.BlockSpec((tm, d), lambda i: (i, 0))]
    else:
        sources, source_specs = [h, h_ctx], _two_source_specs(tm, d, lat_tiles)
    extra, extra_specs = [], []
    out_specs = [pl.BlockSpec((tm, d), lambda i: (i, 0))]
    out_shape = [jax.ShapeDtypeStruct((n_rows, d), F32)]
    if cast_stack is not None:
        n_extra = cast_stack.shape[1]
        rows_blk = -(-pl.cdiv(n_extra, steps) // BF16_SUBLANES) * BF16_SUBLANES
        last_blk = pl.cdiv(n_extra, rows_blk) - 1
        extra = [cast_stack]
        extra_specs = [pl.BlockSpec((None, rows_blk, d), lambda i: (layer, jnp.minimum(i, last_blk), 0))]
        out_specs.append(pl.BlockSpec((rows_blk, d), lambda i: (jnp.minimum(i, last_blk), 0)))
        out_shape.append(jax.ShapeDtypeStruct((n_extra, d), BF16))
    out = pl.pallas_call(
        kernel,
        grid=(steps,),
        in_specs=source_specs + [
                  pl.BlockSpec((1, N_MOD, d), lambda i: (jnp.minimum(i * tm // seq, batch), 0, 0)),
                  _resident_layer(w_up, layer),
                  _resident_layer(w_dn, layer),
                  pl.BlockSpec((1, d), lambda i: (0, 0)),
                  pl.BlockSpec((1, d), lambda i: (0, 0))] + extra_specs,
        out_specs=out_specs,
        out_shape=out_shape,
        scratch_shapes=[pltpu.VMEM((tm, f), BF16)],
        compiler_params=pltpu.CompilerParams(
            dimension_semantics=("arbitrary" if cast_stack is not None else "parallel",),
            vmem_limit_bytes=VMEM_LIMIT),
        name="ffn",
    )(*sources, mods3, w_up, w_dn, ln_g.reshape(1, d), ln_b.reshape(1, d), *extra)
    return out if cast_stack is not None else out[0]


GATE_COL0 = 3 * SC_WIDTH + 2 * ML_QK_W + 2 * ML_V_W


def _proj_cols(d):
    parts = ((("sb", SC_WIDTH), ("sc", SC_WIDTH), ("sx", SC_WIDTH),
              ("mq", ML_QK_W), ("mk", ML_QK_W), ("mv", ML_V_W), ("mo", ML_V_W)),
             (("dq", DA_QK_W), ("dk", DA_QK_W), ("dv", DA_V_W), ("gs", d), ("gm", d), ("gd", d)))
    cols, widths = {}, []
    for part, sizes in enumerate(parts):
        off = 0
        for name, w in sizes:
            cols[name] = (part, off, off + w)
            off += w
        widths.append(off)
    return cols, widths


def _mixer_in_kernel(h_ref, mod_ref, w_ref, bh_ref, bt_ref, bg_ref, cos_ref, sin_ref,
                     sb_o, p_o, mq_o, mk_o, mv_o, smo_o, mg_o, dq_o, dk_o, dv_o, gs_o, gm_o, gd_o):
    d = h_ref.shape[1]
    tm = h_ref.shape[0]
    cols, _ = _proj_cols(d)
    h = h_ref[...]
    shift = mod_ref[0, 3:4, :]
    scale = mod_ref[0, 4:5, :]
    u = (h * (1.0 + scale) + shift).astype(BF16)

    def proj(name):
        part, lo, hi = cols[name]
        b_ref = (bh_ref, bt_ref)[part]
        row0 = 0 if part == 0 else GATE_COL0 + N_GATES
        return _dot_nt(u, w_ref[row0 + lo:row0 + hi, :]) + b_ref[:, lo:hi]

    sb_o[...] = proj("sb").astype(BF16)
    p_o[...] = (proj("sc") * proj("sx")).astype(BF16)
    mq_o[...] = proj("mq").astype(BF16)
    mk_o[...] = (proj("mk") * (ML_DQK ** -0.5)).astype(BF16)
    mv_o[...] = proj("mv").astype(BF16)
    smo_o[...] = jax.nn.sigmoid(proj("mo")).astype(BF16)
    mg_o[...] = _dot_nt(u, w_ref[GATE_COL0:GATE_COL0 + N_GATES, :]) + bg_ref[...]

    cos_t = cos_ref[...]
    sin_t = sin_ref[...]
    lane = lax.broadcasted_iota(jnp.int32, cos_t.shape, 1)
    first_half = (lane % DA_HALF) < (DA_HALF // 2)

    def rope_store(z, out_ref, mult):
        for k in range(z.shape[1] // LANES):
            x = z[:, k * LANES:(k + 1) * LANES]
            partner = jnp.where(first_half,
                                pltpu.roll(x, LANES - DA_HALF // 2, 1),
                                pltpu.roll(x, DA_HALF // 2, 1))
            out_ref[:, k * LANES:(k + 1) * LANES] = ((x * cos_t + partner * sin_t) * mult).astype(BF16)

    rope_store(proj("dq"), dq_o, (DA_HALF ** -0.5) * LOG2_E)
    rope_store(proj("dk"), dk_o, 1.0)
    dv = proj("dv").astype(BF16)
    ones = jnp.ones((tm, DA_DV), BF16)
    for k in range(DA_HEADS):
        dv_o[:, 2 * k * DA_DV:(2 * k + 1) * DA_DV] = dv[:, k * DA_DV:(k + 1) * DA_DV]
        dv_o[:, (2 * k + 1) * DA_DV:(2 * k + 2) * DA_DV] = ones
    gs_o[...] = jax.nn.sigmoid(proj("gs")).astype(BF16)
    gm_o[...] = jax.nn.sigmoid(proj("gm")).astype(BF16)
    gd_o[...] = jax.nn.sigmoid(proj("gd")).astype(BF16)


def _mixer_in(h, mods3, w_t, b_head, b_tail, b_gate, cos_t, sin_t, *, seq, batch, n_lat, tm=256):
    rows, d = h.shape
    _, (n_head, n_tail) = _proj_cols(d)
    tm = min(tm, seq)
    lat_tiles = n_lat // tm
    rope_blocks = seq // tm

    def row_spec(width):
        return pl.BlockSpec((tm, width), lambda i: (i, 0))

    def rope_map(i):
        return (jnp.where(i < lat_tiles, i % rope_blocks, rope_blocks), 0)

    widths = (SC_WIDTH, SC_WIDTH, ML_QK_W, ML_QK_W, ML_V_W, ML_V_W, N_GATES,
              DA_QK_W, DA_QK_W, 2 * DA_V_W, d, d, d)
    dtypes = (BF16,) * 6 + (F32,) + (BF16,) * 6
    return pl.pallas_call(
        _mixer_in_kernel,
        grid=(rows // tm,),
        in_specs=[row_spec(d),
                  pl.BlockSpec((1, N_MOD, d), lambda i: (jnp.minimum(i * tm // seq, batch), 0, 0)),
                  _resident(w_t.shape),
                  _resident((1, n_head)),
                  _resident((1, n_tail)),
                  _resident((1, N_GATES)),
                  pl.BlockSpec((tm, LANES), rope_map),
                  pl.BlockSpec((tm, LANES), rope_map)],
        out_specs=[row_spec(w) for w in widths],
        out_shape=[jax.ShapeDtypeStruct((rows, w), dt) for w, dt in zip(widths, dtypes)],
        compiler_params=_params(1),
        name="mixer_in",
    )(h, mods3, w_t, b_head, b_tail, b_gate, cos_t, sin_t)


def _log_sigmoid(x):
    return jnp.minimum(x, 0.0) - jnp.log(1.0 + jnp.exp(-jnp.abs(x)))


def _split3(x):
    pieces, rest = [], x
    for _ in range(3):
        piece = rest.astype(BF16)
        pieces.append(piece)
        rest = rest - piece.astype(F32)
    return pieces


def _mlstm_gate_kernel(g_ref, o_ref, o3_ref):
    blk = ML_BLOCK
    t_idx = lax.broadcasted_iota(jnp.int32, (blk, blk), 0)
    s_idx = lax.broadcasted_iota(jnp.int32, (blk, blk), 1)
    tril = (s_idx <= t_idx).astype(BF16)
    lane = lax.broadcasted_iota(jnp.int32, (blk, N_GATES), 1)
    src = lax.broadcasted_iota(jnp.int32, (N_GATES, LANES), 0)
    dst = lax.broadcasted_iota(jnp.int32, (N_GATES, LANES), 1)
    places = [(dst == src + term * N_GATES).astype(BF16) for term in range(3)]
    for j in range(g_ref.shape[0] // blk):
        sl = pl.ds(j * blk, blk)
        g = g_ref[sl, :]
        ls = _log_sigmoid(g)
        prefix = functools.reduce(jnp.add, [_dot(tril, piece) for piece in _split3(ls)])
        suffix = prefix[blk - 1:blk, :] - prefix + ls
        out = jnp.where(lane < 2 * ML_HEADS, g, jnp.where(lane < 3 * ML_HEADS, prefix, suffix))
        o_ref[sl, :] = out
        placed = functools.reduce(jnp.add, [_dot(piece, place) for piece, place in zip(_split3(out), places)])
        o3_ref[sl, :] = placed.astype(BF16)


def _mlstm_gates(mg):
    rows = mg.shape[0]
    step = math.gcd(rows, 8 * ML_BLOCK)
    return pl.pallas_call(
        _mlstm_gate_kernel,
        grid=(rows // step,),
        in_specs=[pl.BlockSpec((step, N_GATES), lambda i: (i, 0))],
        out_specs=[pl.BlockSpec((step, N_GATES), lambda i: (i, 0)),
                   pl.BlockSpec((step, LANES), lambda i: (i, 0))],
        out_shape=[jax.ShapeDtypeStruct((rows, N_GATES), F32),
                   jax.ShapeDtypeStruct((rows, LANES), BF16)],
        compiler_params=_params(1),
        name="mlstm_gates",
    )(mg)


ML_HEADS_PER_STEP = 4
N_ROWS_PAD = BF16_SUBLANES
ST_ROWS = 2 * N_ROWS_PAD + 2 * ML_DV


def _mlstm_scan_kernel(*refs, need_ctx, heads_per_step):
    ins, rest = refs[:12], refs[12:]
    if need_ctx:
        hl_ref, hc_ref, st_in, m_in = rest
    else:
        hl_ref, st_in, m_in = rest
        hc_ref = None
    ql, qc, kl, kc, ktl, ktc, vtl, vtc, g3l, g3c, grl, grc = ins
    for hh in range(heads_per_step):
        qk_cols = pl.ds(hh * ML_DQK, ML_DQK)
        v_cols = pl.ds(hh * ML_DV, ML_DV)
        _mlstm_head(pl.program_id(1) * heads_per_step + hh,
                    ql.at[:, qk_cols], qc.at[:, qk_cols], kl.at[:, qk_cols], kc.at[:, qk_cols],
                    ktl.at[:, qk_cols, :], ktc.at[:, qk_cols, :], vtl.at[:, v_cols, :], vtc.at[:, v_cols, :],
                    g3l, g3c, grl, grc, hl_ref.at[:, v_cols],
                    None if hc_ref is None else hc_ref.at[:, v_cols], st_in.at[hh], m_in.at[hh])


def _mlstm_head(head, ql_ref, qc_ref, kl_ref, kc_ref, ktl_ref, ktc_ref, vtl_ref, vtc_ref,
                g3l_ref, g3c_ref, grl_ref, grc_ref, hl_ref, hc_ref, st_in, m_in):
    need_ctx = hc_ref is not None
    blk = ML_BLOCK
    n_lat_blk = ql_ref.shape[0] // blk
    n_ctx_blk = qc_ref.shape[0] // blk

    s_idx = lax.broadcasted_iota(jnp.int32, (blk, blk), 0)
    t_idx = lax.broadcasted_iota(jnp.int32, (blk, blk), 1)
    lane_t = lax.broadcasted_iota(jnp.int32, (1, blk), 1)
    ones_n = jnp.ones((N_ROWS_PAD, blk), BF16)
    sel_src = lax.broadcasted_iota(jnp.int32, (LANES, 2 * LANES), 0)
    sel_dst = lax.broadcasted_iota(jnp.int32, (LANES, 2 * LANES), 1)
    sel_bwd = sel_dst >= LANES
    col = sel_src % N_GATES
    in_terms = sel_src < 3 * N_GATES
    plus = jnp.logical_and(col == head + jnp.where(sel_bwd, ML_HEADS, 0), in_terms)
    minus = jnp.logical_and(col == head + jnp.where(sel_bwd, 3 * ML_HEADS, 2 * ML_HEADS), in_terms)
    sel = (plus.astype(F32) - minus.astype(F32)).astype(BF16)

    def gate_rows(gr_ref, j, bwd):
        i_idx = head + (ML_HEADS if bwd else 0)
        f_idx = head + (3 * ML_HEADS if bwd else 2 * ML_HEADS)
        return gr_ref[j, pl.ds(f_idx, 1), :], gr_ref[j, pl.ds(i_idx, 1), :]

    def scan_states():
        ctx_blocks = [(grc_ref, ktc_ref, vtc_ref, j, j) for j in range(n_ctx_blk)]
        lat_blocks = [(grl_ref, ktl_ref, vtl_ref, j, n_ctx_blk + j) for j in range(n_lat_blk)]
        orders = (ctx_blocks + lat_blocks, ctx_blocks[::-1] + lat_blocks[::-1])
        m_prev = [jnp.zeros((1, 1), F32)] * 2
        ct_run = [jnp.zeros((ML_DV, ML_DQK), F32)] * 2
        n_run = [jnp.zeros((N_ROWS_PAD, ML_DQK), F32)] * 2
        for step in range(len(orders[0])):
            for d, bwd in enumerate((False, True)):
                gr_ref, kt_ref, vt_ref, j, slot = orders[d][step]
                c_row0 = 2 * N_ROWS_PAD + d * ML_DV
                f_row, i_row = gate_rows(gr_ref, j, bwd)
                b_end = jnp.sum(jnp.where(lane_t == (0 if bwd else blk - 1), f_row, 0.0), axis=1,
                                keepdims=True)
                g_log = b_end + (i_row - f_row)
                m_new = jnp.maximum(b_end + m_prev[d], jnp.max(g_log, axis=1, keepdims=True))
                a_prev = jnp.exp(b_end + m_prev[d] - m_new)
                kw = (kt_ref[j].astype(F32) * jnp.exp(g_log - m_new)).astype(BF16)
                st_in[slot, d * N_ROWS_PAD:(d + 1) * N_ROWS_PAD, :] = n_run[d].astype(BF16)
                st_in[slot, c_row0:c_row0 + ML_DV, :] = ct_run[d].astype(BF16)
                m_in[slot, :, d * LANES:(d + 1) * LANES] = jnp.broadcast_to(m_prev[d], (8, LANES))
                ct_run[d] = a_prev * ct_run[d] + _dot_nt(vt_ref[j], kw)
                n_run[d] = a_prev * n_run[d] + _dot_nt(ones_n, kw)
                m_prev[d] = m_new

    def outputs(q, k, vt, g3, rows, slot):
        qk_t = _dot_nt(k, q)
        state = _dot_nt(st_in[slot], q)
        r_both = _dot(g3, sel)
        m_prev_both = m_in[slot][0:1, :]
        scaled, inter = [], []
        for d, bwd in enumerate((False, True)):
            f_row, _ = rows[d]
            r_rep = r_both[:, d * LANES:(d + 1) * LANES]
            mask = (s_idx >= t_idx) if bwd else (s_idx <= t_idx)
            d_log = jnp.where(mask, _wide(r_rep, blk) + f_row, NEG_BIG)
            m_loc = jnp.max(d_log, axis=0, keepdims=True)
            s = qk_t * jnp.exp(d_log - m_loc)
            den_loc = jnp.sum(s, axis=0, keepdims=True)
            m_inter = f_row + m_prev_both[:, d * LANES:d * LANES + 1]
            m_t = jnp.maximum(m_inter, m_loc)
            w_inter = jnp.exp(m_inter - m_t)
            w_loc = jnp.exp(m_loc - m_t)
            qn = state[d * N_ROWS_PAD:d * N_ROWS_PAD + 1, :]
            den = w_inter * qn + w_loc * den_loc
            inv = 1.0 / jnp.maximum(jnp.abs(den), jnp.exp(-m_t))
            scaled.append(s * (w_loc * inv))
            c_row0 = 2 * N_ROWS_PAD + d * ML_DV
            inter.append(state[c_row0:c_row0 + ML_DV, :] * (w_inter * inv))
        return _dot(vt, (scaled[0] + scaled[1]).astype(BF16)) + inter[0] + inter[1]

    def lat_rows(j):
        return pl.ds(pl.multiple_of(j * blk, blk), blk)

    scan_states()

    if need_ctx:
        for j in range(n_ctx_blk):
            sl = pl.ds(j * blk, blk)
            rows = (gate_rows(grc_ref, j, False), gate_rows(grc_ref, j, True))
            out_t = outputs(qc_ref[sl, :], kc_ref[sl, :], vtc_ref[j], g3c_ref[sl, :], rows, j)
            hc_ref[sl, :] = out_t.T.astype(hc_ref.dtype)

    def out_body(j, carry):
        sl = lat_rows(j)
        rows = (gate_rows(grl_ref, j, False), gate_rows(grl_ref, j, True))
        out_t = outputs(ql_ref[sl, :], kl_ref[sl, :], vtl_ref[j], g3l_ref[sl, :], rows, n_ctx_blk + j)
        hl_ref[sl, :] = out_t.T.astype(hl_ref.dtype)
        return carry

    lax.fori_loop(0, n_lat_blk, out_body, 0, unroll=True)


def _mlstm_scan(mq, mk, mv, mg, *, seq, ctx_len, batch, n_lat, need_ctx):
    blk = ML_BLOCK
    rows = mq.shape[0]
    ctx0 = n_lat // ctx_len
    n_blk = (seq + ctx_len) // blk
    gates, gates3 = _mlstm_gates(mg)
    gates_rows = gates.T.reshape(N_GATES, rows // blk, blk).transpose(1, 0, 2)
    mk_t = mk.reshape(rows // blk, blk, ML_QK_W).transpose(0, 2, 1)
    mv_t = mv.reshape(rows // blk, blk, ML_V_W).transpose(0, 2, 1)

    hps = ML_HEADS_PER_STEP

    def lat(width):
        return pl.BlockSpec((seq, hps * width), lambda b, h: (b, h))

    def ctx(width):
        return pl.BlockSpec((ctx_len, hps * width), lambda b, h: (ctx0 + b, h))

    def lat_t(width):
        return pl.BlockSpec((seq // blk, hps * width, blk), lambda b, h: (b, h, 0))

    def ctx_t(width):
        return pl.BlockSpec((ctx_len // blk, hps * width, blk), lambda b, h: (ctx0 + b, h, 0))

    in_specs = [lat(ML_DQK), ctx(ML_DQK), lat(ML_DQK), ctx(ML_DQK),
                lat_t(ML_DQK), ctx_t(ML_DQK), lat_t(ML_DV), ctx_t(ML_DV),
                pl.BlockSpec((seq, LANES), lambda b, h: (b, 0)),
                pl.BlockSpec((ctx_len, LANES), lambda b, h: (ctx0 + b, 0)),
                pl.BlockSpec((seq // blk, N_GATES, blk), lambda b, h: (b, 0, 0)),
                pl.BlockSpec((ctx_len // blk, N_GATES, blk), lambda b, h: (ctx0 + b, 0, 0))]
    out_specs = [pl.BlockSpec((seq, hps * ML_DV), lambda b, h: (b, h))]
    out_shape = [jax.ShapeDtypeStruct((n_lat, ML_V_W), BF16)]
    if need_ctx:
        out_specs.append(pl.BlockSpec((ctx_len, hps * ML_DV), lambda b, h: (b, h)))
        out_shape.append(jax.ShapeDtypeStruct((batch * ctx_len, ML_V_W), BF16))
    scratch = [pltpu.VMEM((hps, n_blk, ST_ROWS, ML_DQK), BF16),
               pltpu.VMEM((hps, n_blk, 8, 2 * LANES), F32)]
    out = pl.pallas_call(
        functools.partial(_mlstm_scan_kernel, need_ctx=need_ctx, heads_per_step=hps),
        grid=(batch, ML_HEADS // hps),
        in_specs=in_specs,
        out_specs=out_specs,
        out_shape=out_shape,
        scratch_shapes=scratch,
        compiler_params=_params(2),
        name="mlstm",
    )(mq, mq, mk, mk, mk_t, mk_t, mv_t, mv_t, gates3, gates3, gates_rows, gates_rows)
    return out if need_ctx else (out[0], None)


def _diffattn_kernel(lam_ref, g_ref, q_ref, *refs, n_seg, lam_init):
    k_refs = refs[:n_seg]
    v_refs = refs[n_seg:2 * n_seg]
    o_ref = refs[2 * n_seg]
    lv = lam_ref[...]
    lam = (jnp.exp(jnp.sum(lv[0:1] * lv[1:2], axis=1, keepdims=True))
           - jnp.exp(jnp.sum(lv[2:3] * lv[3:4], axis=1, keepdims=True)) + lam_init)
    for hh in range(q_ref.shape[1] // LANES):
        q = q_ref[:, hh * LANES:(hh + 1) * LANES]
        ks = [k_ref[:, hh * LANES:(hh + 1) * LANES] for k_ref in k_refs]
        vs = [v_ref[:, 2 * hh * DA_DV:2 * (hh + 1) * DA_DV] for v_ref in v_refs]
        lane = lax.broadcasted_iota(jnp.int32, q.shape, 1)
        zero = jnp.zeros_like(q)
        halves = [jnp.where(lane < DA_HALF, q, zero), jnp.where(lane >= DA_HALF, q, zero)]
        scores = [[_dot_nt(qh, k) for k in ks] for qh in halves]
        maxes = [functools.reduce(jnp.maximum, [jnp.max(s, axis=1, keepdims=True) for s in ss])
                 for ss in scores]
        probs = [[jnp.exp2((s - m).astype(BF16)) for s in ss] for ss, m in zip(scores, maxes)]
        heads = []
        for ps in probs:
            acc = None
            for p, v in zip(ps, vs):
                part = _dot(p, v)
                acc = part if acc is None else acc + part
            heads.append(acc[:, :DA_DV] * (1.0 / acc[:, DA_DV:]))
        o = heads[0] - lam * heads[1]
        ms = jnp.mean(o * o, axis=1, keepdims=True)
        o_ref[:, hh * DA_DV:(hh + 1) * DA_DV] = (o * lax.rsqrt(ms + LN_EPS) * g_ref[...]
                                                 * (1.0 - lam_init)).astype(o_ref.dtype)


def _diffattn(dq, dk, dv, lam_vecs, da_g, *, q_row0, q_len, segments, batch, lam_init, tq=1024,
              heads_per_step=1):
    tq = min(tq, q_len)
    nq = q_len // tq
    q0 = q_row0 // tq
    hps = heads_per_step

    def seg_spec(row0, length, width):
        return pl.BlockSpec((length, hps * width), lambda b, h, i: (row0 // length + b, h))

    return pl.pallas_call(
        functools.partial(_diffattn_kernel, n_seg=len(segments), lam_init=lam_init),
        grid=(batch, DA_HEADS // hps, nq),
        in_specs=[pl.BlockSpec(lam_vecs.shape, lambda b, h, i: (0, 0)),
                  pl.BlockSpec((1, DA_DV), lambda b, h, i: (0, 0)),
                  pl.BlockSpec((tq, hps * LANES), lambda b, h, i: (q0 + b * nq + i, h))]
                 + [seg_spec(r0, ln, LANES) for r0, ln in segments]
                 + [seg_spec(r0, ln, 2 * DA_DV) for r0, ln in segments],
        out_specs=pl.BlockSpec((tq, hps * DA_DV), lambda b, h, i: (b * nq + i, h)),
        out_shape=jax.ShapeDtypeStruct((batch * q_len, DA_V_W), BF16),
        compiler_params=_params(3),
        name="diffattn",
    )(lam_vecs, da_g.reshape(1, DA_DV), dq, *([dk] * len(segments)), *([dv] * len(segments)))


def _mixer_out_kernel(h_ref, mod_ref, sb_ref, p_ref, pprev_ref, pnext_ref, smo_ref, *refs,
                      seq, ctx_len, n_lat, alpha, two_source):
    tm = h_ref.shape[0]
    r0 = pl.program_id(0) * tm
    is_lat = r0 < n_lat
    if two_source:
        hml_l, hml_c, yda_l, yda_c = refs[:4]
        refs = refs[4:]
        hml = jnp.where(is_lat, hml_l[...], hml_c[...])
        yda = jnp.where(is_lat, yda_l[...], yda_c[...])
    else:
        hml, yda = refs[0][...], refs[1][...]
        refs = refs[2:]
    gs_ref, gm_ref, gd_ref, convw_ref, wsc_ref, wml_ref, wda_ref, wo_ref, lng_ref, lnb_ref, o_ref = refs
    row = lax.broadcasted_iota(jnp.int32, (tm, 1), 0)
    pos = jnp.where(is_lat, (r0 + row) % seq, (r0 - n_lat + row) % ctx_len)
    seq_len = jnp.where(is_lat, seq, ctx_len)
    first = pos == 0
    last = pos == seq_len - 1

    p = p_ref[...].astype(F32)
    prev_row = pprev_ref[...].astype(F32)[BF16_SUBLANES - 1:BF16_SUBLANES, :]
    next_row = pnext_ref[...].astype(F32)[0:1, :]
    p_before = jnp.where(first, 0.0, jnp.where(row == 0, prev_row, pltpu.roll(p, 1, 0)))
    p_after = jnp.where(last, 0.0, jnp.where(row == tm - 1, next_row, pltpu.roll(p, tm - 1, 0)))
    cw = convw_ref[...]
    conv = cw[0:1, :] * p_before + cw[1:2, :] * p + cw[2:3, :] * p_after
    y_sc = (sb_ref[...].astype(F32) * conv).astype(BF16)
    y_ml = smo_ref[...] * hml
    y = (gs_ref[...].astype(F32) * _dot(y_sc, wsc_ref[...].astype(BF16))
         + gm_ref[...].astype(F32) * _dot(y_ml, wml_ref[...].astype(BF16))
         + gd_ref[...].astype(F32) * _dot(yda, wda_ref[...].astype(BF16)))
    y = _dot(y.astype(BF16), wo_ref[...].astype(BF16))
    h = h_ref[...]
    o_ref[...] = _layer_norm(alpha * h + mod_ref[0, 5:6, :] * y, lng_ref[...], lnb_ref[...])


def _mixer_out(h, mods3, sb, p, smo, hml, yda, gs, gm, gd, conv_w, w_sc, w_ml, w_da, w_o, ln_g, ln_b,
               *, layer, n_rows, seq, ctx_len, batch, n_lat, alpha, hml_ctx=None, yda_ctx=None, tm=512):
    d = h.shape[1]
    tm = math.gcd(math.gcd(tm, seq), batch * ctx_len)
    halo = BF16_SUBLANES
    last_halo = p.shape[0] // halo - 1
    two_source = hml_ctx is not None

    def row_spec(width):
        return pl.BlockSpec((tm, width), lambda i: (i, 0))

    if two_source:
        branch = [hml, hml_ctx, yda, yda_ctx]
        branch_specs = (_two_source_specs(tm, ML_V_W, n_lat // tm) + _two_source_specs(tm, DA_V_W, n_lat // tm))
    else:
        branch = [hml, yda]
        branch_specs = [row_spec(ML_V_W), row_spec(DA_V_W)]
    kernel = functools.partial(_mixer_out_kernel, seq=seq, ctx_len=ctx_len, n_lat=n_lat, alpha=alpha,
                               two_source=two_source)
    return pl.pallas_call(
        kernel,
        grid=(n_rows // tm,),
        in_specs=[row_spec(d),
                  pl.BlockSpec((1, N_MOD, d), lambda i: (jnp.minimum(i * tm // seq, batch), 0, 0)),
                  row_spec(SC_WIDTH), row_spec(SC_WIDTH),
                  pl.BlockSpec((halo, SC_WIDTH), lambda i: (jnp.maximum(i * (tm // halo) - 1, 0), 0)),
                  pl.BlockSpec((halo, SC_WIDTH),
                               lambda i: (jnp.minimum((i + 1) * (tm // halo), last_halo), 0)),
                  row_spec(ML_V_W)] + branch_specs + [
                  row_spec(d), row_spec(d), row_spec(d),
                  _resident((SC_KSIZE, SC_WIDTH)),
                  _resident_layer(w_sc, layer), _resident_layer(w_ml, layer), _resident_layer(w_da, layer),
                  _resident_layer(w_o, layer),
                  pl.BlockSpec((1, d), lambda i: (0, 0)),
                  pl.BlockSpec((1, d), lambda i: (0, 0))],
        out_specs=row_spec(d),
        out_shape=jax.ShapeDtypeStruct((n_rows, d), F32),
        compiler_params=_params(1),
        name="mixer_out",
    )(h, mods3, sb, p, p, p, smo, *branch, gs, gm, gd, conv_w, w_sc, w_ml, w_da, w_o,
      ln_g.reshape(1, d), ln_b.reshape(1, d))


def _rope_tables(seq, tile):
    n_freq = DA_HALF // 4
    t = jnp.arange(seq)
    row_ids = (t // GRID_W).astype(F32)
    col_ids = (t % GRID_W).astype(F32)
    inv = ROPE_BASE ** (-jnp.arange(n_freq, dtype=F32) / n_freq)
    ang = jnp.concatenate([row_ids[:, None] * inv, col_ids[:, None] * inv], axis=-1)
    cos, sin = jnp.cos(ang), jnp.sin(ang)
    reps = LANES // DA_HALF
    cos_t = jnp.tile(jnp.concatenate([cos, cos], axis=-1), (1, reps))
    sin_t = jnp.tile(jnp.concatenate([-sin, sin], axis=-1), (1, reps))
    cos_t = jnp.concatenate([cos_t, jnp.ones((tile, LANES), F32)], axis=0)
    sin_t = jnp.concatenate([sin_t, jnp.zeros((tile, LANES), F32)], axis=0)
    return cos_t, sin_t


def kernel(x, c, ctx, c_ctx, w_ada, b_ada, ln_g, ln_b, ffn1_up, ffn1_down, ffn2_up, ffn2_down, w_in, b_in,
           conv_w, w_sc, w_ml, w_da, w_o, lam_q1, lam_k1, lam_q2, lam_k2, da_norm_g):
    batch, seq, d = x.shape
    ctx_len = ctx.shape[1]
    depth = w_ada.shape[0]
    alpha = (2 * depth) ** 0.25
    n_lat = batch * seq
    n_ctx = batch * ctx_len
    mixer_in_tile = min(256, seq)

    n_cond = -(-(batch + 1) // 8) * 8
    cc = jnp.concatenate([c, c_ctx[None, :], jnp.zeros((n_cond - batch - 1, d), F32)], axis=0)
    mods = _ada(cc, w_ada, b_ada).reshape(depth, n_cond, N_MOD, d)

    cos_t, sin_t = _rope_tables(seq, mixer_in_tile)
    gate_lo = GATE_COL0
    gate_hi = gate_lo + N_GATES
    w_in_t = jnp.swapaxes(w_in, 1, 2)

    h, h_ctx = x.reshape(n_lat, d), ctx.reshape(n_ctx, d)
    for l in range(depth):
        last = l == depth - 1
        lam_init = 0.8 - 0.6 * math.exp(-0.3 * l)
        mods3 = mods[l]
        b_head = b_in[l, :gate_lo][None, :]
        b_tail = b_in[l, gate_hi:][None, :]
        b_gate = b_in[l, gate_lo:gate_hi][None, :]
        lam_vecs = jnp.stack([lam_q1[l], lam_k1[l], lam_q2[l], lam_k2[l]]).astype(F32)

        h, w_proj_t = _ffn(h, mods3, ffn1_up, ffn1_down, ln_g[l, 0], ln_b[l, 0], layer=l, mod_base=0,
                           n_rows=n_lat + n_ctx, seq=seq, batch=batch, alpha=alpha, h_ctx=h_ctx,
                           cast_stack=w_in_t)
        h_ctx = None

        (sb, p, mq, mk, mv, smo, mg, dq, dk, dv, gs, gm, gd) = _mixer_in(
            h, mods3, w_proj_t, b_head, b_tail, b_gate, cos_t, sin_t,
            seq=seq, batch=batch, n_lat=n_lat, tm=mixer_in_tile)

        hml, hml_ctx = _mlstm_scan(mq, mk, mv, mg, seq=seq, ctx_len=ctx_len, batch=batch, n_lat=n_lat,
                                   need_ctx=not last)
        yda = _diffattn(dq, dk, dv, lam_vecs, da_norm_g[l], q_row0=0, q_len=seq,
                        segments=((0, seq), (n_lat, ctx_len)), batch=batch, lam_init=lam_init)
        yda_ctx = None
        if not last:
            yda_ctx = _diffattn(dq, dk, dv, lam_vecs, da_norm_g[l], q_row0=n_lat, q_len=ctx_len,
                                segments=((n_lat, ctx_len),), batch=batch, lam_init=lam_init,
                                heads_per_step=DA_HEADS)

        n_rows = n_lat if last else n_lat + n_ctx
        h = _mixer_out(h, mods3, sb, p, smo, hml, yda, gs, gm, gd, conv_w[l], w_sc, w_ml, w_da, w_o,
                       ln_g[l, 1], ln_b[l, 1], hml_ctx=hml_ctx, yda_ctx=yda_ctx, layer=l,
                       n_rows=n_rows, seq=seq, ctx_len=ctx_len, batch=batch, n_lat=n_lat, alpha=alpha)
        h = _ffn(h, mods3, ffn2_up, ffn2_down, ln_g[l, 2], ln_b[l, 2], layer=l,
                 mod_base=6, n_rows=n_rows, seq=seq, batch=batch, alpha=alpha)
    return h[:n_lat].reshape(batch, seq, d)
```

```python
import functools
import math

import jax
import jax.numpy as jnp
from jax import lax
from jax.experimental import pallas as pl
from jax.experimental.pallas import tpu as pltpu

GRID_W = 64
N_MOD = 9
SC_WIDTH = 512
SC_KSIZE = 3
ML_HEADS = 4
ML_DQK = 128
ML_DV = 256
DA_HEADS = 4
DA_HALF = 64
DA_DV = 2 * DA_HALF
ROPE_BASE = 10000.0
LN_EPS = 1e-5

ML_QK_W = ML_HEADS * ML_DQK
ML_V_W = ML_HEADS * ML_DV
DA_QK_W = DA_HEADS * 2 * DA_HALF
DA_V_W = DA_HEADS * DA_DV
N_GATES = 4 * ML_HEADS

ML_BLOCK = 256
LANES = 128
BF16_SUBLANES = 16
V7X_VMEM_BYTES = 64 * 1024 * 1024
VMEM_LIMIT = V7X_VMEM_BYTES - 8 * 1024 * 1024

FFN_TILE = 512
FFN_COLS = 256
MIXER_IN_TILE = 256
MIXER_OUT_TILE = 512
DIFFATTN_TILE = 1024

F32 = jnp.float32
BF16 = jnp.bfloat16
NEG_BIG = -1e30
LOG2_E = 1.4426950408889634

assert ML_DQK == LANES and DA_DV == LANES and ML_BLOCK == 2 * LANES


def _dot(a, b):
    return jnp.dot(a, b, preferred_element_type=F32)


def _dot_nt(a, b):
    return lax.dot_general(a, b, (((1,), (1,)), ((), ())), preferred_element_type=F32)


def _wide(x, width):
    return jnp.concatenate([x] * (width // LANES), axis=1)


def _layer_norm(y, g, b):
    mu = jnp.mean(y, axis=-1, keepdims=True)
    yc = y - mu
    var = jnp.mean(yc * yc, axis=-1, keepdims=True)
    return yc * lax.rsqrt(var + LN_EPS) * g + b


def _resident(shape):
    return pl.BlockSpec(shape, lambda *_: (0,) * len(shape), pipeline_mode=pl.Buffered(1))


def _resident_layer(stacked, layer):
    n = stacked.ndim - 1
    return pl.BlockSpec((None,) + stacked.shape[1:], lambda *_: (layer,) + (0,) * n,
                        pipeline_mode=pl.Buffered(1))


def _params(n_axes):
    return pltpu.CompilerParams(dimension_semantics=("parallel",) * n_axes,
                                vmem_limit_bytes=VMEM_LIMIT)


def _ada_kernel(c_ref, w_ref, b_ref, o_ref):
    c = c_ref[...]
    a = (c * jax.nn.sigmoid(c)).astype(BF16)
    o_ref[0] = _dot(a, w_ref[0].astype(BF16)) + b_ref[0]


def _ada(cc, w_ada, b_ada):
    depth, d, n = w_ada.shape
    tn = d
    return pl.pallas_call(
        _ada_kernel,
        grid=(depth, n // tn),
        in_specs=[pl.BlockSpec(cc.shape, lambda l, j: (0, 0)),
                  pl.BlockSpec((1, d, tn), lambda l, j: (l, 0, j)),
                  pl.BlockSpec((1, 1, tn), lambda l, j: (l, 0, j))],
        out_specs=pl.BlockSpec((1, cc.shape[0], tn), lambda l, j: (l, 0, j)),
        out_shape=jax.ShapeDtypeStruct((depth, cc.shape[0], n), F32),
        compiler_params=_params(2),
        name="ada",
    )(cc, w_ada, b_ada.reshape(depth, 1, n))


def _two_source_specs(tm, width, lat_tiles):
    return [pl.BlockSpec((tm, width), lambda i: (jnp.minimum(i, lat_tiles - 1), 0)),
            pl.BlockSpec((tm, width), lambda i: (jnp.maximum(i - lat_tiles, 0), 0))]


def _ffn_kernel(*refs, mod_base, alpha, lat_tiles, cast_extra):
    if lat_tiles is None:
        h_ref, hc_ref = refs[0], None
        refs = refs[1:]
    else:
        h_ref, hc_ref = refs[:2]
        refs = refs[2:]
    if cast_extra:
        mod_ref, wup_ref, wdn_ref, lng_ref, lnb_ref, extra_ref, o_ref, extra_o, g_scr = refs
        extra_o[...] = extra_ref[...].astype(BF16)
    else:
        mod_ref, wup_ref, wdn_ref, lng_ref, lnb_ref, o_ref, g_scr = refs
    h = h_ref[...] if hc_ref is None else jnp.where(pl.program_id(0) < lat_tiles, h_ref[...], hc_ref[...])
    f = wdn_ref.shape[0]
    shift = mod_ref[0, mod_base:mod_base + 1, :]
    scale = mod_ref[0, mod_base + 1:mod_base + 2, :]
    gate = mod_ref[0, mod_base + 2:mod_base + 3, :]
    u = (h * (1.0 + scale) + shift).astype(BF16)
    for c in range(f // FFN_COLS):
        lo, hi = c * FFN_COLS, (c + 1) * FFN_COLS
        a = _dot(u, wup_ref[:, lo:hi].astype(BF16))
        v = _dot(u, wup_ref[:, f + lo:f + hi].astype(BF16))
        g_scr[:, lo:hi] = (a * jax.nn.sigmoid(a) * v).astype(BF16)
    d = _dot(g_scr[...], wdn_ref[...].astype(BF16))
    o_ref[...] = _layer_norm(alpha * h + (0.5 * gate) * d, lng_ref[...], lnb_ref[...])


def _ffn(h, mods3, w_up, w_dn, ln_g, ln_b, *, layer, mod_base, n_rows, seq, batch, alpha, h_ctx=None,
         cast_stack=None):
    d = h.shape[1]
    f = w_dn.shape[1]
    tm = min(FFN_TILE, seq)
    steps = n_rows // tm
    lat_tiles = None if h_ctx is None else h.shape[0] // tm
    kernel = functools.partial(_ffn_kernel, mod_base=mod_base, alpha=alpha, lat_tiles=lat_tiles,
                               cast_extra=cast_stack is not None)
    if h_ctx is None:
        sources, source_specs = [h], [pl.BlockSpec((tm, d), lambda i: (i, 0))]
    else:
        sources, source_specs = [h, h_ctx], _two_source_specs(tm, d, lat_tiles)
    extra, extra_specs = [], []
    out_specs = [pl.BlockSpec((tm, d), lambda i: (i, 0))]
    out_shape = [jax.ShapeDtypeStruct((n_rows, d), F32)]
    if cast_stack is not None:
        n_extra = cast_stack.shape[1]
        rows_blk = -(-pl.cdiv(n_extra, steps) // BF16_SUBLANES) * BF16_SUBLANES
        last_blk = pl.cdiv(n_extra, rows_blk) - 1
        extra = [cast_stack]
        extra_specs = [pl.BlockSpec((None, rows_blk, d), lambda i: (layer, jnp.minimum(i, last_blk), 0))]
        out_specs.append(pl.BlockSpec((rows_blk, d), lambda i: (jnp.minimum(i, last_blk), 0)))
        out_shape.append(jax.ShapeDtypeStruct((n_extra, d), BF16))
    out = pl.pallas_call(
        kernel,
        grid=(steps,),
        in_specs=source_specs + [
                  pl.BlockSpec((1, N_MOD, d), lambda i: (jnp.minimum(i * tm // seq, batch), 0, 0)),
                  _resident_layer(w_up, layer),
                  _resident_layer(w_dn, layer),
                  pl.BlockSpec((1, d), lambda i: (0, 0)),
                  pl.BlockSpec((1, d), lambda i: (0, 0))] + extra_specs,
        out_specs=out_specs,
        out_shape=out_shape,
        scratch_shapes=[pltpu.VMEM((tm, f), BF16)],
        compiler_params=pltpu.CompilerParams(
            dimension_semantics=("arbitrary" if cast_stack is not None else "parallel",),
            vmem_limit_bytes=VMEM_LIMIT),
        name="ffn",
    )(*sources, mods3, w_up, w_dn, ln_g.reshape(1, d), ln_b.reshape(1, d), *extra)
    return out if cast_stack is not None else out[0]


GATE_COL0 = 3 * SC_WIDTH + 2 * ML_QK_W + 2 * ML_V_W


def _proj_cols(d):
    parts = ((("sb", SC_WIDTH), ("sc", SC_WIDTH), ("sx", SC_WIDTH),
              ("mq", ML_QK_W), ("mk", ML_QK_W), ("mv", ML_V_W), ("mo", ML_V_W)),
             (("dq", DA_QK_W), ("dk", DA_QK_W), ("dv", DA_V_W), ("gs", d), ("gm", d), ("gd", d)))
    cols, widths = {}, []
    for part, sizes in enumerate(parts):
        off = 0
        for name, w in sizes:
            cols[name] = (part, off, off + w)
            off += w
        widths.append(off)
    return cols, widths


def _mixer_in_kernel(h_ref, mod_ref, w_ref, bh_ref, bt_ref, bg_ref, cos_ref, sin_ref,
                     sb_o, p_o, mq_o, mk_o, mv_o, smo_o, mg_o, dq_o, dk_o, dv_o, gs_o, gm_o, gd_o):
    d = h_ref.shape[1]
    tm = h_ref.shape[0]
    cols, _ = _proj_cols(d)
    h = h_ref[...]
    shift = mod_ref[0, 3:4, :]
    scale = mod_ref[0, 4:5, :]
    u = (h * (1.0 + scale) + shift).astype(BF16)

    def proj(name):
        part, lo, hi = cols[name]
        b_ref = (bh_ref, bt_ref)[part]
        row0 = 0 if part == 0 else GATE_COL0 + N_GATES
        return _dot_nt(u, w_ref[row0 + lo:row0 + hi, :]) + b_ref[:, lo:hi]

    sb_o[...] = proj("sb").astype(BF16)
    p_o[...] = (proj("sc") * proj("sx")).astype(BF16)
    mq_o[...] = proj("mq").astype(BF16)
    mk_o[...] = (proj("mk") * (ML_DQK ** -0.5)).astype(BF16)
    mv_o[...] = proj("mv").astype(BF16)
    smo_o[...] = jax.nn.sigmoid(proj("mo")).astype(BF16)
    mg_o[...] = _dot_nt(u, w_ref[GATE_COL0:GATE_COL0 + N_GATES, :]) + bg_ref[...]

    cos_t = cos_ref[...]
    sin_t = sin_ref[...]
    lane = lax.broadcasted_iota(jnp.int32, cos_t.shape, 1)
    first_half = (lane % DA_HALF) < (DA_HALF // 2)

    def rope_store(z, out_ref, mult):
        for k in range(z.shape[1] // LANES):
            x = z[:, k * LANES:(k + 1) * LANES]
            partner = jnp.where(first_half,
                                pltpu.roll(x, LANES - DA_HALF // 2, 1),
                                pltpu.roll(x, DA_HALF // 2, 1))
            out_ref[:, k * LANES:(k + 1) * LANES] = ((x * cos_t + partner * sin_t) * mult).astype(BF16)

    rope_store(proj("dq"), dq_o, (DA_HALF ** -0.5) * LOG2_E)
    rope_store(proj("dk"), dk_o, 1.0)
    dv = proj("dv").astype(BF16)
    ones = jnp.ones((tm, DA_DV), BF16)
    for k in range(DA_HEADS):
        dv_o[:, 2 * k * DA_DV:(2 * k + 1) * DA_DV] = dv[:, k * DA_DV:(k + 1) * DA_DV]
        dv_o[:, (2 * k + 1) * DA_DV:(2 * k + 2) * DA_DV] = ones
    gs_o[...] = jax.nn.sigmoid(proj("gs")).astype(BF16)
    gm_o[...] = jax.nn.sigmoid(proj("gm")).astype(BF16)
    gd_o[...] = jax.nn.sigmoid(proj("gd")).astype(BF16)


def _mixer_in(h, mods3, w_t, b_head, b_tail, b_gate, cos_t, sin_t, *, seq, batch, n_lat):
    rows, d = h.shape
    _, (n_head, n_tail) = _proj_cols(d)
    tm = min(MIXER_IN_TILE, seq)
    lat_tiles = n_lat // tm
    rope_blocks = seq // tm

    def row_spec(width):
        return pl.BlockSpec((tm, width), lambda i: (i, 0))

    def rope_map(i):
        return (jnp.where(i < lat_tiles, i % rope_blocks, rope_blocks), 0)

    widths = (SC_WIDTH, SC_WIDTH, ML_QK_W, ML_QK_W, ML_V_W, ML_V_W, N_GATES,
              DA_QK_W, DA_QK_W, 2 * DA_V_W, d, d, d)
    dtypes = (BF16,) * 6 + (F32,) + (BF16,) * 6
    return pl.pallas_call(
        _mixer_in_kernel,
        grid=(rows // tm,),
        in_specs=[row_spec(d),
                  pl.BlockSpec((1, N_MOD, d), lambda i: (jnp.minimum(i * tm // seq, batch), 0, 0)),
                  _resident(w_t.shape),
                  _resident((1, n_head)),
                  _resident((1, n_tail)),
                  _resident((1, N_GATES)),
                  pl.BlockSpec((tm, LANES), rope_map),
                  pl.BlockSpec((tm, LANES), rope_map)],
        out_specs=[row_spec(w) for w in widths],
        out_shape=[jax.ShapeDtypeStruct((rows, w), dt) for w, dt in zip(widths, dtypes)],
        compiler_params=_params(1),
        name="mixer_in",
    )(h, mods3, w_t, b_head, b_tail, b_gate, cos_t, sin_t)


def _log_sigmoid(x):
    return jnp.minimum(x, 0.0) - jnp.log(1.0 + jnp.exp(-jnp.abs(x)))


def _split3(x):
    pieces, rest = [], x
    for _ in range(3):
        piece = rest.astype(BF16)
        pieces.append(piece)
        rest = rest - piece.astype(F32)
    return pieces


def _mlstm_gate_kernel(g_ref, o_ref, o3_ref):
    blk = ML_BLOCK
    t_idx = lax.broadcasted_iota(jnp.int32, (blk, blk), 0)
    s_idx = lax.broadcasted_iota(jnp.int32, (blk, blk), 1)
    tril = (s_idx <= t_idx).astype(BF16)
    lane = lax.broadcasted_iota(jnp.int32, (blk, N_GATES), 1)
    src = lax.broadcasted_iota(jnp.int32, (N_GATES, LANES), 0)
    dst = lax.broadcasted_iota(jnp.int32, (N_GATES, LANES), 1)
    places = [(dst == src + term * N_GATES).astype(BF16) for term in range(3)]
    for j in range(g_ref.shape[0] // blk):
        sl = pl.ds(j * blk, blk)
        g = g_ref[sl, :]
        ls = _log_sigmoid(g)
        prefix = functools.reduce(jnp.add, [_dot(tril, piece) for piece in _split3(ls)])
        suffix = prefix[blk - 1:blk, :] - prefix + ls
        out = jnp.where(lane < 2 * ML_HEADS, g, jnp.where(lane < 3 * ML_HEADS, prefix, suffix))
        o_ref[sl, :] = out
        placed = functools.reduce(jnp.add, [_dot(piece, place) for piece, place in zip(_split3(out), places)])
        o3_ref[sl, :] = placed.astype(BF16)


def _mlstm_gates(mg):
    rows = mg.shape[0]
    step = math.gcd(rows, 8 * ML_BLOCK)
    return pl.pallas_call(
        _mlstm_gate_kernel,
        grid=(rows // step,),
        in_specs=[pl.BlockSpec((step, N_GATES), lambda i: (i, 0))],
        out_specs=[pl.BlockSpec((step, N_GATES), lambda i: (i, 0)),
                   pl.BlockSpec((step, LANES), lambda i: (i, 0))],
        out_shape=[jax.ShapeDtypeStruct((rows, N_GATES), F32),
                   jax.ShapeDtypeStruct((rows, LANES), BF16)],
        compiler_params=_params(1),
        name="mlstm_gates",
    )(mg)


ML_HEADS_PER_STEP = 4
N_ROWS_PAD = BF16_SUBLANES
ST_ROWS = 2 * N_ROWS_PAD + 2 * ML_DV


def _mlstm_scan_kernel(*refs, need_ctx, heads_per_step):
    ins, rest = refs[:12], refs[12:]
    if need_ctx:
        hl_ref, hc_ref, st_in, m_in = rest
    else:
        hl_ref, st_in, m_in = rest
        hc_ref = None
    ql, qc, kl, kc, ktl, ktc, vtl, vtc, g3l, g3c, grl, grc = ins
    for hh in range(heads_per_step):
        qk_cols = pl.ds(hh * ML_DQK, ML_DQK)
        v_cols = pl.ds(hh * ML_DV, ML_DV)
        _mlstm_head(pl.program_id(1) * heads_per_step + hh,
                    ql.at[:, qk_cols], qc.at[:, qk_cols], kl.at[:, qk_cols], kc.at[:, qk_cols],
                    ktl.at[:, qk_cols, :], ktc.at[:, qk_cols, :], vtl.at[:, v_cols, :], vtc.at[:, v_cols, :],
                    g3l, g3c, grl, grc, hl_ref.at[:, v_cols],
                    None if hc_ref is None else hc_ref.at[:, v_cols], st_in.at[hh], m_in.at[hh])


def _mlstm_head(head, ql_ref, qc_ref, kl_ref, kc_ref, ktl_ref, ktc_ref, vtl_ref, vtc_ref,
                g3l_ref, g3c_ref, grl_ref, grc_ref, hl_ref, hc_ref, st_in, m_in):
    need_ctx = hc_ref is not None
    blk = ML_BLOCK
    n_lat_blk = ql_ref.shape[0] // blk
    n_ctx_blk = qc_ref.shape[0] // blk

    s_idx = lax.broadcasted_iota(jnp.int32, (blk, blk), 0)
    t_idx = lax.broadcasted_iota(jnp.int32, (blk, blk), 1)
    lane_t = lax.broadcasted_iota(jnp.int32, (1, blk), 1)
    ones_n = jnp.ones((N_ROWS_PAD, blk), BF16)
    sel_src = lax.broadcasted_iota(jnp.int32, (LANES, 2 * LANES), 0)
    sel_dst = lax.broadcasted_iota(jnp.int32, (LANES, 2 * LANES), 1)
    sel_bwd = sel_dst >= LANES
    col = sel_src % N_GATES
    in_terms = sel_src < 3 * N_GATES
    plus = jnp.logical_and(col == head + jnp.where(sel_bwd, ML_HEADS, 0), in_terms)
    minus = jnp.logical_and(col == head + jnp.where(sel_bwd, 3 * ML_HEADS, 2 * ML_HEADS), in_terms)
    sel = (plus.astype(F32) - minus.astype(F32)).astype(BF16)

    def gate_rows(gr_ref, j, bwd):
        i_idx = head + (ML_HEADS if bwd else 0)
        f_idx = head + (3 * ML_HEADS if bwd else 2 * ML_HEADS)
        return gr_ref[j, pl.ds(f_idx, 1), :], gr_ref[j, pl.ds(i_idx, 1), :]

    def scan_states():
        ctx_blocks = [(grc_ref, ktc_ref, vtc_ref, j, j) for j in range(n_ctx_blk)]
        lat_blocks = [(grl_ref, ktl_ref, vtl_ref, j, n_ctx_blk + j) for j in range(n_lat_blk)]
        orders = (ctx_blocks + lat_blocks, ctx_blocks[::-1] + lat_blocks[::-1])
        m_prev = [jnp.zeros((1, 1), F32)] * 2
        ct_run = [jnp.zeros((ML_DV, ML_DQK), F32)] * 2
        n_run = [jnp.zeros((N_ROWS_PAD, ML_DQK), F32)] * 2
        for step in range(len(orders[0])):
            for d, bwd in enumerate((False, True)):
                gr_ref, kt_ref, vt_ref, j, slot = orders[d][step]
                c_row0 = 2 * N_ROWS_PAD + d * ML_DV
                f_row, i_row = gate_rows(gr_ref, j, bwd)
                b_end = jnp.sum(jnp.where(lane_t == (0 if bwd else blk - 1), f_row, 0.0), axis=1,
                                keepdims=True)
                g_log = b_end + (i_row - f_row)
                m_new = jnp.maximum(b_end + m_prev[d], jnp.max(g_log, axis=1, keepdims=True))
                a_prev = jnp.exp(b_end + m_prev[d] - m_new)
                kw = (kt_ref[j].astype(F32) * jnp.exp(g_log - m_new)).astype(BF16)
                st_in[slot, d * N_ROWS_PAD:(d + 1) * N_ROWS_PAD, :] = n_run[d].astype(BF16)
                st_in[slot, c_row0:c_row0 + ML_DV, :] = ct_run[d].astype(BF16)
                m_in[slot, :, d * LANES:(d + 1) * LANES] = jnp.broadcast_to(m_prev[d], (8, LANES))
                ct_run[d] = a_prev * ct_run[d] + _dot_nt(vt_ref[j], kw)
                n_run[d] = a_prev * n_run[d] + _dot_nt(ones_n, kw)
                m_prev[d] = m_new

    def outputs(q, k, vt, g3, rows, slot):
        qk_t = _dot_nt(k, q)
        state = _dot_nt(st_in[slot], q)
        r_both = _dot(g3, sel)
        m_prev_both = m_in[slot][0:1, :]
        scaled, inter = [], []
        for d, bwd in enumerate((False, True)):
            f_row, _ = rows[d]
            r_rep = r_both[:, d * LANES:(d + 1) * LANES]
            mask = (s_idx >= t_idx) if bwd else (s_idx <= t_idx)
            d_log = jnp.where(mask, _wide(r_rep, blk) + f_row, NEG_BIG)
            m_loc = jnp.max(d_log, axis=0, keepdims=True)
            s = qk_t * jnp.exp(d_log - m_loc)
            den_loc = jnp.sum(s, axis=0, keepdims=True)
            m_inter = f_row + m_prev_both[:, d * LANES:d * LANES + 1]
            m_t = jnp.maximum(m_inter, m_loc)
            w_inter = jnp.exp(m_inter - m_t)
            w_loc = jnp.exp(m_loc - m_t)
            qn = state[d * N_ROWS_PAD:d * N_ROWS_PAD + 1, :]
            den = w_inter * qn + w_loc * den_loc
            inv = 1.0 / jnp.maximum(jnp.abs(den), jnp.exp(-m_t))
            scaled.append(s * (w_loc * inv))
            c_row0 = 2 * N_ROWS_PAD + d * ML_DV
            inter.append(state[c_row0:c_row0 + ML_DV, :] * (w_inter * inv))
        return _dot(vt, (scaled[0] + scaled[1]).astype(BF16)) + inter[0] + inter[1]

    def lat_rows(j):
        return pl.ds(pl.multiple_of(j * blk, blk), blk)

    scan_states()

    if need_ctx:
        for j in range(n_ctx_blk):
            sl = pl.ds(j * blk, blk)
            rows = (gate_rows(grc_ref, j, False), gate_rows(grc_ref, j, True))
            out_t = outputs(qc_ref[sl, :], kc_ref[sl, :], vtc_ref[j], g3c_ref[sl, :], rows, j)
            hc_ref[sl, :] = out_t.T.astype(hc_ref.dtype)

    def out_body(j, carry):
        sl = lat_rows(j)
        rows = (gate_rows(grl_ref, j, False), gate_rows(grl_ref, j, True))
        out_t = outputs(ql_ref[sl, :], kl_ref[sl, :], vtl_ref[j], g3l_ref[sl, :], rows, n_ctx_blk + j)
        hl_ref[sl, :] = out_t.T.astype(hl_ref.dtype)
        return carry

    lax.fori_loop(0, n_lat_blk, out_body, 0, unroll=True)


def _mlstm_scan(mq, mk, mv, mg, *, seq, ctx_len, batch, n_lat, need_ctx):
    blk = ML_BLOCK
    rows = mq.shape[0]
    ctx0 = n_lat // ctx_len
    n_blk = (seq + ctx_len) // blk
    gates, gates3 = _mlstm_gates(mg)
    gates_rows = gates.T.reshape(N_GATES, rows // blk, blk).transpose(1, 0, 2)
    mk_t = mk.reshape(rows // blk, blk, ML_QK_W).transpose(0, 2, 1)
    mv_t = mv.reshape(rows // blk, blk, ML_V_W).transpose(0, 2, 1)

    hps = ML_HEADS_PER_STEP

    def lat(width):
        return pl.BlockSpec((seq, hps * width), lambda b, h: (b, h))

    def ctx(width):
        return pl.BlockSpec((ctx_len, hps * width), lambda b, h: (ctx0 + b, h))

    def lat_t(width):
        return pl.BlockSpec((seq // blk, hps * width, blk), lambda b, h: (b, h, 0))

    def ctx_t(width):
        return pl.BlockSpec((ctx_len // blk, hps * width, blk), lambda b, h: (ctx0 + b, h, 0))

    in_specs = [lat(ML_DQK), ctx(ML_DQK), lat(ML_DQK), ctx(ML_DQK),
                lat_t(ML_DQK), ctx_t(ML_DQK), lat_t(ML_DV), ctx_t(ML_DV),
                pl.BlockSpec((seq, LANES), lambda b, h: (b, 0)),
                pl.BlockSpec((ctx_len, LANES), lambda b, h: (ctx0 + b, 0)),
                pl.BlockSpec((seq // blk, N_GATES, blk), lambda b, h: (b, 0, 0)),
                pl.BlockSpec((ctx_len // blk, N_GATES, blk), lambda b, h: (ctx0 + b, 0, 0))]
    out_specs = [pl.BlockSpec((seq, hps * ML_DV), lambda b, h: (b, h))]
    out_shape = [jax.ShapeDtypeStruct((n_lat, ML_V_W), BF16)]
    if need_ctx:
        out_specs.append(pl.BlockSpec((ctx_len, hps * ML_DV), lambda b, h: (b, h)))
        out_shape.append(jax.ShapeDtypeStruct((batch * ctx_len, ML_V_W), BF16))
    scratch = [pltpu.VMEM((hps, n_blk, ST_ROWS, ML_DQK), BF16),
               pltpu.VMEM((hps, n_blk, 8, 2 * LANES), F32)]
    out = pl.pallas_call(
        functools.partial(_mlstm_scan_kernel, need_ctx=need_ctx, heads_per_step=hps),
        grid=(batch, ML_HEADS // hps),
        in_specs=in_specs,
        out_specs=out_specs,
        out_shape=out_shape,
        scratch_shapes=scratch,
        compiler_params=_params(2),
        name="mlstm",
    )(mq, mq, mk, mk, mk_t, mk_t, mv_t, mv_t, gates3, gates3, gates_rows, gates_rows)
    return out if need_ctx else (out[0], None)


def _diffattn_kernel(lam_ref, g_ref, q_ref, *refs, n_seg, lam_init):
    k_refs = refs[:n_seg]
    v_refs = refs[n_seg:2 * n_seg]
    o_ref = refs[2 * n_seg]
    lv = lam_ref[...]
    lam = (jnp.exp(jnp.sum(lv[0:1] * lv[1:2], axis=1, keepdims=True))
           - jnp.exp(jnp.sum(lv[2:3] * lv[3:4], axis=1, keepdims=True)) + lam_init)
    for hh in range(q_ref.shape[1] // LANES):
        q = q_ref[:, hh * LANES:(hh + 1) * LANES]
        ks = [k_ref[:, hh * LANES:(hh + 1) * LANES] for k_ref in k_refs]
        vs = [v_ref[:, 2 * hh * DA_DV:2 * (hh + 1) * DA_DV] for v_ref in v_refs]
        lane = lax.broadcasted_iota(jnp.int32, q.shape, 1)
        zero = jnp.zeros_like(q)
        halves = [jnp.where(lane < DA_HALF, q, zero), jnp.where(lane >= DA_HALF, q, zero)]
        scores = [[_dot_nt(qh, k) for k in ks] for qh in halves]
        maxes = [functools.reduce(jnp.maximum, [jnp.max(s, axis=1, keepdims=True) for s in ss])
                 for ss in scores]
        probs = [[jnp.exp2((s - m).astype(BF16)) for s in ss] for ss, m in zip(scores, maxes)]
        heads = []
        for ps in probs:
            acc = None
            for p, v in zip(ps, vs):
                part = _dot(p, v)
                acc = part if acc is None else acc + part
            heads.append(acc[:, :DA_DV] * (1.0 / acc[:, DA_DV:]))
        o = heads[0] - lam * heads[1]
        ms = jnp.mean(o * o, axis=1, keepdims=True)
        o_ref[:, hh * DA_DV:(hh + 1) * DA_DV] = (o * lax.rsqrt(ms + LN_EPS) * g_ref[...]
                                                 * (1.0 - lam_init)).astype(o_ref.dtype)


def _diffattn(dq, dk, dv, lam_vecs, da_g, *, q_row0, q_len, segments, batch, lam_init, heads_per_step=1):
    tq = min(DIFFATTN_TILE, q_len)
    nq = q_len // tq
    q0 = q_row0 // tq
    hps = heads_per_step

    def seg_spec(row0, length, width):
        return pl.BlockSpec((length, hps * width), lambda b, h, i: (row0 // length + b, h))

    return pl.pallas_call(
        functools.partial(_diffattn_kernel, n_seg=len(segments), lam_init=lam_init),
        grid=(batch, DA_HEADS // hps, nq),
        in_specs=[pl.BlockSpec(lam_vecs.shape, lambda b, h, i: (0, 0)),
                  pl.BlockSpec((1, DA_DV), lambda b, h, i: (0, 0)),
                  pl.BlockSpec((tq, hps * LANES), lambda b, h, i: (q0 + b * nq + i, h))]
                 + [seg_spec(r0, ln, LANES) for r0, ln in segments]
                 + [seg_spec(r0, ln, 2 * DA_DV) for r0, ln in segments],
        out_specs=pl.BlockSpec((tq, hps * DA_DV), lambda b, h, i: (b * nq + i, h)),
        out_shape=jax.ShapeDtypeStruct((batch * q_len, DA_V_W), BF16),
        compiler_params=_params(3),
        name="diffattn",
    )(lam_vecs, da_g.reshape(1, DA_DV), dq, *([dk] * len(segments)), *([dv] * len(segments)))


def _mixer_out_kernel(h_ref, mod_ref, sb_ref, p_ref, pprev_ref, pnext_ref, smo_ref, *refs,
                      seq, ctx_len, n_lat, alpha, two_source):
    tm = h_ref.shape[0]
    r0 = pl.program_id(0) * tm
    is_lat = r0 < n_lat
    if two_source:
        hml_l, hml_c, yda_l, yda_c = refs[:4]
        refs = refs[4:]
        hml = jnp.where(is_lat, hml_l[...], hml_c[...])
        yda = jnp.where(is_lat, yda_l[...], yda_c[...])
    else:
        hml, yda = refs[0][...], refs[1][...]
        refs = refs[2:]
    gs_ref, gm_ref, gd_ref, convw_ref, wsc_ref, wml_ref, wda_ref, wo_ref, lng_ref, lnb_ref, o_ref = refs
    row = lax.broadcasted_iota(jnp.int32, (tm, 1), 0)
    pos = jnp.where(is_lat, (r0 + row) % seq, (r0 - n_lat + row) % ctx_len)
    seq_len = jnp.where(is_lat, seq, ctx_len)
    first = pos == 0
    last = pos == seq_len - 1

    p = p_ref[...].astype(F32)
    prev_row = pprev_ref[...].astype(F32)[BF16_SUBLANES - 1:BF16_SUBLANES, :]
    next_row = pnext_ref[...].astype(F32)[0:1, :]
    p_before = jnp.where(first, 0.0, jnp.where(row == 0, prev_row, pltpu.roll(p, 1, 0)))
    p_after = jnp.where(last, 0.0, jnp.where(row == tm - 1, next_row, pltpu.roll(p, tm - 1, 0)))
    cw = convw_ref[...]
    conv = cw[0:1, :] * p_before + cw[1:2, :] * p + cw[2:3, :] * p_after
    y_sc = (sb_ref[...].astype(F32) * conv).astype(BF16)
    y_ml = smo_ref[...] * hml
    y = (gs_ref[...].astype(F32) * _dot(y_sc, wsc_ref[...].astype(BF16))
         + gm_ref[...].astype(F32) * _dot(y_ml, wml_ref[...].astype(BF16))
         + gd_ref[...].astype(F32) * _dot(yda, wda_ref[...].astype(BF16)))
    y = _dot(y.astype(BF16), wo_ref[...].astype(BF16))
    h = h_ref[...]
    o_ref[...] = _layer_norm(alpha * h + mod_ref[0, 5:6, :] * y, lng_ref[...], lnb_ref[...])


def _mixer_out(h, mods3, sb, p, smo, hml, yda, gs, gm, gd, conv_w, w_sc, w_ml, w_da, w_o, ln_g, ln_b,
               *, layer, n_rows, seq, ctx_len, batch, n_lat, alpha, hml_ctx=None, yda_ctx=None):
    d = h.shape[1]
    tm = math.gcd(math.gcd(MIXER_OUT_TILE, seq), batch * ctx_len)
    halo = BF16_SUBLANES
    last_halo = p.shape[0] // halo - 1
    two_source = hml_ctx is not None

    def row_spec(width):
        return pl.BlockSpec((tm, width), lambda i: (i, 0))

    if two_source:
        branch = [hml, hml_ctx, yda, yda_ctx]
        branch_specs = (_two_source_specs(tm, ML_V_W, n_lat // tm) + _two_source_specs(tm, DA_V_W, n_lat // tm))
    else:
        branch = [hml, yda]
        branch_specs = [row_spec(ML_V_W), row_spec(DA_V_W)]
    kernel = functools.partial(_mixer_out_kernel, seq=seq, ctx_len=ctx_len, n_lat=n_lat, alpha=alpha,
                               two_source=two_source)
    return pl.pallas_call(
        kernel,
        grid=(n_rows // tm,),
        in_specs=[row_spec(d),
                  pl.BlockSpec((1, N_MOD, d), lambda i: (jnp.minimum(i * tm // seq, batch), 0, 0)),
                  row_spec(SC_WIDTH), row_spec(SC_WIDTH),
                  pl.BlockSpec((halo, SC_WIDTH), lambda i: (jnp.maximum(i * (tm // halo) - 1, 0), 0)),
                  pl.BlockSpec((halo, SC_WIDTH),
                               lambda i: (jnp.minimum((i + 1) * (tm // halo), last_halo), 0)),
                  row_spec(ML_V_W)] + branch_specs + [
                  row_spec(d), row_spec(d), row_spec(d),
                  _resident((SC_KSIZE, SC_WIDTH)),
                  _resident_layer(w_sc, layer), _resident_layer(w_ml, layer), _resident_layer(w_da, layer),
                  _resident_layer(w_o, layer),
                  pl.BlockSpec((1, d), lambda i: (0, 0)),
                  pl.BlockSpec((1, d), lambda i: (0, 0))],
        out_specs=row_spec(d),
        out_shape=jax.ShapeDtypeStruct((n_rows, d), F32),
        compiler_params=_params(1),
        name="mixer_out",
    )(h, mods3, sb, p, p, p, smo, *branch, gs, gm, gd, conv_w, w_sc, w_ml, w_da, w_o,
      ln_g.reshape(1, d), ln_b.reshape(1, d))


def _rope_tables(seq, tile):
    n_freq = DA_HALF // 4
    t = jnp.arange(seq)
    row_ids = (t // GRID_W).astype(F32)
    col_ids = (t % GRID_W).astype(F32)
    inv = ROPE_BASE ** (-jnp.arange(n_freq, dtype=F32) / n_freq)
    ang = jnp.concatenate([row_ids[:, None] * inv, col_ids[:, None] * inv], axis=-1)
    cos, sin = jnp.cos(ang), jnp.sin(ang)
    reps = LANES // DA_HALF
    cos_t = jnp.tile(jnp.concatenate([cos, cos], axis=-1), (1, reps))
    sin_t = jnp.tile(jnp.concatenate([-sin, sin], axis=-1), (1, reps))
    cos_t = jnp.concatenate([cos_t, jnp.ones((tile, LANES), F32)], axis=0)
    sin_t = jnp.concatenate([sin_t, jnp.zeros((tile, LANES), F32)], axis=0)
    return cos_t, sin_t


def kernel(x, c, ctx, c_ctx, w_ada, b_ada, ln_g, ln_b, ffn1_up, ffn1_down, ffn2_up, ffn2_down, w_in, b_in,
           conv_w, w_sc, w_ml, w_da, w_o, lam_q1, lam_k1, lam_q2, lam_k2, da_norm_g):
    batch, seq, d = x.shape
    ctx_len = ctx.shape[1]
    depth = w_ada.shape[0]
    alpha = (2 * depth) ** 0.25
    n_lat = batch * seq
    n_ctx = batch * ctx_len

    n_cond = -(-(batch + 1) // 8) * 8
    cc = jnp.concatenate([c, c_ctx[None, :], jnp.zeros((n_cond - batch - 1, d), F32)], axis=0)
    mods = _ada(cc, w_ada, b_ada).reshape(depth, n_cond, N_MOD, d)

    cos_t, sin_t = _rope_tables(seq, min(MIXER_IN_TILE, seq))
    gate_lo = GATE_COL0
    gate_hi = gate_lo + N_GATES
    w_in_t = jnp.swapaxes(w_in, 1, 2)

    h, h_ctx = x.reshape(n_lat, d), ctx.reshape(n_ctx, d)
    for l in range(depth):
        last = l == depth - 1
        lam_init = 0.8 - 0.6 * math.exp(-0.3 * l)
        mods3 = mods[l]
        b_head = b_in[l, :gate_lo][None, :]
        b_tail = b_in[l, gate_hi:][None, :]
        b_gate = b_in[l, gate_lo:gate_hi][None, :]
        lam_vecs = jnp.stack([lam_q1[l], lam_k1[l], lam_q2[l], lam_k2[l]]).astype(F32)

        h, w_proj_t = _ffn(h, mods3, ffn1_up, ffn1_down, ln_g[l, 0], ln_b[l, 0], layer=l, mod_base=0,
                           n_rows=n_lat + n_ctx, seq=seq, batch=batch, alpha=alpha, h_ctx=h_ctx,
                           cast_stack=w_in_t)
        h_ctx = None

        (sb, p, mq, mk, mv, smo, mg, dq, dk, dv, gs, gm, gd) = _mixer_in(
            h, mods3, w_proj_t, b_head, b_tail, b_gate, cos_t, sin_t,
            seq=seq, batch=batch, n_lat=n_lat)

        hml, hml_ctx = _mlstm_scan(mq, mk, mv, mg, seq=seq, ctx_len=ctx_len, batch=batch, n_lat=n_lat,
                                   need_ctx=not last)
        yda = _diffattn(dq, dk, dv, lam_vecs, da_norm_g[l], q_row0=0, q_len=seq,
                        segments=((0, seq), (n_lat, ctx_len)), batch=batch, lam_init=lam_init)
        yda_ctx = None
        if not last:
            yda_ctx = _diffattn(dq, dk, dv, lam_vecs, da_norm_g[l], q_row0=n_lat, q_len=ctx_len,
                                segments=((n_lat, ctx_len),), batch=batch, lam_init=lam_init,
                                heads_per_step=DA_HEADS)

        n_rows = n_lat if last else n_lat + n_ctx
        h = _mixer_out(h, mods3, sb, p, smo, hml, yda, gs, gm, gd, conv_w[l], w_sc, w_ml, w_da, w_o,
                       ln_g[l, 1], ln_b[l, 1], hml_ctx=hml_ctx, yda_ctx=yda_ctx, layer=l,
                       n_rows=n_rows, seq=seq, ctx_len=ctx_len, batch=batch, n_lat=n_lat, alpha=alpha)
        h = _ffn(h, mods3, ffn2_up, ffn2_down, ln_g[l, 2], ln_b[l, 2], layer=l,
                 mod_base=6, n_rows=n_rows, seq=seq, batch=batch, alpha=alpha)
    return h[:n_lat].reshape(batch, seq, d)
```

```python
import functools
import math

import jax
import jax.numpy as jnp
from jax import lax
from jax.experimental import pallas as pl
from jax.experimental.pallas import tpu as pltpu

GRID_W = 64
N_MOD = 9
SC_WIDTH = 512
SC_KSIZE = 3
ML_HEADS = 4
ML_DQK = 128
ML_DV = 256
DA_HEADS = 4
DA_HALF = 64
DA_DV = 2 * DA_HALF
ROPE_BASE = 10000.0
LN_EPS = 1e-5

ML_QK_W = ML_HEADS * ML_DQK
ML_V_W = ML_HEADS * ML_DV
DA_QK_W = DA_HEADS * 2 * DA_HALF
DA_V_W = DA_HEADS * DA_DV
N_GATES = 4 * ML_HEADS

ML_BLOCK = 256
LANES = 128
BF16_SUBLANES = 16
V7X_VMEM_BYTES = 64 * 1024 * 1024
VMEM_LIMIT = V7X_VMEM_BYTES - 8 * 1024 * 1024

FFN_TILE = 512
FFN_COLS = 256
MIXER_IN_TILE = 256
MIXER_OUT_TILE = 512
DIFFATTN_TILE = 1024

F32 = jnp.float32
BF16 = jnp.bfloat16
NEG_BIG = -1e30
LOG2_E = 1.4426950408889634

assert ML_DQK == LANES and DA_DV == LANES and ML_BLOCK == 2 * LANES


def _dot(a, b):
    return jnp.dot(a, b, preferred_element_type=F32)


def _dot_nt(a, b):
    return lax.dot_general(a, b, (((1,), (1,)), ((), ())), preferred_element_type=F32)


def _wide(x, width):
    return jnp.concatenate([x] * (width // LANES), axis=1)


def _layer_norm(y, g, b):
    mu = jnp.mean(y, axis=-1, keepdims=True)
    yc = y - mu
    var = jnp.mean(yc * yc, axis=-1, keepdims=True)
    return yc * lax.rsqrt(var + LN_EPS) * g + b


def _resident(shape):
    return pl.BlockSpec(shape, lambda *_: (0,) * len(shape), pipeline_mode=pl.Buffered(1))


def _resident_layer(stacked, layer):
    n = stacked.ndim - 1
    return pl.BlockSpec((None,) + stacked.shape[1:], lambda *_: (layer,) + (0,) * n,
                        pipeline_mode=pl.Buffered(1))


def _params(n_axes):
    return pltpu.CompilerParams(dimension_semantics=("parallel",) * n_axes,
                                vmem_limit_bytes=VMEM_LIMIT)


def _ada_kernel(c_ref, w_ref, b_ref, o_ref):
    c = c_ref[...]
    a = (c * jax.nn.sigmoid(c)).astype(BF16)
    o_ref[0] = _dot(a, w_ref[0].astype(BF16)) + b_ref[0]


def _ada(cc, w_ada, b_ada):
    depth, d, n = w_ada.shape
    tn = d
    return pl.pallas_call(
        _ada_kernel,
        grid=(depth, n // tn),
        in_specs=[pl.BlockSpec(cc.shape, lambda l, j: (0, 0)),
                  pl.BlockSpec((1, d, tn), lambda l, j: (l, 0, j)),
                  pl.BlockSpec((1, 1, tn), lambda l, j: (l, 0, j))],
        out_specs=pl.BlockSpec((1, cc.shape[0], tn), lambda l, j: (l, 0, j)),
        out_shape=jax.ShapeDtypeStruct((depth, cc.shape[0], n), F32),
        compiler_params=_params(2),
        name="ada",
    )(cc, w_ada, b_ada.reshape(depth, 1, n))


def _two_source_specs(tm, width, lat_tiles):
    return [pl.BlockSpec((tm, width), lambda i: (jnp.minimum(i, lat_tiles - 1), 0)),
            pl.BlockSpec((tm, width), lambda i: (jnp.maximum(i - lat_tiles, 0), 0))]


def _ffn_kernel(*refs, mod_base, alpha, lat_tiles, cast_extra):
    if lat_tiles is None:
        h_ref, hc_ref = refs[0], None
        refs = refs[1:]
    else:
        h_ref, hc_ref = refs[:2]
        refs = refs[2:]
    if cast_extra:
        mod_ref, wup_ref, wdn_ref, lng_ref, lnb_ref, extra_ref, o_ref, extra_o, g_scr = refs
        extra_o[...] = extra_ref[...].astype(BF16)
    else:
        mod_ref, wup_ref, wdn_ref, lng_ref, lnb_ref, o_ref, g_scr = refs
    h = h_ref[...] if hc_ref is None else jnp.where(pl.program_id(0) < lat_tiles, h_ref[...], hc_ref[...])
    f = wdn_ref.shape[0]
    shift = mod_ref[0, mod_base:mod_base + 1, :]
    scale = mod_ref[0, mod_base + 1:mod_base + 2, :]
    gate = mod_ref[0, mod_base + 2:mod_base + 3, :]
    u = (h * (1.0 + scale) + shift).astype(BF16)
    for c in range(f // FFN_COLS):
        lo, hi = c * FFN_COLS, (c + 1) * FFN_COLS
        a = _dot(u, wup_ref[:, lo:hi].astype(BF16))
        v = _dot(u, wup_ref[:, f + lo:f + hi].astype(BF16))
        g_scr[:, lo:hi] = (a * jax.nn.sigmoid(a) * v).astype(BF16)
    half = h.shape[0] // 2
    for r in range(2):
        rows = slice(r * half, (r + 1) * half)
        d = _dot(g_scr[rows, :], wdn_ref[...].astype(BF16))
        o_ref[rows, :] = _layer_norm(alpha * h[rows, :] + (0.5 * gate) * d, lng_ref[...], lnb_ref[...])


def _ffn(h, mods3, w_up, w_dn, ln_g, ln_b, *, layer, mod_base, n_rows, seq, batch, alpha, h_ctx=None,
         cast_stack=None):
    d = h.shape[1]
    f = w_dn.shape[1]
    tm = min(FFN_TILE, seq)
    steps = n_rows // tm
    lat_tiles = None if h_ctx is None else h.shape[0] // tm
    kernel = functools.partial(_ffn_kernel, mod_base=mod_base, alpha=alpha, lat_tiles=lat_tiles,
                               cast_extra=cast_stack is not None)
    if h_ctx is None:
        sources, source_specs = [h], [pl.BlockSpec((tm, d), lambda i: (i, 0))]
    else:
        sources, source_specs = [h, h_ctx], _two_source_specs(tm, d, lat_tiles)
    extra, extra_specs = [], []
    out_specs = [pl.BlockSpec((tm, d), lambda i: (i, 0))]
    out_shape = [jax.ShapeDtypeStruct((n_rows, d), F32)]
    if cast_stack is not None:
        n_extra = cast_stack.shape[1]
        rows_blk = -(-pl.cdiv(n_extra, steps) // BF16_SUBLANES) * BF16_SUBLANES
        last_blk = pl.cdiv(n_extra, rows_blk) - 1
        extra = [cast_stack]
        extra_specs = [pl.BlockSpec((None, rows_blk, d), lambda i: (layer, jnp.minimum(i, last_blk), 0))]
        out_specs.append(pl.BlockSpec((rows_blk, d), lambda i: (jnp.minimum(i, last_blk), 0)))
        out_shape.append(jax.ShapeDtypeStruct((n_extra, d), BF16))
    out = pl.pallas_call(
        kernel,
        grid=(steps,),
        in_specs=source_specs + [
                  pl.BlockSpec((1, N_MOD, d), lambda i: (jnp.minimum(i * tm // seq, batch), 0, 0)),
                  _resident_layer(w_up, layer),
                  _resident_layer(w_dn, layer),
                  pl.BlockSpec((1, d), lambda i: (0, 0)),
                  pl.BlockSpec((1, d), lambda i: (0, 0))] + extra_specs,
        out_specs=out_specs,
        out_shape=out_shape,
        scratch_shapes=[pltpu.VMEM((tm, f), BF16)],
        compiler_params=pltpu.CompilerParams(
            dimension_semantics=("arbitrary" if cast_stack is not None else "parallel",),
            vmem_limit_bytes=VMEM_LIMIT),
        name="ffn",
    )(*sources, mods3, w_up, w_dn, ln_g.reshape(1, d), ln_b.reshape(1, d), *extra)
    return out if cast_stack is not None else out[0]


GATE_COL0 = 3 * SC_WIDTH + 2 * ML_QK_W + 2 * ML_V_W


def _proj_cols(d):
    parts = ((("sb", SC_WIDTH), ("sc", SC_WIDTH), ("sx", SC_WIDTH),
              ("mq", ML_QK_W), ("mk", ML_QK_W), ("mv", ML_V_W), ("mo", ML_V_W)),
             (("dq", DA_QK_W), ("dk", DA_QK_W), ("dv", DA_V_W), ("gs", d), ("gm", d), ("gd", d)))
    cols, widths = {}, []
    for part, sizes in enumerate(parts):
        off = 0
        for name, w in sizes:
            cols[name] = (part, off, off + w)
            off += w
        widths.append(off)
    return cols, widths


def _mixer_in_kernel(h_ref, mod_ref, w_ref, bh_ref, bt_ref, bg_ref, cos_ref, sin_ref,
                     sb_o, p_o, mq_o, mk_o, mv_o, smo_o, mg_o, dq_o, dk_o, dv_o, gs_o, gm_o, gd_o):
    d = h_ref.shape[1]
    tm = h_ref.shape[0]
    cols, _ = _proj_cols(d)
    h = h_ref[...]
    shift = mod_ref[0, 3:4, :]
    scale = mod_ref[0, 4:5, :]
    u = (h * (1.0 + scale) + shift).astype(BF16)

    def proj(name):
        part, lo, hi = cols[name]
        b_ref = (bh_ref, bt_ref)[part]
        row0 = 0 if part == 0 else GATE_COL0 + N_GATES
        return _dot_nt(u, w_ref[row0 + lo:row0 + hi, :]) + b_ref[:, lo:hi]

    sb_o[...] = proj("sb").astype(BF16)
    p_o[...] = (proj("sc") * proj("sx")).astype(BF16)
    mq_o[...] = proj("mq").astype(BF16)
    mk_o[...] = (proj("mk") * (ML_DQK ** -0.5)).astype(BF16)
    mv_o[...] = proj("mv").astype(BF16)
    smo_o[...] = jax.nn.sigmoid(proj("mo")).astype(BF16)
    mg_o[...] = _dot_nt(u, w_ref[GATE_COL0:GATE_COL0 + N_GATES, :]) + bg_ref[...]

    cos_t = cos_ref[...]
    sin_t = sin_ref[...]
    lane = lax.broadcasted_iota(jnp.int32, cos_t.shape, 1)
    first_half = (lane % DA_HALF) < (DA_HALF // 2)

    def rope_store(z, out_ref, mult):
        for k in range(z.shape[1] // LANES):
            x = z[:, k * LANES:(k + 1) * LANES]
            partner = jnp.where(first_half,
                                pltpu.roll(x, LANES - DA_HALF // 2, 1),
                                pltpu.roll(x, DA_HALF // 2, 1))
            out_ref[:, k * LANES:(k + 1) * LANES] = ((x * cos_t + partner * sin_t) * mult).astype(BF16)

    rope_store(proj("dq"), dq_o, (DA_HALF ** -0.5) * LOG2_E)
    rope_store(proj("dk"), dk_o, 1.0)
    dv = proj("dv").astype(BF16)
    ones = jnp.ones((tm, DA_DV), BF16)
    for k in range(DA_HEADS):
        dv_o[:, 2 * k * DA_DV:(2 * k + 1) * DA_DV] = dv[:, k * DA_DV:(k + 1) * DA_DV]
        dv_o[:, (2 * k + 1) * DA_DV:(2 * k + 2) * DA_DV] = ones
    gs_o[...] = jax.nn.sigmoid(proj("gs")).astype(BF16)
    gm_o[...] = jax.nn.sigmoid(proj("gm")).astype(BF16)
    gd_o[...] = jax.nn.sigmoid(proj("gd")).astype(BF16)


def _mixer_in(h, mods3, w_t, b_head, b_tail, b_gate, cos_t, sin_t, *, seq, batch, n_lat):
    rows, d = h.shape
    _, (n_head, n_tail) = _proj_cols(d)
    tm = min(MIXER_IN_TILE, seq)
    lat_tiles = n_lat // tm
    rope_blocks = seq // tm

    def row_spec(width):
        return pl.BlockSpec((tm, width), lambda i: (i, 0))

    def rope_map(i):
        return (jnp.where(i < lat_tiles, i % rope_blocks, rope_blocks), 0)

    widths = (SC_WIDTH, SC_WIDTH, ML_QK_W, ML_QK_W, ML_V_W, ML_V_W, N_GATES,
              DA_QK_W, DA_QK_W, 2 * DA_V_W, d, d, d)
    dtypes = (BF16,) * 6 + (F32,) + (BF16,) * 6
    return pl.pallas_call(
        _mixer_in_kernel,
        grid=(rows // tm,),
        in_specs=[row_spec(d),
                  pl.BlockSpec((1, N_MOD, d), lambda i: (jnp.minimum(i * tm // seq, batch), 0, 0)),
                  _resident(w_t.shape),
                  _resident((1, n_head)),
                  _resident((1, n_tail)),
                  _resident((1, N_GATES)),
                  pl.BlockSpec((tm, LANES), rope_map),
                  pl.BlockSpec((tm, LANES), rope_map)],
        out_specs=[row_spec(w) for w in widths],
        out_shape=[jax.ShapeDtypeStruct((rows, w), dt) for w, dt in zip(widths, dtypes)],
        compiler_params=_params(1),
        name="mixer_in",
    )(h, mods3, w_t, b_head, b_tail, b_gate, cos_t, sin_t)


def _log_sigmoid(x):
    return jnp.minimum(x, 0.0) - jnp.log(1.0 + jnp.exp(-jnp.abs(x)))


def _split3(x):
    pieces, rest = [], x
    for _ in range(3):
        piece = rest.astype(BF16)
        pieces.append(piece)
        rest = rest - piece.astype(F32)
    return pieces


def _mlstm_gate_kernel(g_ref, o_ref, o3_ref):
    blk = ML_BLOCK
    t_idx = lax.broadcasted_iota(jnp.int32, (blk, blk), 0)
    s_idx = lax.broadcasted_iota(jnp.int32, (blk, blk), 1)
    tril = (s_idx <= t_idx).astype(BF16)
    lane = lax.broadcasted_iota(jnp.int32, (blk, N_GATES), 1)
    src = lax.broadcasted_iota(jnp.int32, (N_GATES, LANES), 0)
    dst = lax.broadcasted_iota(jnp.int32, (N_GATES, LANES), 1)
    places = [(dst == src + term * N_GATES).astype(BF16) for term in range(3)]
    for j in range(g_ref.shape[0] // blk):
        sl = pl.ds(j * blk, blk)
        g = g_ref[sl, :]
        ls = _log_sigmoid(g)
        prefix = functools.reduce(jnp.add, [_dot(tril, piece) for piece in _split3(ls)])
        suffix = prefix[blk - 1:blk, :] - prefix + ls
        out = jnp.where(lane < 2 * ML_HEADS, g, jnp.where(lane < 3 * ML_HEADS, prefix, suffix))
        o_ref[sl, :] = out
        placed = functools.reduce(jnp.add, [_dot(piece, place) for piece, place in zip(_split3(out), places)])
        o3_ref[sl, :] = placed.astype(BF16)


def _mlstm_gates(mg):
    rows = mg.shape[0]
    step = math.gcd(rows, 8 * ML_BLOCK)
    return pl.pallas_call(
        _mlstm_gate_kernel,
        grid=(rows // step,),
        in_specs=[pl.BlockSpec((step, N_GATES), lambda i: (i, 0))],
        out_specs=[pl.BlockSpec((step, N_GATES), lambda i: (i, 0)),
                   pl.BlockSpec((step, LANES), lambda i: (i, 0))],
        out_shape=[jax.ShapeDtypeStruct((rows, N_GATES), F32),
                   jax.ShapeDtypeStruct((rows, LANES), BF16)],
        compiler_params=_params(1),
        name="mlstm_gates",
    )(mg)


ML_HEADS_PER_STEP = 4
N_ROWS_PAD = BF16_SUBLANES
ST_ROWS = 2 * N_ROWS_PAD + 2 * ML_DV


def _mlstm_scan_kernel(*refs, need_ctx, heads_per_step):
    ins, rest = refs[:12], refs[12:]
    if need_ctx:
        hl_ref, hc_ref, st_in, m_in = rest
    else:
        hl_ref, st_in, m_in = rest
        hc_ref = None
    ql, qc, kl, kc, ktl, ktc, vtl, vtc, g3l, g3c, grl, grc = ins
    for hh in range(heads_per_step):
        qk_cols = pl.ds(hh * ML_DQK, ML_DQK)
        v_cols = pl.ds(hh * ML_DV, ML_DV)
        _mlstm_head(pl.program_id(1) * heads_per_step + hh,
                    ql.at[:, qk_cols], qc.at[:, qk_cols], kl.at[:, qk_cols], kc.at[:, qk_cols],
                    ktl.at[:, qk_cols, :], ktc.at[:, qk_cols, :], vtl.at[:, v_cols, :], vtc.at[:, v_cols, :],
                    g3l, g3c, grl, grc, hl_ref.at[:, v_cols],
                    None if hc_ref is None else hc_ref.at[:, v_cols], st_in.at[hh], m_in.at[hh])


def _mlstm_head(head, ql_ref, qc_ref, kl_ref, kc_ref, ktl_ref, ktc_ref, vtl_ref, vtc_ref,
                g3l_ref, g3c_ref, grl_ref, grc_ref, hl_ref, hc_ref, st_in, m_in):
    need_ctx = hc_ref is not None
    blk = ML_BLOCK
    n_lat_blk = ql_ref.shape[0] // blk
    n_ctx_blk = qc_ref.shape[0] // blk

    s_idx = lax.broadcasted_iota(jnp.int32, (blk, blk), 0)
    t_idx = lax.broadcasted_iota(jnp.int32, (blk, blk), 1)
    lane_t = lax.broadcasted_iota(jnp.int32, (1, blk), 1)
    ones_n = jnp.ones((N_ROWS_PAD, blk), BF16)
    sel_src = lax.broadcasted_iota(jnp.int32, (LANES, 2 * LANES), 0)
    sel_dst = lax.broadcasted_iota(jnp.int32, (LANES, 2 * LANES), 1)
    sel_bwd = sel_dst >= LANES
    col = sel_src % N_GATES
    in_terms = sel_src < 3 * N_GATES
    plus = jnp.logical_and(col == head + jnp.where(sel_bwd, ML_HEADS, 0), in_terms)
    minus = jnp.logical_and(col == head + jnp.where(sel_bwd, 3 * ML_HEADS, 2 * ML_HEADS), in_terms)
    sel = (plus.astype(F32) - minus.astype(F32)).astype(BF16)

    def gate_rows(gr_ref, j, bwd):
        i_idx = head + (ML_HEADS if bwd else 0)
        f_idx = head + (3 * ML_HEADS if bwd else 2 * ML_HEADS)
        return gr_ref[j, pl.ds(f_idx, 1), :], gr_ref[j, pl.ds(i_idx, 1), :]

    def scan_states():
        ctx_blocks = [(grc_ref, ktc_ref, vtc_ref, j, j) for j in range(n_ctx_blk)]
        lat_blocks = [(grl_ref, ktl_ref, vtl_ref, j, n_ctx_blk + j) for j in range(n_lat_blk)]
        orders = (ctx_blocks + lat_blocks, ctx_blocks[::-1] + lat_blocks[::-1])
        m_prev = [jnp.zeros((1, 1), F32)] * 2
        ct_run = [jnp.zeros((ML_DV, ML_DQK), F32)] * 2
        n_run = [jnp.zeros((N_ROWS_PAD, ML_DQK), F32)] * 2
        for step in range(len(orders[0])):
            for d, bwd in enumerate((False, True)):
                gr_ref, kt_ref, vt_ref, j, slot = orders[d][step]
                c_row0 = 2 * N_ROWS_PAD + d * ML_DV
                f_row, i_row = gate_rows(gr_ref, j, bwd)
                b_end = jnp.sum(jnp.where(lane_t == (0 if bwd else blk - 1), f_row, 0.0), axis=1,
                                keepdims=True)
                g_log = b_end + (i_row - f_row)
                m_new = jnp.maximum(b_end + m_prev[d], jnp.max(g_log, axis=1, keepdims=True))
                a_prev = jnp.exp(b_end + m_prev[d] - m_new)
                kw = (kt_ref[j].astype(F32) * jnp.exp(g_log - m_new)).astype(BF16)
                st_in[slot, d * N_ROWS_PAD:(d + 1) * N_ROWS_PAD, :] = n_run[d].astype(BF16)
                st_in[slot, c_row0:c_row0 + ML_DV, :] = ct_run[d].astype(BF16)
                m_in[slot, :, d * LANES:(d + 1) * LANES] = jnp.broadcast_to(m_prev[d], (8, LANES))
                ct_run[d] = a_prev * ct_run[d] + _dot_nt(vt_ref[j], kw)
                n_run[d] = a_prev * n_run[d] + _dot_nt(ones_n, kw)
                m_prev[d] = m_new

    def outputs(q, k, vt, g3, rows, slot):
        qk_t = _dot_nt(k, q)
        state = _dot_nt(st_in[slot], q)
        r_both = _dot(g3, sel)
        m_prev_both = m_in[slot][0:1, :]
        scaled, inter = [], []
        for d, bwd in enumerate((False, True)):
            f_row, _ = rows[d]
            r_rep = r_both[:, d * LANES:(d + 1) * LANES]
            mask = (s_idx >= t_idx) if bwd else (s_idx <= t_idx)
            d_log = jnp.where(mask, _wide(r_rep, blk) + f_row, NEG_BIG)
            m_loc = jnp.max(d_log, axis=0, keepdims=True)
            s = qk_t * jnp.exp(d_log - m_loc)
            den_loc = jnp.sum(s, axis=0, keepdims=True)
            m_inter = f_row + m_prev_both[:, d * LANES:d * LANES + 1]
            m_t = jnp.maximum(m_inter, m_loc)
            w_inter = jnp.exp(m_inter - m_t)
            w_loc = jnp.exp(m_loc - m_t)
            qn = state[d * N_ROWS_PAD:d * N_ROWS_PAD + 1, :]
            den = w_inter * qn + w_loc * den_loc
            inv = 1.0 / jnp.maximum(jnp.abs(den), jnp.exp(-m_t))
            scaled.append(s * (w_loc * inv))
            c_row0 = 2 * N_ROWS_PAD + d * ML_DV
            inter.append(state[c_row0:c_row0 + ML_DV, :] * (w_inter * inv))
        return _dot(vt, (scaled[0] + scaled[1]).astype(BF16)) + inter[0] + inter[1]

    def lat_rows(j):
        return pl.ds(pl.multiple_of(j * blk, blk), blk)

    scan_states()

    if need_ctx:
        for j in range(n_ctx_blk):
            sl = pl.ds(j * blk, blk)
            rows = (gate_rows(grc_ref, j, False), gate_rows(grc_ref, j, True))
            out_t = outputs(qc_ref[sl, :], kc_ref[sl, :], vtc_ref[j], g3c_ref[sl, :], rows, j)
            hc_ref[sl, :] = out_t.T.astype(hc_ref.dtype)

    def out_body(j, carry):
        sl = lat_rows(j)
        rows = (gate_rows(grl_ref, j, False), gate_rows(grl_ref, j, True))
        out_t = outputs(ql_ref[sl, :], kl_ref[sl, :], vtl_ref[j], g3l_ref[sl, :], rows, n_ctx_blk + j)
        hl_ref[sl, :] = out_t.T.astype(hl_ref.dtype)
        return carry

    lax.fori_loop(0, n_lat_blk, out_body, 0, unroll=True)


def _mlstm_scan(mq, mk, mv, mg, *, seq, ctx_len, batch, n_lat, need_ctx):
    blk = ML_BLOCK
    rows = mq.shape[0]
    ctx0 = n_lat // ctx_len
    n_blk = (seq + ctx_len) // blk
    gates, gates3 = _mlstm_gates(mg)
    gates_rows = gates.T.reshape(N_GATES, rows // blk, blk).transpose(1, 0, 2)
    mk_t = mk.reshape(rows // blk, blk, ML_QK_W).transpose(0, 2, 1)
    mv_t = mv.reshape(rows // blk, blk, ML_V_W).transpose(0, 2, 1)

    hps = ML_HEADS_PER_STEP

    def lat(width):
        return pl.BlockSpec((seq, hps * width), lambda b, h: (b, h))

    def ctx(width):
        return pl.BlockSpec((ctx_len, hps * width), lambda b, h: (ctx0 + b, h))

    def lat_t(width):
        return pl.BlockSpec((seq // blk, hps * width, blk), lambda b, h: (b, h, 0))

    def ctx_t(width):
        return pl.BlockSpec((ctx_len // blk, hps * width, blk), lambda b, h: (ctx0 + b, h, 0))

    in_specs = [lat(ML_DQK), ctx(ML_DQK), lat(ML_DQK), ctx(ML_DQK),
                lat_t(ML_DQK), ctx_t(ML_DQK), lat_t(ML_DV), ctx_t(ML_DV),
                pl.BlockSpec((seq, LANES), lambda b, h: (b, 0)),
                pl.BlockSpec((ctx_len, LANES), lambda b, h: (ctx0 + b, 0)),
                pl.BlockSpec((seq // blk, N_GATES, blk), lambda b, h: (b, 0, 0)),
                pl.BlockSpec((ctx_len // blk, N_GATES, blk), lambda b, h: (ctx0 + b, 0, 0))]
    out_specs = [pl.BlockSpec((seq, hps * ML_DV), lambda b, h: (b, h))]
    out_shape = [jax.ShapeDtypeStruct((n_lat, ML_V_W), BF16)]
    if need_ctx:
        out_specs.append(pl.BlockSpec((ctx_len, hps * ML_DV), lambda b, h: (b, h)))
        out_shape.append(jax.ShapeDtypeStruct((batch * ctx_len, ML_V_W), BF16))
    scratch = [pltpu.VMEM((hps, n_blk, ST_ROWS, ML_DQK), BF16),
               pltpu.VMEM((hps, n_blk, 8, 2 * LANES), F32)]
    out = pl.pallas_call(
        functools.partial(_mlstm_scan_kernel, need_ctx=need_ctx, heads_per_step=hps),
        grid=(batch, ML_HEADS // hps),
        in_specs=in_specs,
        out_specs=out_specs,
        out_shape=out_shape,
        scratch_shapes=scratch,
        compiler_params=_params(2),
        name="mlstm",
    )(mq, mq, mk, mk, mk_t, mk_t, mv_t, mv_t, gates3, gates3, gates_rows, gates_rows)
    return out if need_ctx else (out[0], None)


def _diffattn_kernel(lam_ref, g_ref, q_ref, *refs, n_seg, lam_init):
    k_refs = refs[:n_seg]
    v_refs = refs[n_seg:2 * n_seg]
    o_ref = refs[2 * n_seg]
    lv = lam_ref[...]
    lam = (jnp.exp(jnp.sum(lv[0:1] * lv[1:2], axis=1, keepdims=True))
           - jnp.exp(jnp.sum(lv[2:3] * lv[3:4], axis=1, keepdims=True)) + lam_init)
    for hh in range(q_ref.shape[1] // LANES):
        q = q_ref[:, hh * LANES:(hh + 1) * LANES]
        ks = [k_ref[:, hh * LANES:(hh + 1) * LANES] for k_ref in k_refs]
        vs = [v_ref[:, 2 * hh * DA_DV:2 * (hh + 1) * DA_DV] for v_ref in v_refs]
        lane = lax.broadcasted_iota(jnp.int32, q.shape, 1)
        zero = jnp.zeros_like(q)
        halves = [jnp.where(lane < DA_HALF, q, zero), jnp.where(lane >= DA_HALF, q, zero)]
        scores = [[_dot_nt(qh, k) for k in ks] for qh in halves]
        maxes = [functools.reduce(jnp.maximum, [jnp.max(s, axis=1, keepdims=True) for s in ss])
                 for ss in scores]
        probs = [[jnp.exp2((s - m).astype(BF16)) for s in ss] for ss, m in zip(scores, maxes)]
        heads = []
        for ps in probs:
            acc = None
            for p, v in zip(ps, vs):
                part = _dot(p, v)
                acc = part if acc is None else acc + part
            heads.append(acc[:, :DA_DV] * (1.0 / acc[:, DA_DV:]))
        o = heads[0] - lam * heads[1]
        ms = jnp.mean(o * o, axis=1, keepdims=True)
        o_ref[:, hh * DA_DV:(hh + 1) * DA_DV] = (o * lax.rsqrt(ms + LN_EPS) * g_ref[...]
                                                 * (1.0 - lam_init)).astype(o_ref.dtype)


def _diffattn(dq, dk, dv, lam_vecs, da_g, *, q_row0, q_len, segments, batch, lam_init, heads_per_step=1):
    tq = min(DIFFATTN_TILE, q_len)
    nq = q_len // tq
    q0 = q_row0 // tq
    hps = heads_per_step

    def seg_spec(row0, length, width):
        return pl.BlockSpec((length, hps * width), lambda b, h, i: (row0 // length + b, h))

    return pl.pallas_call(
        functools.partial(_diffattn_kernel, n_seg=len(segments), lam_init=lam_init),
        grid=(batch, DA_HEADS // hps, nq),
        in_specs=[pl.BlockSpec(lam_vecs.shape, lambda b, h, i: (0, 0)),
                  pl.BlockSpec((1, DA_DV), lambda b, h, i: (0, 0)),
                  pl.BlockSpec((tq, hps * LANES), lambda b, h, i: (q0 + b * nq + i, h))]
                 + [seg_spec(r0, ln, LANES) for r0, ln in segments]
                 + [seg_spec(r0, ln, 2 * DA_DV) for r0, ln in segments],
        out_specs=pl.BlockSpec((tq, hps * DA_DV), lambda b, h, i: (b * nq + i, h)),
        out_shape=jax.ShapeDtypeStruct((batch * q_len, DA_V_W), BF16),
        compiler_params=_params(3),
        name="diffattn",
    )(lam_vecs, da_g.reshape(1, DA_DV), dq, *([dk] * len(segments)), *([dv] * len(segments)))


def _mixer_out_kernel(h_ref, mod_ref, sb_ref, p_ref, pprev_ref, pnext_ref, smo_ref, *refs,
                      seq, ctx_len, n_lat, alpha, two_source):
    tm = h_ref.shape[0]
    r0 = pl.program_id(0) * tm
    is_lat = r0 < n_lat
    if two_source:
        hml_l, hml_c, yda_l, yda_c = refs[:4]
        refs = refs[4:]
        hml = jnp.where(is_lat, hml_l[...], hml_c[...])
        yda = jnp.where(is_lat, yda_l[...], yda_c[...])
    else:
        hml, yda = refs[0][...], refs[1][...]
        refs = refs[2:]
    gs_ref, gm_ref, gd_ref, convw_ref, wsc_ref, wml_ref, wda_ref, wo_ref, lng_ref, lnb_ref, o_ref = refs
    row = lax.broadcasted_iota(jnp.int32, (tm, 1), 0)
    pos = jnp.where(is_lat, (r0 + row) % seq, (r0 - n_lat + row) % ctx_len)
    seq_len = jnp.where(is_lat, seq, ctx_len)
    first = pos == 0
    last = pos == seq_len - 1

    p = p_ref[...].astype(F32)
    prev_row = pprev_ref[...].astype(F32)[BF16_SUBLANES - 1:BF16_SUBLANES, :]
    next_row = pnext_ref[...].astype(F32)[0:1, :]
    p_before = jnp.where(first, 0.0, jnp.where(row == 0, prev_row, pltpu.roll(p, 1, 0)))
    p_after = jnp.where(last, 0.0, jnp.where(row == tm - 1, next_row, pltpu.roll(p, tm - 1, 0)))
    cw = convw_ref[...]
    conv = cw[0:1, :] * p_before + cw[1:2, :] * p + cw[2:3, :] * p_after
    y_sc = (sb_ref[...].astype(F32) * conv).astype(BF16)
    y_ml = smo_ref[...] * hml
    y = (gs_ref[...].astype(F32) * _dot(y_sc, wsc_ref[...].astype(BF16))
         + gm_ref[...].astype(F32) * _dot(y_ml, wml_ref[...].astype(BF16))
         + gd_ref[...].astype(F32) * _dot(yda, wda_ref[...].astype(BF16)))
    y = y.astype(BF16)
    half = tm // 2
    for r in range(2):
        rows = slice(r * half, (r + 1) * half)
        z = _dot(y[rows, :], wo_ref[...].astype(BF16))
        o_ref[rows, :] = _layer_norm(alpha * h_ref[rows, :] + mod_ref[0, 5:6, :] * z,
                                     lng_ref[...], lnb_ref[...])


def _mixer_out(h, mods3, sb, p, smo, hml, yda, gs, gm, gd, conv_w, w_sc, w_ml, w_da, w_o, ln_g, ln_b,
               *, layer, n_rows, seq, ctx_len, batch, n_lat, alpha, hml_ctx=None, yda_ctx=None):
    d = h.shape[1]
    tm = math.gcd(math.gcd(MIXER_OUT_TILE, seq), batch * ctx_len)
    halo = BF16_SUBLANES
    last_halo = p.shape[0] // halo - 1
    two_source = hml_ctx is not None

    def row_spec(width):
        return pl.BlockSpec((tm, width), lambda i: (i, 0))

    if two_source:
        branch = [hml, hml_ctx, yda, yda_ctx]
        branch_specs = (_two_source_specs(tm, ML_V_W, n_lat // tm) + _two_source_specs(tm, DA_V_W, n_lat // tm))
    else:
        branch = [hml, yda]
        branch_specs = [row_spec(ML_V_W), row_spec(DA_V_W)]
    kernel = functools.partial(_mixer_out_kernel, seq=seq, ctx_len=ctx_len, n_lat=n_lat, alpha=alpha,
                               two_source=two_source)
    return pl.pallas_call(
        kernel,
        grid=(n_rows // tm,),
        in_specs=[row_spec(d),
                  pl.BlockSpec((1, N_MOD, d), lambda i: (jnp.minimum(i * tm // seq, batch), 0, 0)),
                  row_spec(SC_WIDTH), row_spec(SC_WIDTH),
                  pl.BlockSpec((halo, SC_WIDTH), lambda i: (jnp.maximum(i * (tm // halo) - 1, 0), 0)),
                  pl.BlockSpec((halo, SC_WIDTH),
                               lambda i: (jnp.minimum((i + 1) * (tm // halo), last_halo), 0)),
                  row_spec(ML_V_W)] + branch_specs + [
                  row_spec(d), row_spec(d), row_spec(d),
                  _resident((SC_KSIZE, SC_WIDTH)),
                  _resident_layer(w_sc, layer), _resident_layer(w_ml, layer), _resident_layer(w_da, layer),
                  _resident_layer(w_o, layer),
                  pl.BlockSpec((1, d), lambda i: (0, 0)),
                  pl.BlockSpec((1, d), lambda i: (0, 0))],
        out_specs=row_spec(d),
        out_shape=jax.ShapeDtypeStruct((n_rows, d), F32),
        compiler_params=_params(1),
        name="mixer_out",
    )(h, mods3, sb, p, p, p, smo, *branch, gs, gm, gd, conv_w, w_sc, w_ml, w_da, w_o,
      ln_g.reshape(1, d), ln_b.reshape(1, d))


def _rope_tables(seq, tile):
    n_freq = DA_HALF // 4
    t = jnp.arange(seq)
    row_ids = (t // GRID_W).astype(F32)
    col_ids = (t % GRID_W).astype(F32)
    inv = ROPE_BASE ** (-jnp.arange(n_freq, dtype=F32) / n_freq)
    ang = jnp.concatenate([row_ids[:, None] * inv, col_ids[:, None] * inv], axis=-1)
    cos, sin = jnp.cos(ang), jnp.sin(ang)
    reps = LANES // DA_HALF
    cos_t = jnp.tile(jnp.concatenate([cos, cos], axis=-1), (1, reps))
    sin_t = jnp.tile(jnp.concatenate([-sin, sin], axis=-1), (1, reps))
    cos_t = jnp.concatenate([cos_t, jnp.ones((tile, LANES), F32)], axis=0)
    sin_t = jnp.concatenate([sin_t, jnp.zeros((tile, LANES), F32)], axis=0)
    return cos_t, sin_t


def kernel(x, c, ctx, c_ctx, w_ada, b_ada, ln_g, ln_b, ffn1_up, ffn1_down, ffn2_up, ffn2_down, w_in, b_in,
           conv_w, w_sc, w_ml, w_da, w_o, lam_q1, lam_k1, lam_q2, lam_k2, da_norm_g):
    batch, seq, d = x.shape
    ctx_len = ctx.shape[1]
    depth = w_ada.shape[0]
    alpha = (2 * depth) ** 0.25
    n_lat = batch * seq
    n_ctx = batch * ctx_len

    n_cond = -(-(batch + 1) // 8) * 8
    cc = jnp.concatenate([c, c_ctx[None, :], jnp.zeros((n_cond - batch - 1, d), F32)], axis=0)
    mods = _ada(cc, w_ada, b_ada).reshape(depth, n_cond, N_MOD, d)

    cos_t, sin_t = _rope_tables(seq, min(MIXER_IN_TILE, seq))
    gate_lo = GATE_COL0
    gate_hi = gate_lo + N_GATES
    w_in_t = jnp.swapaxes(w_in, 1, 2)

    h, h_ctx = x.reshape(n_lat, d), ctx.reshape(n_ctx, d)
    for l in range(depth):
        last = l == depth - 1
        lam_init = 0.8 - 0.6 * math.exp(-0.3 * l)
        mods3 = mods[l]
        b_head = b_in[l, :gate_lo][None, :]
        b_tail = b_in[l, gate_hi:][None, :]
        b_gate = b_in[l, gate_lo:gate_hi][None, :]
        lam_vecs = jnp.stack([lam_q1[l], lam_k1[l], lam_q2[l], lam_k2[l]]).astype(F32)

        h, w_proj_t = _ffn(h, mods3, ffn1_up, ffn1_down, ln_g[l, 0], ln_b[l, 0], layer=l, mod_base=0,
                           n_rows=n_lat + n_ctx, seq=seq, batch=batch, alpha=alpha, h_ctx=h_ctx,
                           cast_stack=w_in_t)
        h_ctx = None

        (sb, p, mq, mk, mv, smo, mg, dq, dk, dv, gs, gm, gd) = _mixer_in(
            h, mods3, w_proj_t, b_head, b_tail, b_gate, cos_t, sin_t,
            seq=seq, batch=batch, n_lat=n_lat)

        hml, hml_ctx = _mlstm_scan(mq, mk, mv, mg, seq=seq, ctx_len=ctx_len, batch=batch, n_lat=n_lat,
                                   need_ctx=not last)
        yda = _diffattn(dq, dk, dv, lam_vecs, da_norm_g[l], q_row0=0, q_len=seq,
                        segments=((0, seq), (n_lat, ctx_len)), batch=batch, lam_init=lam_init)
        yda_ctx = None
        if not last:
            yda_ctx = _diffattn(dq, dk, dv, lam_vecs, da_norm_g[l], q_row0=n_lat, q_len=ctx_len,
                                segments=((n_lat, ctx_len),), batch=batch, lam_init=lam_init,
                                heads_per_step=DA_HEADS)

        n_rows = n_lat if last else n_lat + n_ctx
        h = _mixer_out(h, mods3, sb, p, smo, hml, yda, gs, gm, gd, conv_w[l], w_sc, w_ml, w_da, w_o,
                       ln_g[l, 1], ln_b[l, 1], hml_ctx=hml_ctx, yda_ctx=yda_ctx, layer=l,
                       n_rows=n_rows, seq=seq, ctx_len=ctx_len, batch=batch, n_lat=n_lat, alpha=alpha)
        h = _ffn(h, mods3, ffn2_up, ffn2_down, ln_g[l, 2], ln_b[l, 2], layer=l,
                 mod_base=6, n_rows=n_rows, seq=seq, batch=batch, alpha=alpha)
    return h[:n_lat].reshape(batch, seq, d)
```

```python
import functools
import math

import jax
import jax.numpy as jnp
from jax import lax
from jax.experimental import pallas as pl
from jax.experimental.pallas import tpu as pltpu

GRID_W = 64
N_MOD = 9
SC_WIDTH = 512
SC_KSIZE = 3
ML_HEADS = 4
ML_DQK = 128
ML_DV = 256
DA_HEADS = 4
DA_HALF = 64
DA_DV = 2 * DA_HALF
ROPE_BASE = 10000.0
LN_EPS = 1e-5

ML_QK_W = ML_HEADS * ML_DQK
ML_V_W = ML_HEADS * ML_DV
DA_QK_W = DA_HEADS * 2 * DA_HALF
DA_V_W = DA_HEADS * DA_DV
N_GATES = 4 * ML_HEADS

ML_BLOCK = 256
LANES = 128
BF16_SUBLANES = 16
V7X_VMEM_BYTES = 64 * 1024 * 1024
VMEM_LIMIT = V7X_VMEM_BYTES - 8 * 1024 * 1024

FFN_TILE = 512
FFN_COLS = 256
MIXER_IN_TILE = 256
MIXER_OUT_TILE = 512
DIFFATTN_TILE = 1024

F32 = jnp.float32
BF16 = jnp.bfloat16
NEG_BIG = -1e30
LOG2_E = 1.4426950408889634

assert ML_DQK == LANES and DA_DV == LANES and ML_BLOCK == 2 * LANES


def _dot(a, b):
    return jnp.dot(a, b, preferred_element_type=F32)


def _dot_nt(a, b):
    return lax.dot_general(a, b, (((1,), (1,)), ((), ())), preferred_element_type=F32)


def _wide(x, width):
    return jnp.concatenate([x] * (width // LANES), axis=1)


def _layer_norm(y, g, b):
    mu = jnp.mean(y, axis=-1, keepdims=True)
    yc = y - mu
    var = jnp.mean(yc * yc, axis=-1, keepdims=True)
    return yc * lax.rsqrt(var + LN_EPS) * g + b


def _resident(shape):
    return pl.BlockSpec(shape, lambda *_: (0,) * len(shape), pipeline_mode=pl.Buffered(1))


def _resident_layer(stacked, layer):
    n = stacked.ndim - 1
    return pl.BlockSpec((None,) + stacked.shape[1:], lambda *_: (layer,) + (0,) * n,
                        pipeline_mode=pl.Buffered(1))


def _params(n_axes):
    return pltpu.CompilerParams(dimension_semantics=("parallel",) * n_axes,
                                vmem_limit_bytes=VMEM_LIMIT)


def _ada_kernel(c_ref, w_ref, b_ref, o_ref):
    c = c_ref[...]
    a = (c * jax.nn.sigmoid(c)).astype(BF16)
    o_ref[0] = _dot(a, w_ref[0].astype(BF16)) + b_ref[0]


def _ada(cc, w_ada, b_ada):
    depth, d, n = w_ada.shape
    tn = d
    return pl.pallas_call(
        _ada_kernel,
        grid=(depth, n // tn),
        in_specs=[pl.BlockSpec(cc.shape, lambda l, j: (0, 0)),
                  pl.BlockSpec((1, d, tn), lambda l, j: (l, 0, j)),
                  pl.BlockSpec((1, 1, tn), lambda l, j: (l, 0, j))],
        out_specs=pl.BlockSpec((1, cc.shape[0], tn), lambda l, j: (l, 0, j)),
        out_shape=jax.ShapeDtypeStruct((depth, cc.shape[0], n), F32),
        compiler_params=_params(2),
        name="ada",
    )(cc, w_ada, b_ada.reshape(depth, 1, n))


def _two_source_specs(tm, width, lat_tiles):
    return [pl.BlockSpec((tm, width), lambda i: (jnp.minimum(i, lat_tiles - 1), 0)),
            pl.BlockSpec((tm, width), lambda i: (jnp.maximum(i - lat_tiles, 0), 0))]


def _ffn_kernel(*refs, mod_base, alpha, lat_tiles, cast_extra):
    if lat_tiles is None:
        h_ref, hc_ref = refs[0], None
        refs = refs[1:]
    else:
        h_ref, hc_ref = refs[:2]
        refs = refs[2:]
    if cast_extra:
        mod_ref, wup_ref, wdn_ref, lng_ref, lnb_ref, extra_ref, o_ref, extra_o, g_scr = refs
        extra_o[...] = extra_ref[...].astype(BF16)
    else:
        mod_ref, wup_ref, wdn_ref, lng_ref, lnb_ref, o_ref, g_scr = refs
    h = h_ref[...] if hc_ref is None else jnp.where(pl.program_id(0) < lat_tiles, h_ref[...], hc_ref[...])
    f = wdn_ref.shape[0]
    shift = mod_ref[0, mod_base:mod_base + 1, :]
    scale = mod_ref[0, mod_base + 1:mod_base + 2, :]
    gate = mod_ref[0, mod_base + 2:mod_base + 3, :]
    u = (h * (1.0 + scale) + shift).astype(BF16)
    for c in range(f // FFN_COLS):
        lo, hi = c * FFN_COLS, (c + 1) * FFN_COLS
        a = _dot(u, wup_ref[:, lo:hi].astype(BF16))
        v = _dot(u, wup_ref[:, f + lo:f + hi].astype(BF16))
        g_scr[:, lo:hi] = (a * jax.nn.sigmoid(a) * v).astype(BF16)
    half = h.shape[0] // 2
    for r in range(2):
        rows = slice(r * half, (r + 1) * half)
        d = _dot(g_scr[rows, :], wdn_ref[...].astype(BF16))
        o_ref[rows, :] = _layer_norm(alpha * h[rows, :] + (0.5 * gate) * d, lng_ref[...], lnb_ref[...])


def _ffn(h, mods3, w_up, w_dn, ln_g, ln_b, *, layer, mod_base, n_rows, seq, batch, alpha, h_ctx=None,
         cast_stack=None):
    d = h.shape[1]
    f = w_dn.shape[1]
    tm = min(FFN_TILE, seq)
    steps = n_rows // tm
    lat_tiles = None if h_ctx is None else h.shape[0] // tm
    kernel = functools.partial(_ffn_kernel, mod_base=mod_base, alpha=alpha, lat_tiles=lat_tiles,
                               cast_extra=cast_stack is not None)
    if h_ctx is None:
        sources, source_specs = [h], [pl.BlockSpec((tm, d), lambda i: (i, 0))]
    else:
        sources, source_specs = [h, h_ctx], _two_source_specs(tm, d, lat_tiles)
    extra, extra_specs = [], []
    out_specs = [pl.BlockSpec((tm, d), lambda i: (i, 0))]
    out_shape = [jax.ShapeDtypeStruct((n_rows, d), F32)]
    if cast_stack is not None:
        n_extra = cast_stack.shape[1]
        rows_blk = -(-pl.cdiv(n_extra, steps) // BF16_SUBLANES) * BF16_SUBLANES
        last_blk = pl.cdiv(n_extra, rows_blk) - 1
        extra = [cast_stack]
        extra_specs = [pl.BlockSpec((None, rows_blk, d), lambda i: (layer, jnp.minimum(i, last_blk), 0))]
        out_specs.append(pl.BlockSpec((rows_blk, d), lambda i: (jnp.minimum(i, last_blk), 0)))
        out_shape.append(jax.ShapeDtypeStruct((n_extra, d), BF16))
    out = pl.pallas_call(
        kernel,
        grid=(steps,),
        in_specs=source_specs + [
                  pl.BlockSpec((1, N_MOD, d), lambda i: (jnp.minimum(i * tm // seq, batch), 0, 0)),
                  _resident_layer(w_up, layer),
                  _resident_layer(w_dn, layer),
                  pl.BlockSpec((1, d), lambda i: (0, 0)),
                  pl.BlockSpec((1, d), lambda i: (0, 0))] + extra_specs,
        out_specs=out_specs,
        out_shape=out_shape,
        scratch_shapes=[pltpu.VMEM((tm, f), BF16)],
        compiler_params=pltpu.CompilerParams(
            dimension_semantics=("arbitrary" if cast_stack is not None else "parallel",),
            vmem_limit_bytes=VMEM_LIMIT),
        name="ffn",
    )(*sources, mods3, w_up, w_dn, ln_g.reshape(1, d), ln_b.reshape(1, d), *extra)
    return out if cast_stack is not None else out[0]


GATE_COL0 = 3 * SC_WIDTH + 2 * ML_QK_W + 2 * ML_V_W


def _proj_cols(d):
    parts = ((("sb", SC_WIDTH), ("sc", SC_WIDTH), ("sx", SC_WIDTH),
              ("mq", ML_QK_W), ("mk", ML_QK_W), ("mv", ML_V_W), ("mo", ML_V_W)),
             (("dq", DA_QK_W), ("dk", DA_QK_W), ("dv", DA_V_W), ("gs", d), ("gm", d), ("gd", d)))
    cols, widths = {}, []
    for part, sizes in enumerate(parts):
        off = 0
        for name, w in sizes:
            cols[name] = (part, off, off + w)
            off += w
        widths.append(off)
    return cols, widths


def _mixer_in_kernel(h_ref, mod_ref, w_ref, bh_ref, bt_ref, bg_ref, cos_ref, sin_ref,
                     sb_o, p_o, mq_o, mk_o, mv_o, smo_o, mg_o, dq_o, dk_o, dv_o, gs_o, gm_o, gd_o):
    d = h_ref.shape[1]
    tm = h_ref.shape[0]
    cols, _ = _proj_cols(d)
    h = h_ref[...]
    shift = mod_ref[0, 3:4, :]
    scale = mod_ref[0, 4:5, :]
    u = (h * (1.0 + scale) + shift).astype(BF16)

    def proj(name):
        part, lo, hi = cols[name]
        b_ref = (bh_ref, bt_ref)[part]
        row0 = 0 if part == 0 else GATE_COL0 + N_GATES
        return _dot_nt(u, w_ref[row0 + lo:row0 + hi, :]) + b_ref[:, lo:hi]

    sb_o[...] = proj("sb").astype(BF16)
    p_o[...] = (proj("sc") * proj("sx")).astype(BF16)
    mq_o[...] = proj("mq").astype(BF16)
    mk_o[...] = (proj("mk") * (ML_DQK ** -0.5)).astype(BF16)
    mv_o[...] = proj("mv").astype(BF16)
    smo_o[...] = jax.nn.sigmoid(proj("mo")).astype(BF16)
    mg_o[...] = _dot_nt(u, w_ref[GATE_COL0:GATE_COL0 + N_GATES, :]) + bg_ref[...]

    cos_t = cos_ref[...]
    sin_t = sin_ref[...]
    lane = lax.broadcasted_iota(jnp.int32, cos_t.shape, 1)
    first_half = (lane % DA_HALF) < (DA_HALF // 2)

    def rope_store(z, out_ref, mult):
        for k in range(z.shape[1] // LANES):
            x = z[:, k * LANES:(k + 1) * LANES]
            partner = jnp.where(first_half,
                                pltpu.roll(x, LANES - DA_HALF // 2, 1),
                                pltpu.roll(x, DA_HALF // 2, 1))
            out_ref[:, k * LANES:(k + 1) * LANES] = ((x * cos_t + partner * sin_t) * mult).astype(BF16)

    rope_store(proj("dq"), dq_o, (DA_HALF ** -0.5) * LOG2_E)
    rope_store(proj("dk"), dk_o, 1.0)
    dv = proj("dv").astype(BF16)
    ones = jnp.ones((tm, DA_DV), BF16)
    for k in range(DA_HEADS):
        dv_o[:, 2 * k * DA_DV:(2 * k + 1) * DA_DV] = dv[:, k * DA_DV:(k + 1) * DA_DV]
        dv_o[:, (2 * k + 1) * DA_DV:(2 * k + 2) * DA_DV] = ones
    gs_o[...] = jax.nn.sigmoid(proj("gs")).astype(BF16)
    gm_o[...] = jax.nn.sigmoid(proj("gm")).astype(BF16)
    gd_o[...] = jax.nn.sigmoid(proj("gd")).astype(BF16)


def _mixer_in(h, mods3, w_t, b_head, b_tail, b_gate, cos_t, sin_t, *, seq, batch, n_lat):
    rows, d = h.shape
    _, (n_head, n_tail) = _proj_cols(d)
    tm = min(MIXER_IN_TILE, seq)
    lat_tiles = n_lat // tm
    rope_blocks = seq // tm

    def row_spec(width):
        return pl.BlockSpec((tm, width), lambda i: (i, 0))

    def rope_map(i):
        return (jnp.where(i < lat_tiles, i % rope_blocks, rope_blocks), 0)

    widths = (SC_WIDTH, SC_WIDTH, ML_QK_W, ML_QK_W, ML_V_W, ML_V_W, N_GATES,
              DA_QK_W, DA_QK_W, 2 * DA_V_W, d, d, d)
    dtypes = (BF16,) * 6 + (F32,) + (BF16,) * 6
    return pl.pallas_call(
        _mixer_in_kernel,
        grid=(rows // tm,),
        in_specs=[row_spec(d),
                  pl.BlockSpec((1, N_MOD, d), lambda i: (jnp.minimum(i * tm // seq, batch), 0, 0)),
                  _resident(w_t.shape),
                  _resident((1, n_head)),
                  _resident((1, n_tail)),
                  _resident((1, N_GATES)),
                  pl.BlockSpec((tm, LANES), rope_map),
                  pl.BlockSpec((tm, LANES), rope_map)],
        out_specs=[row_spec(w) for w in widths],
        out_shape=[jax.ShapeDtypeStruct((rows, w), dt) for w, dt in zip(widths, dtypes)],
        compiler_params=_params(1),
        name="mixer_in",
    )(h, mods3, w_t, b_head, b_tail, b_gate, cos_t, sin_t)


def _log_sigmoid(x):
    return jnp.minimum(x, 0.0) - jnp.log(1.0 + jnp.exp(-jnp.abs(x)))


def _split3(x):
    pieces, rest = [], x
    for _ in range(3):
        piece = rest.astype(BF16)
        pieces.append(piece)
        rest = rest - piece.astype(F32)
    return pieces


def _mlstm_gate_kernel(g_ref, o_ref, o3_ref):
    blk = ML_BLOCK
    t_idx = lax.broadcasted_iota(jnp.int32, (blk, blk), 0)
    s_idx = lax.broadcasted_iota(jnp.int32, (blk, blk), 1)
    tril = (s_idx <= t_idx).astype(BF16)
    lane = lax.broadcasted_iota(jnp.int32, (blk, N_GATES), 1)
    src = lax.broadcasted_iota(jnp.int32, (N_GATES, LANES), 0)
    dst = lax.broadcasted_iota(jnp.int32, (N_GATES, LANES), 1)
    places = [(dst == src + term * N_GATES).astype(BF16) for term in range(3)]
    for j in range(g_ref.shape[0] // blk):
        sl = pl.ds(j * blk, blk)
        g = g_ref[sl, :]
        ls = _log_sigmoid(g)
        prefix = functools.reduce(jnp.add, [_dot(tril, piece) for piece in _split3(ls)])
        suffix = prefix[blk - 1:blk, :] - prefix + ls
        out = jnp.where(lane < 2 * ML_HEADS, g, jnp.where(lane < 3 * ML_HEADS, prefix, suffix))
        o_ref[sl, :] = out
        placed = functools.reduce(jnp.add, [_dot(piece, place) for piece, place in zip(_split3(out), places)])
        o3_ref[sl, :] = placed.astype(BF16)


def _mlstm_gates(mg):
    rows = mg.shape[0]
    step = math.gcd(rows, 8 * ML_BLOCK)
    return pl.pallas_call(
        _mlstm_gate_kernel,
        grid=(rows // step,),
        in_specs=[pl.BlockSpec((step, N_GATES), lambda i: (i, 0))],
        out_specs=[pl.BlockSpec((step, N_GATES), lambda i: (i, 0)),
                   pl.BlockSpec((step, LANES), lambda i: (i, 0))],
        out_shape=[jax.ShapeDtypeStruct((rows, N_GATES), F32),
                   jax.ShapeDtypeStruct((rows, LANES), BF16)],
        compiler_params=_params(1),
        name="mlstm_gates",
    )(mg)


ML_HEADS_PER_STEP = 4
N_ROWS_PAD = BF16_SUBLANES
ST_ROWS = 2 * N_ROWS_PAD + 2 * ML_DV


def _mlstm_scan_kernel(*refs, need_ctx, heads_per_step):
    ins, rest = refs[:12], refs[12:]
    if need_ctx:
        hl_ref, hc_ref, st_in, m_in = rest
    else:
        hl_ref, st_in, m_in = rest
        hc_ref = None
    ql, qc, kl, kc, ktl, ktc, vtl, vtc, g3l, g3c, grl, grc = ins
    for hh in range(heads_per_step):
        qk_cols = pl.ds(hh * ML_DQK, ML_DQK)
        v_cols = pl.ds(hh * ML_DV, ML_DV)
        _mlstm_head(pl.program_id(1) * heads_per_step + hh,
                    ql.at[:, qk_cols], qc.at[:, qk_cols], kl.at[:, qk_cols], kc.at[:, qk_cols],
                    ktl.at[:, qk_cols, :], ktc.at[:, qk_cols, :], vtl.at[:, v_cols, :], vtc.at[:, v_cols, :],
                    g3l, g3c, grl, grc, hl_ref.at[:, v_cols],
                    None if hc_ref is None else hc_ref.at[:, v_cols], st_in.at[hh], m_in.at[hh])


def _mlstm_head(head, ql_ref, qc_ref, kl_ref, kc_ref, ktl_ref, ktc_ref, vtl_ref, vtc_ref,
                g3l_ref, g3c_ref, grl_ref, grc_ref, hl_ref, hc_ref, st_in, m_in):
    need_ctx = hc_ref is not None
    blk = ML_BLOCK
    n_lat_blk = ql_ref.shape[0] // blk
    n_ctx_blk = qc_ref.shape[0] // blk

    s_idx = lax.broadcasted_iota(jnp.int32, (blk, blk), 0)
    t_idx = lax.broadcasted_iota(jnp.int32, (blk, blk), 1)
    lane_t = lax.broadcasted_iota(jnp.int32, (1, blk), 1)
    ones_n = jnp.ones((N_ROWS_PAD, blk), BF16)
    sel_src = lax.broadcasted_iota(jnp.int32, (LANES, 2 * LANES), 0)
    sel_dst = lax.broadcasted_iota(jnp.int32, (LANES, 2 * LANES), 1)
    sel_bwd = sel_dst >= LANES
    col = sel_src % N_GATES
    in_terms = sel_src < 3 * N_GATES
    plus = jnp.logical_and(col == head + jnp.where(sel_bwd, ML_HEADS, 0), in_terms)
    minus = jnp.logical_and(col == head + jnp.where(sel_bwd, 3 * ML_HEADS, 2 * ML_HEADS), in_terms)
    sel = (plus.astype(F32) - minus.astype(F32)).astype(BF16)

    def gate_rows(gr_ref, j, bwd):
        i_idx = head + (ML_HEADS if bwd else 0)
        f_idx = head + (3 * ML_HEADS if bwd else 2 * ML_HEADS)
        return gr_ref[j, pl.ds(f_idx, 1), :], gr_ref[j, pl.ds(i_idx, 1), :]

    def scan_states():
        ctx_blocks = [(grc_ref, ktc_ref, vtc_ref, j, j) for j in range(n_ctx_blk)]
        lat_blocks = [(grl_ref, ktl_ref, vtl_ref, j, n_ctx_blk + j) for j in range(n_lat_blk)]
        orders = (ctx_blocks + lat_blocks, ctx_blocks[::-1] + lat_blocks[::-1])
        m_prev = [jnp.zeros((1, 1), F32)] * 2
        ct_run = [jnp.zeros((ML_DV, ML_DQK), F32)] * 2
        n_run = [jnp.zeros((N_ROWS_PAD, ML_DQK), F32)] * 2
        for step in range(len(orders[0])):
            for d, bwd in enumerate((False, True)):
                gr_ref, kt_ref, vt_ref, j, slot = orders[d][step]
                c_row0 = 2 * N_ROWS_PAD + d * ML_DV
                f_row, i_row = gate_rows(gr_ref, j, bwd)
                b_end = jnp.sum(jnp.where(lane_t == (0 if bwd else blk - 1), f_row, 0.0), axis=1,
                                keepdims=True)
                g_log = b_end + (i_row - f_row)
                m_new = jnp.maximum(b_end + m_prev[d], jnp.max(g_log, axis=1, keepdims=True))
                a_prev = jnp.exp(b_end + m_prev[d] - m_new)
                kw = (kt_ref[j].astype(F32) * jnp.exp(g_log - m_new)).astype(BF16)
                st_in[slot, d * N_ROWS_PAD:(d + 1) * N_ROWS_PAD, :] = n_run[d].astype(BF16)
                st_in[slot, c_row0:c_row0 + ML_DV, :] = ct_run[d].astype(BF16)
                m_in[slot, :, d * LANES:(d + 1) * LANES] = jnp.broadcast_to(m_prev[d], (8, LANES))
                ct_run[d] = a_prev * ct_run[d] + _dot_nt(vt_ref[j], kw)
                n_run[d] = a_prev * n_run[d] + _dot_nt(ones_n, kw)
                m_prev[d] = m_new

    def outputs(q, k, vt, g3, rows, slot):
        qk_t = _dot_nt(k, q)
        state = _dot_nt(st_in[slot], q)
        r_both = _dot(g3, sel)
        m_prev_both = m_in[slot][0:1, :]
        scaled, inter = [], []
        for d, bwd in enumerate((False, True)):
            f_row, _ = rows[d]
            r_rep = r_both[:, d * LANES:(d + 1) * LANES]
            mask = (s_idx >= t_idx) if bwd else (s_idx <= t_idx)
            d_log = jnp.where(mask, _wide(r_rep, blk) + f_row, NEG_BIG)
            m_loc = jnp.max(d_log, axis=0, keepdims=True)
            s = qk_t * jnp.exp(d_log - m_loc)
            den_loc = jnp.sum(s, axis=0, keepdims=True)
            m_inter = f_row + m_prev_both[:, d * LANES:d * LANES + 1]
            m_t = jnp.maximum(m_inter, m_loc)
            w_inter = jnp.exp(m_inter - m_t)
            w_loc = jnp.exp(m_loc - m_t)
            qn = state[d * N_ROWS_PAD:d * N_ROWS_PAD + 1, :]
            den = w_inter * qn + w_loc * den_loc
            inv = 1.0 / jnp.maximum(jnp.abs(den), jnp.exp(-m_t))
            scaled.append(s * (w_loc * inv))
            c_row0 = 2 * N_ROWS_PAD + d * ML_DV
            inter.append(state[c_row0:c_row0 + ML_DV, :] * (w_inter * inv))
        return _dot(vt, (scaled[0] + scaled[1]).astype(BF16)) + inter[0] + inter[1]

    def lat_rows(j):
        return pl.ds(pl.multiple_of(j * blk, blk), blk)

    scan_states()

    if need_ctx:
        for j in range(n_ctx_blk):
            sl = pl.ds(j * blk, blk)
            rows = (gate_rows(grc_ref, j, False), gate_rows(grc_ref, j, True))
            out_t = outputs(qc_ref[sl, :], kc_ref[sl, :], vtc_ref[j], g3c_ref[sl, :], rows, j)
            hc_ref[sl, :] = out_t.T.astype(hc_ref.dtype)

    def out_body(j, carry):
        sl = lat_rows(j)
        rows = (gate_rows(grl_ref, j, False), gate_rows(grl_ref, j, True))
        out_t = outputs(ql_ref[sl, :], kl_ref[sl, :], vtl_ref[j], g3l_ref[sl, :], rows, n_ctx_blk + j)
        hl_ref[sl, :] = out_t.T.astype(hl_ref.dtype)
        return carry

    lax.fori_loop(0, n_lat_blk, out_body, 0, unroll=True)


def _mlstm_scan(mq, mk, mv, mg, *, seq, ctx_len, batch, n_lat, need_ctx):
    blk = ML_BLOCK
    rows = mq.shape[0]
    ctx0 = n_lat // ctx_len
    n_blk = (seq + ctx_len) // blk
    gates, gates3 = _mlstm_gates(mg)
    gates_rows = gates.T.reshape(N_GATES, rows // blk, blk).transpose(1, 0, 2)
    mk_t = mk.reshape(rows // blk, blk, ML_QK_W).transpose(0, 2, 1)
    mv_t = mv.reshape(rows // blk, blk, ML_V_W).transpose(0, 2, 1)

    hps = ML_HEADS_PER_STEP

    def lat(width):
        return pl.BlockSpec((seq, hps * width), lambda b, h: (b, h))

    def ctx(width):
        return pl.BlockSpec((ctx_len, hps * width), lambda b, h: (ctx0 + b, h))

    def lat_t(width):
        return pl.BlockSpec((seq // blk, hps * width, blk), lambda b, h: (b, h, 0))

    def ctx_t(width):
        return pl.BlockSpec((ctx_len // blk, hps * width, blk), lambda b, h: (ctx0 + b, h, 0))

    in_specs = [lat(ML_DQK), ctx(ML_DQK), lat(ML_DQK), ctx(ML_DQK),
                lat_t(ML_DQK), ctx_t(ML_DQK), lat_t(ML_DV), ctx_t(ML_DV),
                pl.BlockSpec((seq, LANES), lambda b, h: (b, 0)),
                pl.BlockSpec((ctx_len, LANES), lambda b, h: (ctx0 + b, 0)),
                pl.BlockSpec((seq // blk, N_GATES, blk), lambda b, h: (b, 0, 0)),
                pl.BlockSpec((ctx_len // blk, N_GATES, blk), lambda b, h: (ctx0 + b, 0, 0))]
    out_specs = [pl.BlockSpec((seq, hps * ML_DV), lambda b, h: (b, h))]
    out_shape = [jax.ShapeDtypeStruct((n_lat, ML_V_W), BF16)]
    if need_ctx:
        out_specs.append(pl.BlockSpec((ctx_len, hps * ML_DV), lambda b, h: (b, h)))
        out_shape.append(jax.ShapeDtypeStruct((batch * ctx_len, ML_V_W), BF16))
    scratch = [pltpu.VMEM((hps, n_blk, ST_ROWS, ML_DQK), BF16),
               pltpu.VMEM((hps, n_blk, 8, 2 * LANES), F32)]
    out = pl.pallas_call(
        functools.partial(_mlstm_scan_kernel, need_ctx=need_ctx, heads_per_step=hps),
        grid=(batch, ML_HEADS // hps),
        in_specs=in_specs,
        out_specs=out_specs,
        out_shape=out_shape,
        scratch_shapes=scratch,
        compiler_params=_params(2),
        name="mlstm",
    )(mq, mq, mk, mk, mk_t, mk_t, mv_t, mv_t, gates3, gates3, gates_rows, gates_rows)
    return out if need_ctx else (out[0], None)


def _diffattn_kernel(lam_ref, g_ref, q_ref, *refs, n_seg, lam_init):
    k_refs = refs[:n_seg]
    v_refs = refs[n_seg:2 * n_seg]
    o_ref = refs[2 * n_seg]
    lv = lam_ref[...]
    lam = (jnp.exp(jnp.sum(lv[0:1] * lv[1:2], axis=1, keepdims=True))
           - jnp.exp(jnp.sum(lv[2:3] * lv[3:4], axis=1, keepdims=True)) + lam_init)
    for hh in range(q_ref.shape[1] // LANES):
        q = q_ref[:, hh * LANES:(hh + 1) * LANES]
        ks = [k_ref[:, hh * LANES:(hh + 1) * LANES] for k_ref in k_refs]
        vs = [v_ref[:, 2 * hh * DA_DV:2 * (hh + 1) * DA_DV] for v_ref in v_refs]
        lane = lax.broadcasted_iota(jnp.int32, q.shape, 1)
        zero = jnp.zeros_like(q)
        halves = [jnp.where(lane < DA_HALF, q, zero), jnp.where(lane >= DA_HALF, q, zero)]
        scores = [[_dot_nt(qh, k) for k in ks] for qh in halves]
        maxes = [functools.reduce(jnp.maximum, [jnp.max(s, axis=1, keepdims=True) for s in ss])
                 for ss in scores]
        probs = [[jnp.exp2((s - m).astype(BF16)) for s in ss] for ss, m in zip(scores, maxes)]
        heads = []
        for ps in probs:
            acc = None
            for p, v in zip(ps, vs):
                part = _dot(p, v)
                acc = part if acc is None else acc + part
            heads.append(acc[:, :DA_DV] * (1.0 / acc[:, DA_DV:]))
        o = heads[0] - lam * heads[1]
        ms = jnp.mean(o * o, axis=1, keepdims=True)
        o_ref[:, hh * DA_DV:(hh + 1) * DA_DV] = (o * lax.rsqrt(ms + LN_EPS) * g_ref[...]
                                                 * (1.0 - lam_init)).astype(o_ref.dtype)


def _diffattn(dq, dk, dv, lam_vecs, da_g, *, q_row0, q_len, segments, batch, lam_init, heads_per_step=1):
    tq = min(DIFFATTN_TILE, q_len)
    nq = q_len // tq
    q0 = q_row0 // tq
    hps = heads_per_step

    def seg_spec(row0, length, width):
        return pl.BlockSpec((length, hps * width), lambda b, h, i: (row0 // length + b, h))

    return pl.pallas_call(
        functools.partial(_diffattn_kernel, n_seg=len(segments), lam_init=lam_init),
        grid=(batch, DA_HEADS // hps, nq),
        in_specs=[pl.BlockSpec(lam_vecs.shape, lambda b, h, i: (0, 0)),
                  pl.BlockSpec((1, DA_DV), lambda b, h, i: (0, 0)),
                  pl.BlockSpec((tq, hps * LANES), lambda b, h, i: (q0 + b * nq + i, h))]
                 + [seg_spec(r0, ln, LANES) for r0, ln in segments]
                 + [seg_spec(r0, ln, 2 * DA_DV) for r0, ln in segments],
        out_specs=pl.BlockSpec((tq, hps * DA_DV), lambda b, h, i: (b * nq + i, h)),
        out_shape=jax.ShapeDtypeStruct((batch * q_len, DA_V_W), BF16),
        compiler_params=_params(3),
        name="diffattn",
    )(lam_vecs, da_g.reshape(1, DA_DV), dq, *([dk] * len(segments)), *([dv] * len(segments)))


def _mixer_out_kernel(h_ref, mod_ref, sb_ref, p_ref, pprev_ref, pnext_ref, smo_ref, *refs,
                      seq, ctx_len, n_lat, alpha, two_source):
    tm = h_ref.shape[0]
    r0 = pl.program_id(0) * tm
    is_lat = r0 < n_lat
    if two_source:
        hml_l, hml_c, yda_l, yda_c = refs[:4]
        refs = refs[4:]
        hml = jnp.where(is_lat, hml_l[...], hml_c[...])
        yda = jnp.where(is_lat, yda_l[...], yda_c[...])
    else:
        hml, yda = refs[0][...], refs[1][...]
        refs = refs[2:]
    gs_ref, gm_ref, gd_ref, convw_ref, wsc_ref, wml_ref, wda_ref, wo_ref, lng_ref, lnb_ref, o_ref = refs
    pos0 = jnp.where(is_lat, r0 % seq, (r0 - n_lat) % ctx_len)
    seq_len = jnp.where(is_lat, seq, ctx_len)
    starts_seq = pos0 == 0
    ends_seq = (pos0 + tm) % seq_len == 0
    inner_starts = tuple(range(ctx_len, tm, ctx_len))
    slab_row = lax.broadcasted_iota(jnp.int32, (8, 1), 0)

    def put_row(x, r, new_row):
        s = r // 8 * 8
        slab = jnp.where(slab_row == r - s, new_row, x[s:s + 8, :])
        parts = ([x[:s, :]] if s else []) + [slab] + ([x[s + 8:, :]] if s + 8 < x.shape[0] else [])
        return jnp.concatenate(parts, axis=0)

    p = p_ref[...].astype(F32)
    prev_row = pprev_ref[...].astype(F32)[BF16_SUBLANES - 1:BF16_SUBLANES, :]
    next_row = pnext_ref[...].astype(F32)[0:1, :]
    p_before = put_row(pltpu.roll(p, 1, 0), 0, jnp.where(starts_seq, 0.0, prev_row))
    p_after = put_row(pltpu.roll(p, tm - 1, 0), tm - 1, jnp.where(ends_seq, 0.0, next_row))
    for b in inner_starts:
        p_before = put_row(p_before, b, jnp.where(is_lat, p[b - 1:b, :], 0.0))
        p_after = put_row(p_after, b - 1, jnp.where(is_lat, p[b:b + 1, :], 0.0))
    cw = convw_ref[...]
    conv = cw[0:1, :] * p_before + cw[1:2, :] * p + cw[2:3, :] * p_after
    y_sc = (sb_ref[...].astype(F32) * conv).astype(BF16)
    y_ml = smo_ref[...] * hml
    y = (gs_ref[...].astype(F32) * _dot(y_sc, wsc_ref[...].astype(BF16))
         + gm_ref[...].astype(F32) * _dot(y_ml, wml_ref[...].astype(BF16))
         + gd_ref[...].astype(F32) * _dot(yda, wda_ref[...].astype(BF16)))
    y = y.astype(BF16)
    half = tm // 2
    for r in range(2):
        rows = slice(r * half, (r + 1) * half)
        z = _dot(y[rows, :], wo_ref[...].astype(BF16))
        o_ref[rows, :] = _layer_norm(alpha * h_ref[rows, :] + mod_ref[0, 5:6, :] * z,
                                     lng_ref[...], lnb_ref[...])


def _mixer_out(h, mods3, sb, p, smo, hml, yda, gs, gm, gd, conv_w, w_sc, w_ml, w_da, w_o, ln_g, ln_b,
               *, layer, n_rows, seq, ctx_len, batch, n_lat, alpha, hml_ctx=None, yda_ctx=None):
    d = h.shape[1]
    tm = math.gcd(math.gcd(MIXER_OUT_TILE, seq), batch * ctx_len)
    halo = BF16_SUBLANES
    last_halo = p.shape[0] // halo - 1
    two_source = hml_ctx is not None

    def row_spec(width):
        return pl.BlockSpec((tm, width), lambda i: (i, 0))

    if two_source:
        branch = [hml, hml_ctx, yda, yda_ctx]
        branch_specs = (_two_source_specs(tm, ML_V_W, n_lat // tm) + _two_source_specs(tm, DA_V_W, n_lat // tm))
    else:
        branch = [hml, yda]
        branch_specs = [row_spec(ML_V_W), row_spec(DA_V_W)]
    kernel = functools.partial(_mixer_out_kernel, seq=seq, ctx_len=ctx_len, n_lat=n_lat, alpha=alpha,
                               two_source=two_source)
    return pl.pallas_call(
        kernel,
        grid=(n_rows // tm,),
        in_specs=[row_spec(d),
                  pl.BlockSpec((1, N_MOD, d), lambda i: (jnp.minimum(i * tm // seq, batch), 0, 0)),
                  row_spec(SC_WIDTH), row_spec(SC_WIDTH),
                  pl.BlockSpec((halo, SC_WIDTH), lambda i: (jnp.maximum(i * (tm // halo) - 1, 0), 0)),
                  pl.BlockSpec((halo, SC_WIDTH),
                               lambda i: (jnp.minimum((i + 1) * (tm // halo), last_halo), 0)),
                  row_spec(ML_V_W)] + branch_specs + [
                  row_spec(d), row_spec(d), row_spec(d),
                  _resident((SC_KSIZE, SC_WIDTH)),
                  _resident_layer(w_sc, layer), _resident_layer(w_ml, layer), _resident_layer(w_da, layer),
                  _resident_layer(w_o, layer),
                  pl.BlockSpec((1, d), lambda i: (0, 0)),
                  pl.BlockSpec((1, d), lambda i: (0, 0))],
        out_specs=row_spec(d),
        out_shape=jax.ShapeDtypeStruct((n_rows, d), F32),
        compiler_params=_params(1),
        name="mixer_out",
    )(h, mods3, sb, p, p, p, smo, *branch, gs, gm, gd, conv_w, w_sc, w_ml, w_da, w_o,
      ln_g.reshape(1, d), ln_b.reshape(1, d))


def _rope_tables(seq, tile):
    n_freq = DA_HALF // 4
    t = jnp.arange(seq)
    row_ids = (t // GRID_W).astype(F32)
    col_ids = (t % GRID_W).astype(F32)
    inv = ROPE_BASE ** (-jnp.arange(n_freq, dtype=F32) / n_freq)
    ang = jnp.concatenate([row_ids[:, None] * inv, col_ids[:, None] * inv], axis=-1)
    cos, sin = jnp.cos(ang), jnp.sin(ang)
    reps = LANES // DA_HALF
    cos_t = jnp.tile(jnp.concatenate([cos, cos], axis=-1), (1, reps))
    sin_t = jnp.tile(jnp.concatenate([-sin, sin], axis=-1), (1, reps))
    cos_t = jnp.concatenate([cos_t, jnp.ones((tile, LANES), F32)], axis=0)
    sin_t = jnp.concatenate([sin_t, jnp.zeros((tile, LANES), F32)], axis=0)
    return cos_t, sin_t


def kernel(x, c, ctx, c_ctx, w_ada, b_ada, ln_g, ln_b, ffn1_up, ffn1_down, ffn2_up, ffn2_down, w_in, b_in,
           conv_w, w_sc, w_ml, w_da, w_o, lam_q1, lam_k1, lam_q2, lam_k2, da_norm_g):
    batch, seq, d = x.shape
    ctx_len = ctx.shape[1]
    depth = w_ada.shape[0]
    alpha = (2 * depth) ** 0.25
    n_lat = batch * seq
    n_ctx = batch * ctx_len

    n_cond = -(-(batch + 1) // 8) * 8
    cc = jnp.concatenate([c, c_ctx[None, :], jnp.zeros((n_cond - batch - 1, d), F32)], axis=0)
    mods = _ada(cc, w_ada, b_ada).reshape(depth, n_cond, N_MOD, d)

    cos_t, sin_t = _rope_tables(seq, min(MIXER_IN_TILE, seq))
    gate_lo = GATE_COL0
    gate_hi = gate_lo + N_GATES
    w_in_t = jnp.swapaxes(w_in, 1, 2)

    h, h_ctx = x.reshape(n_lat, d), ctx.reshape(n_ctx, d)
    for l in range(depth):
        last = l == depth - 1
        lam_init = 0.8 - 0.6 * math.exp(-0.3 * l)
        mods3 = mods[l]
        b_head = b_in[l, :gate_lo][None, :]
        b_tail = b_in[l, gate_hi:][None, :]
        b_gate = b_in[l, gate_lo:gate_hi][None, :]
        lam_vecs = jnp.stack([lam_q1[l], lam_k1[l], lam_q2[l], lam_k2[l]]).astype(F32)

        h, w_proj_t = _ffn(h, mods3, ffn1_up, ffn1_down, ln_g[l, 0], ln_b[l, 0], layer=l, mod_base=0,
                           n_rows=n_lat + n_ctx, seq=seq, batch=batch, alpha=alpha, h_ctx=h_ctx,
                           cast_stack=w_in_t)
        h_ctx = None

        (sb, p, mq, mk, mv, smo, mg, dq, dk, dv, gs, gm, gd) = _mixer_in(
            h, mods3, w_proj_t, b_head, b_tail, b_gate, cos_t, sin_t,
            seq=seq, batch=batch, n_lat=n_lat)

        hml, hml_ctx = _mlstm_scan(mq, mk, mv, mg, seq=seq, ctx_len=ctx_len, batch=batch, n_lat=n_lat,
                                   need_ctx=not last)
        yda = _diffattn(dq, dk, dv, lam_vecs, da_norm_g[l], q_row0=0, q_len=seq,
                        segments=((0, seq), (n_lat, ctx_len)), batch=batch, lam_init=lam_init)
        yda_ctx = None
        if not last:
            yda_ctx = _diffattn(dq, dk, dv, lam_vecs, da_norm_g[l], q_row0=n_lat, q_len=ctx_len,
                                segments=((n_lat, ctx_len),), batch=batch, lam_init=lam_init,
                                heads_per_step=DA_HEADS)

        n_rows = n_lat if last else n_lat + n_ctx
        h = _mixer_out(h, mods3, sb, p, smo, hml, yda, gs, gm, gd, conv_w[l], w_sc, w_ml, w_da, w_o,
                       ln_g[l, 1], ln_b[l, 1], hml_ctx=hml_ctx, yda_ctx=yda_ctx, layer=l,
                       n_rows=n_rows, seq=seq, ctx_len=ctx_len, batch=batch, n_lat=n_lat, alpha=alpha)
        h = _ffn(h, mods3, ffn2_up, ffn2_down, ln_g[l, 2], ln_b[l, 2], layer=l,
                 mod_base=6, n_rows=n_rows, seq=seq, batch=batch, alpha=alpha)
    return h[:n_lat].reshape(batch, seq, d)
```

```python
import functools
import math

import jax
import jax.numpy as jnp
from jax import lax
from jax.experimental import pallas as pl
from jax.experimental.pallas import tpu as pltpu

GRID_W = 64
N_MOD = 9
SC_WIDTH = 512
SC_KSIZE = 3
ML_HEADS = 4
ML_DQK = 128
ML_DV = 256
DA_HEADS = 4
DA_HALF = 64
DA_DV = 2 * DA_HALF
ROPE_BASE = 10000.0
LN_EPS = 1e-5

ML_QK_W = ML_HEADS * ML_DQK
ML_V_W = ML_HEADS * ML_DV
DA_QK_W = DA_HEADS * 2 * DA_HALF
DA_V_W = DA_HEADS * DA_DV
N_GATES = 4 * ML_HEADS

ML_BLOCK = 256
LANES = 128
BF16_SUBLANES = 16
V7X_VMEM_BYTES = 64 * 1024 * 1024
VMEM_LIMIT = V7X_VMEM_BYTES - 8 * 1024 * 1024

FFN_TILE = 512
FFN_COLS = 256
MIXER_IN_TILE = 256
MIXER_OUT_TILE = 512
DIFFATTN_TILE = 1024

F32 = jnp.float32
BF16 = jnp.bfloat16
NEG_BIG = -1e30
LOG2_E = 1.4426950408889634

assert ML_DQK == LANES and DA_DV == LANES and ML_BLOCK == 2 * LANES


def _dot(a, b):
    return jnp.dot(a, b, preferred_element_type=F32)


def _dot_nt(a, b):
    return lax.dot_general(a, b, (((1,), (1,)), ((), ())), preferred_element_type=F32)


def _wide(x, width):
    return jnp.concatenate([x] * (width // LANES), axis=1)


def _layer_norm(y, g, b):
    mu = jnp.mean(y, axis=-1, keepdims=True)
    yc = y - mu
    var = jnp.mean(yc * yc, axis=-1, keepdims=True)
    return yc * lax.rsqrt(var + LN_EPS) * g + b


def _resident(shape):
    return pl.BlockSpec(shape, lambda *_: (0,) * len(shape), pipeline_mode=pl.Buffered(1))


def _resident_layer(stacked, layer):
    n = stacked.ndim - 1
    return pl.BlockSpec((None,) + stacked.shape[1:], lambda *_: (layer,) + (0,) * n,
                        pipeline_mode=pl.Buffered(1))


def _params(n_axes):
    return pltpu.CompilerParams(dimension_semantics=("parallel",) * n_axes,
                                vmem_limit_bytes=VMEM_LIMIT)


def _ada_kernel(c_ref, w_ref, b_ref, o_ref):
    c = c_ref[...]
    a = (c * jax.nn.sigmoid(c)).astype(BF16)
    o_ref[0] = _dot(a, w_ref[0].astype(BF16)) + b_ref[0]


def _ada(cc, w_ada, b_ada):
    depth, d, n = w_ada.shape
    tn = d
    return pl.pallas_call(
        _ada_kernel,
        grid=(depth, n // tn),
        in_specs=[pl.BlockSpec(cc.shape, lambda l, j: (0, 0)),
                  pl.BlockSpec((1, d, tn), lambda l, j: (l, 0, j)),
                  pl.BlockSpec((1, 1, tn), lambda l, j: (l, 0, j))],
        out_specs=pl.BlockSpec((1, cc.shape[0], tn), lambda l, j: (l, 0, j)),
        out_shape=jax.ShapeDtypeStruct((depth, cc.shape[0], n), F32),
        compiler_params=_params(2),
        name="ada",
    )(cc, w_ada, b_ada.reshape(depth, 1, n))


def _two_source_specs(tm, width, lat_tiles):
    return [pl.BlockSpec((tm, width), lambda i: (jnp.minimum(i, lat_tiles - 1), 0)),
            pl.BlockSpec((tm, width), lambda i: (jnp.maximum(i - lat_tiles, 0), 0))]


def _ffn_kernel(*refs, mod_base, alpha, lat_tiles, cast_extra):
    if lat_tiles is None:
        h_ref, hc_ref = refs[0], None
        refs = refs[1:]
    else:
        h_ref, hc_ref = refs[:2]
        refs = refs[2:]
    if cast_extra:
        mod_ref, wup_ref, wdn_ref, lng_ref, lnb_ref, extra_ref, o_ref, extra_o, g_scr = refs
        extra_o[...] = extra_ref[...].astype(BF16)
    else:
        mod_ref, wup_ref, wdn_ref, lng_ref, lnb_ref, o_ref, g_scr = refs
    h = h_ref[...] if hc_ref is None else jnp.where(pl.program_id(0) < lat_tiles, h_ref[...], hc_ref[...])
    f = wdn_ref.shape[0]
    shift = mod_ref[0, mod_base:mod_base + 1, :]
    scale = mod_ref[0, mod_base + 1:mod_base + 2, :]
    gate = mod_ref[0, mod_base + 2:mod_base + 3, :]
    u = (h * (1.0 + scale) + shift).astype(BF16)
    for c in range(f // FFN_COLS):
        lo, hi = c * FFN_COLS, (c + 1) * FFN_COLS
        a = _dot(u, wup_ref[:, lo:hi].astype(BF16))
        v = _dot(u, wup_ref[:, f + lo:f + hi].astype(BF16))
        g_scr[:, lo:hi] = (a * jax.nn.sigmoid(a) * v).astype(BF16)
    half = h.shape[0] // 2
    for r in range(2):
        rows = slice(r * half, (r + 1) * half)
        d = _dot(g_scr[rows, :], wdn_ref[...].astype(BF16))
        o_ref[rows, :] = _layer_norm(alpha * h[rows, :] + (0.5 * gate) * d, lng_ref[...], lnb_ref[...])


def _ffn(h, mods3, w_up, w_dn, ln_g, ln_b, *, layer, mod_base, n_rows, seq, batch, alpha, h_ctx=None,
         cast_stack=None):
    d = h.shape[1]
    f = w_dn.shape[1]
    tm = min(FFN_TILE, seq)
    steps = n_rows // tm
    lat_tiles = None if h_ctx is None else h.shape[0] // tm
    kernel = functools.partial(_ffn_kernel, mod_base=mod_base, alpha=alpha, lat_tiles=lat_tiles,
                               cast_extra=cast_stack is not None)
    if h_ctx is None:
        sources, source_specs = [h], [pl.BlockSpec((tm, d), lambda i: (i, 0))]
    else:
        sources, source_specs = [h, h_ctx], _two_source_specs(tm, d, lat_tiles)
    extra, extra_specs = [], []
    out_specs = [pl.BlockSpec((tm, d), lambda i: (i, 0))]
    out_shape = [jax.ShapeDtypeStruct((n_rows, d), F32)]
    if cast_stack is not None:
        n_extra = cast_stack.shape[1]
        rows_blk = -(-pl.cdiv(n_extra, steps) // BF16_SUBLANES) * BF16_SUBLANES
        last_blk = pl.cdiv(n_extra, rows_blk) - 1
        extra = [cast_stack]
        extra_specs = [pl.BlockSpec((None, rows_blk, d), lambda i: (layer, jnp.minimum(i, last_blk), 0))]
        out_specs.append(pl.BlockSpec((rows_blk, d), lambda i: (jnp.minimum(i, last_blk), 0)))
        out_shape.append(jax.ShapeDtypeStruct((n_extra, d), BF16))
    out = pl.pallas_call(
        kernel,
        grid=(steps,),
        in_specs=source_specs + [
                  pl.BlockSpec((1, N_MOD, d), lambda i: (jnp.minimum(i * tm // seq, batch), 0, 0)),
                  _resident_layer(w_up, layer),
                  _resident_layer(w_dn, layer),
                  pl.BlockSpec((1, d), lambda i: (0, 0)),
                  pl.BlockSpec((1, d), lambda i: (0, 0))] + extra_specs,
        out_specs=out_specs,
        out_shape=out_shape,
        scratch_shapes=[pltpu.VMEM((tm, f), BF16)],
        compiler_params=pltpu.CompilerParams(
            dimension_semantics=("arbitrary" if cast_stack is not None else "parallel",),
            vmem_limit_bytes=VMEM_LIMIT),
        name="ffn",
    )(*sources, mods3, w_up, w_dn, ln_g.reshape(1, d), ln_b.reshape(1, d), *extra)
    return out if cast_stack is not None else out[0]


GATE_COL0 = 3 * SC_WIDTH + 2 * ML_QK_W + 2 * ML_V_W


def _proj_cols(d):
    parts = ((("sb", SC_WIDTH), ("sc", SC_WIDTH), ("sx", SC_WIDTH),
              ("mq", ML_QK_W), ("mk", ML_QK_W), ("mv", ML_V_W), ("mo", ML_V_W)),
             (("dq", DA_QK_W), ("dk", DA_QK_W), ("dv", DA_V_W), ("gs", d), ("gm", d), ("gd", d)))
    cols, widths = {}, []
    for part, sizes in enumerate(parts):
        off = 0
        for name, w in sizes:
            cols[name] = (part, off, off + w)
            off += w
        widths.append(off)
    return cols, widths


def _mixer_in_kernel(h_ref, mod_ref, w_ref, bh_ref, bt_ref, bg_ref, cos_ref, sin_ref,
                     sb_o, p_o, mq_o, mk_o, mv_o, smo_o, mg_o, dq_o, dk_o, dv_o, gs_o, gm_o, gd_o):
    d = h_ref.shape[1]
    tm = h_ref.shape[0]
    cols, _ = _proj_cols(d)
    h = h_ref[...]
    shift = mod_ref[0, 3:4, :]
    scale = mod_ref[0, 4:5, :]
    u = (h * (1.0 + scale) + shift).astype(BF16)

    def proj(name):
        part, lo, hi = cols[name]
        b_ref = (bh_ref, bt_ref)[part]
        row0 = 0 if part == 0 else GATE_COL0 + N_GATES
        return _dot_nt(u, w_ref[row0 + lo:row0 + hi, :]) + b_ref[:, lo:hi]

    sb_o[...] = proj("sb").astype(BF16)
    p_o[...] = (proj("sc") * proj("sx")).astype(BF16)
    mq_o[...] = proj("mq").astype(BF16)
    mk_o[...] = (proj("mk") * (ML_DQK ** -0.5)).astype(BF16)
    mv_o[...] = proj("mv").astype(BF16)
    smo_o[...] = jax.nn.sigmoid(proj("mo")).astype(BF16)
    mg_o[...] = _dot_nt(u, w_ref[GATE_COL0:GATE_COL0 + N_GATES, :]) + bg_ref[...]

    cos_t = cos_ref[...]
    sin_t = sin_ref[...]
    lane = lax.broadcasted_iota(jnp.int32, cos_t.shape, 1)
    first_half = (lane % DA_HALF) < (DA_HALF // 2)

    def rope_store(z, out_ref, mult):
        for k in range(z.shape[1] // LANES):
            x = z[:, k * LANES:(k + 1) * LANES]
            partner = jnp.where(first_half,
                                pltpu.roll(x, LANES - DA_HALF // 2, 1),
                                pltpu.roll(x, DA_HALF // 2, 1))
            out_ref[:, k * LANES:(k + 1) * LANES] = ((x * cos_t + partner * sin_t) * mult).astype(BF16)

    rope_store(proj("dq"), dq_o, (DA_HALF ** -0.5) * LOG2_E)
    rope_store(proj("dk"), dk_o, 1.0)
    dv = proj("dv").astype(BF16)
    ones = jnp.ones((tm, DA_DV), BF16)
    for k in range(DA_HEADS):
        dv_o[:, 2 * k * DA_DV:(2 * k + 1) * DA_DV] = dv[:, k * DA_DV:(k + 1) * DA_DV]
        dv_o[:, (2 * k + 1) * DA_DV:(2 * k + 2) * DA_DV] = ones
    gs_o[...] = jax.nn.sigmoid(proj("gs")).astype(BF16)
    gm_o[...] = jax.nn.sigmoid(proj("gm")).astype(BF16)
    gd_o[...] = jax.nn.sigmoid(proj("gd")).astype(BF16)


def _mixer_in(h, mods3, w_t, b_head, b_tail, b_gate, cos_t, sin_t, *, seq, batch, n_lat):
    rows, d = h.shape
    _, (n_head, n_tail) = _proj_cols(d)
    tm = min(MIXER_IN_TILE, seq)
    lat_tiles = n_lat // tm
    rope_blocks = seq // tm

    def row_spec(width):
        return pl.BlockSpec((tm, width), lambda i: (i, 0))

    def rope_map(i):
        return (jnp.where(i < lat_tiles, i % rope_blocks, rope_blocks), 0)

    widths = (SC_WIDTH, SC_WIDTH, ML_QK_W, ML_QK_W, ML_V_W, ML_V_W, N_GATES,
              DA_QK_W, DA_QK_W, 2 * DA_V_W, d, d, d)
    dtypes = (BF16,) * 6 + (F32,) + (BF16,) * 6
    return pl.pallas_call(
        _mixer_in_kernel,
        grid=(rows // tm,),
        in_specs=[row_spec(d),
                  pl.BlockSpec((1, N_MOD, d), lambda i: (jnp.minimum(i * tm // seq, batch), 0, 0)),
                  _resident(w_t.shape),
                  _resident((1, n_head)),
                  _resident((1, n_tail)),
                  _resident((1, N_GATES)),
                  pl.BlockSpec((tm, LANES), rope_map),
                  pl.BlockSpec((tm, LANES), rope_map)],
        out_specs=[row_spec(w) for w in widths],
        out_shape=[jax.ShapeDtypeStruct((rows, w), dt) for w, dt in zip(widths, dtypes)],
        compiler_params=_params(1),
        name="mixer_in",
    )(h, mods3, w_t, b_head, b_tail, b_gate, cos_t, sin_t)


def _log_sigmoid(x):
    return jnp.minimum(x, 0.0) - jnp.log(1.0 + jnp.exp(-jnp.abs(x)))


def _split3(x):
    pieces, rest = [], x
    for _ in range(3):
        piece = rest.astype(BF16)
        pieces.append(piece)
        rest = rest - piece.astype(F32)
    return pieces


def _mlstm_gate_kernel(g_ref, o_ref, o3_ref):
    blk = ML_BLOCK
    t_idx = lax.broadcasted_iota(jnp.int32, (blk, blk), 0)
    s_idx = lax.broadcasted_iota(jnp.int32, (blk, blk), 1)
    tril = (s_idx <= t_idx).astype(BF16)
    lane = lax.broadcasted_iota(jnp.int32, (blk, N_GATES), 1)
    src = lax.broadcasted_iota(jnp.int32, (N_GATES, LANES), 0)
    dst = lax.broadcasted_iota(jnp.int32, (N_GATES, LANES), 1)
    places = [(dst == src + term * N_GATES).astype(BF16) for term in range(3)]
    for j in range(g_ref.shape[0] // blk):
        sl = pl.ds(j * blk, blk)
        g = g_ref[sl, :]
        ls = _log_sigmoid(g)
        prefix = functools.reduce(jnp.add, [_dot(tril, piece) for piece in _split3(ls)])
        suffix = prefix[blk - 1:blk, :] - prefix + ls
        out = jnp.where(lane < 2 * ML_HEADS, g, jnp.where(lane < 3 * ML_HEADS, prefix, suffix)) * LOG2_E
        o_ref[sl, :] = out
        placed = functools.reduce(jnp.add, [_dot(piece, place) for piece, place in zip(_split3(out), places)])
        o3_ref[sl, :] = placed.astype(BF16)


def _mlstm_gates(mg):
    rows = mg.shape[0]
    step = math.gcd(rows, 8 * ML_BLOCK)
    return pl.pallas_call(
        _mlstm_gate_kernel,
        grid=(rows // step,),
        in_specs=[pl.BlockSpec((step, N_GATES), lambda i: (i, 0))],
        out_specs=[pl.BlockSpec((step, N_GATES), lambda i: (i, 0)),
                   pl.BlockSpec((step, LANES), lambda i: (i, 0))],
        out_shape=[jax.ShapeDtypeStruct((rows, N_GATES), F32),
                   jax.ShapeDtypeStruct((rows, LANES), BF16)],
        compiler_params=_params(1),
        name="mlstm_gates",
    )(mg)


ML_HEADS_PER_STEP = 4
N_ROWS_PAD = BF16_SUBLANES
ST_ROWS = 2 * N_ROWS_PAD + 2 * ML_DV


def _mlstm_scan_kernel(*refs, need_ctx, heads_per_step):
    ins, rest = refs[:12], refs[12:]
    if need_ctx:
        hl_ref, hc_ref, st_in, m_in = rest
    else:
        hl_ref, st_in, m_in = rest
        hc_ref = None
    ql, qc, kl, kc, ktl, ktc, vtl, vtc, g3l, g3c, grl, grc = ins
    for hh in range(heads_per_step):
        qk_cols = pl.ds(hh * ML_DQK, ML_DQK)
        v_cols = pl.ds(hh * ML_DV, ML_DV)
        _mlstm_head(pl.program_id(1) * heads_per_step + hh,
                    ql.at[:, qk_cols], qc.at[:, qk_cols], kl.at[:, qk_cols], kc.at[:, qk_cols],
                    ktl.at[:, qk_cols, :], ktc.at[:, qk_cols, :], vtl.at[:, v_cols, :], vtc.at[:, v_cols, :],
                    g3l, g3c, grl, grc, hl_ref.at[:, v_cols],
                    None if hc_ref is None else hc_ref.at[:, v_cols], st_in.at[hh], m_in.at[hh])


def _mlstm_head(head, ql_ref, qc_ref, kl_ref, kc_ref, ktl_ref, ktc_ref, vtl_ref, vtc_ref,
                g3l_ref, g3c_ref, grl_ref, grc_ref, hl_ref, hc_ref, st_in, m_in):
    need_ctx = hc_ref is not None
    blk = ML_BLOCK
    n_lat_blk = ql_ref.shape[0] // blk
    n_ctx_blk = qc_ref.shape[0] // blk

    s_idx = lax.broadcasted_iota(jnp.int32, (blk, blk), 0)
    t_idx = lax.broadcasted_iota(jnp.int32, (blk, blk), 1)
    lane_t = lax.broadcasted_iota(jnp.int32, (1, blk), 1)
    ones_n = jnp.ones((N_ROWS_PAD, blk), BF16)
    sel_src = lax.broadcasted_iota(jnp.int32, (LANES, 2 * LANES), 0)
    sel_dst = lax.broadcasted_iota(jnp.int32, (LANES, 2 * LANES), 1)
    sel_bwd = sel_dst >= LANES
    col = sel_src % N_GATES
    in_terms = sel_src < 3 * N_GATES
    plus = jnp.logical_and(col == head + jnp.where(sel_bwd, ML_HEADS, 0), in_terms)
    minus = jnp.logical_and(col == head + jnp.where(sel_bwd, 3 * ML_HEADS, 2 * ML_HEADS), in_terms)
    sel = (plus.astype(F32) - minus.astype(F32)).astype(BF16)

    def gate_rows(gr_ref, j, bwd):
        i_idx = head + (ML_HEADS if bwd else 0)
        f_idx = head + (3 * ML_HEADS if bwd else 2 * ML_HEADS)
        return gr_ref[j, pl.ds(f_idx, 1), :], gr_ref[j, pl.ds(i_idx, 1), :]

    def scan_states():
        ctx_blocks = [(grc_ref, ktc_ref, vtc_ref, j, j) for j in range(n_ctx_blk)]
        lat_blocks = [(grl_ref, ktl_ref, vtl_ref, j, n_ctx_blk + j) for j in range(n_lat_blk)]
        orders = (ctx_blocks + lat_blocks, ctx_blocks[::-1] + lat_blocks[::-1])
        m_prev = [jnp.zeros((1, 1), F32)] * 2
        ct_run = [jnp.zeros((ML_DV, ML_DQK), F32)] * 2
        n_run = [jnp.zeros((N_ROWS_PAD, ML_DQK), F32)] * 2
        for step in range(len(orders[0])):
            for d, bwd in enumerate((False, True)):
                gr_ref, kt_ref, vt_ref, j, slot = orders[d][step]
                c_row0 = 2 * N_ROWS_PAD + d * ML_DV
                f_row, i_row = gate_rows(gr_ref, j, bwd)
                b_end = jnp.sum(jnp.where(lane_t == (0 if bwd else blk - 1), f_row, 0.0), axis=1,
                                keepdims=True)
                g_log = b_end + (i_row - f_row)
                m_new = jnp.maximum(b_end + m_prev[d], jnp.max(g_log, axis=1, keepdims=True))
                a_prev = jnp.exp2(b_end + m_prev[d] - m_new)
                kw = (kt_ref[j].astype(F32) * jnp.exp2(g_log - m_new)).astype(BF16)
                st_in[slot, d * N_ROWS_PAD:(d + 1) * N_ROWS_PAD, :] = n_run[d].astype(BF16)
                st_in[slot, c_row0:c_row0 + ML_DV, :] = ct_run[d].astype(BF16)
                m_in[slot, :, d * LANES:(d + 1) * LANES] = jnp.broadcast_to(m_prev[d], (8, LANES))
                ct_run[d] = a_prev * ct_run[d] + _dot_nt(vt_ref[j], kw)
                n_run[d] = a_prev * n_run[d] + _dot_nt(ones_n, kw)
                m_prev[d] = m_new

    def outputs(q, k, vt, g3, rows, slot):
        qk_t = _dot_nt(k, q)
        state = _dot_nt(st_in[slot], q)
        r_both = _dot(g3, sel)
        m_prev_both = m_in[slot][0:1, :]
        scaled, inter = [], []
        for d, bwd in enumerate((False, True)):
            f_row, _ = rows[d]
            r_rep = r_both[:, d * LANES:(d + 1) * LANES]
            mask = (s_idx >= t_idx) if bwd else (s_idx <= t_idx)
            d_log = jnp.where(mask, _wide(r_rep, blk) + f_row, NEG_BIG)
            m_loc = jnp.max(d_log, axis=0, keepdims=True)
            s = qk_t * jnp.exp2(d_log - m_loc)
            den_loc = jnp.sum(s, axis=0, keepdims=True)
            m_inter = f_row + m_prev_both[:, d * LANES:d * LANES + 1]
            m_t = jnp.maximum(m_inter, m_loc)
            w_inter = jnp.exp2(m_inter - m_t)
            w_loc = jnp.exp2(m_loc - m_t)
            qn = state[d * N_ROWS_PAD:d * N_ROWS_PAD + 1, :]
            den = w_inter * qn + w_loc * den_loc
            inv = 1.0 / jnp.maximum(jnp.abs(den), jnp.exp2(-m_t))
            scaled.append(s * (w_loc * inv))
            c_row0 = 2 * N_ROWS_PAD + d * ML_DV
            inter.append(state[c_row0:c_row0 + ML_DV, :] * (w_inter * inv))
        return _dot(vt, (scaled[0] + scaled[1]).astype(BF16)) + inter[0] + inter[1]

    def lat_rows(j):
        return pl.ds(pl.multiple_of(j * blk, blk), blk)

    scan_states()

    if need_ctx:
        for j in range(n_ctx_blk):
            sl = pl.ds(j * blk, blk)
            rows = (gate_rows(grc_ref, j, False), gate_rows(grc_ref, j, True))
            out_t = outputs(qc_ref[sl, :], kc_ref[sl, :], vtc_ref[j], g3c_ref[sl, :], rows, j)
            hc_ref[sl, :] = out_t.T.astype(hc_ref.dtype)

    def out_body(j, carry):
        sl = lat_rows(j)
        rows = (gate_rows(grl_ref, j, False), gate_rows(grl_ref, j, True))
        out_t = outputs(ql_ref[sl, :], kl_ref[sl, :], vtl_ref[j], g3l_ref[sl, :], rows, n_ctx_blk + j)
        hl_ref[sl, :] = out_t.T.astype(hl_ref.dtype)
        return carry

    lax.fori_loop(0, n_lat_blk, out_body, 0, unroll=True)


def _mlstm_scan(mq, mk, mv, mg, *, seq, ctx_len, batch, n_lat, need_ctx):
    blk = ML_BLOCK
    rows = mq.shape[0]
    ctx0 = n_lat // ctx_len
    n_blk = (seq + ctx_len) // blk
    gates, gates3 = _mlstm_gates(mg)
    gates_rows = gates.T.reshape(N_GATES, rows // blk, blk).transpose(1, 0, 2)
    mk_t = mk.reshape(rows // blk, blk, ML_QK_W).transpose(0, 2, 1)
    mv_t = mv.reshape(rows // blk, blk, ML_V_W).transpose(0, 2, 1)

    hps = ML_HEADS_PER_STEP

    def lat(width):
        return pl.BlockSpec((seq, hps * width), lambda b, h: (b, h))

    def ctx(width):
        return pl.BlockSpec((ctx_len, hps * width), lambda b, h: (ctx0 + b, h))

    def lat_t(width):
        return pl.BlockSpec((seq // blk, hps * width, blk), lambda b, h: (b, h, 0))

    def ctx_t(width):
        return pl.BlockSpec((ctx_len // blk, hps * width, blk), lambda b, h: (ctx0 + b, h, 0))

    in_specs = [lat(ML_DQK), ctx(ML_DQK), lat(ML_DQK), ctx(ML_DQK),
                lat_t(ML_DQK), ctx_t(ML_DQK), lat_t(ML_DV), ctx_t(ML_DV),
                pl.BlockSpec((seq, LANES), lambda b, h: (b, 0)),
                pl.BlockSpec((ctx_len, LANES), lambda b, h: (ctx0 + b, 0)),
                pl.BlockSpec((seq // blk, N_GATES, blk), lambda b, h: (b, 0, 0)),
                pl.BlockSpec((ctx_len // blk, N_GATES, blk), lambda b, h: (ctx0 + b, 0, 0))]
    out_specs = [pl.BlockSpec((seq, hps * ML_DV), lambda b, h: (b, h))]
    out_shape = [jax.ShapeDtypeStruct((n_lat, ML_V_W), BF16)]
    if need_ctx:
        out_specs.append(pl.BlockSpec((ctx_len, hps * ML_DV), lambda b, h: (b, h)))
        out_shape.append(jax.ShapeDtypeStruct((batch * ctx_len, ML_V_W), BF16))
    scratch = [pltpu.VMEM((hps, n_blk, ST_ROWS, ML_DQK), BF16),
               pltpu.VMEM((hps, n_blk, 8, 2 * LANES), F32)]
    out = pl.pallas_call(
        functools.partial(_mlstm_scan_kernel, need_ctx=need_ctx, heads_per_step=hps),
        grid=(batch, ML_HEADS // hps),
        in_specs=in_specs,
        out_specs=out_specs,
        out_shape=out_shape,
        scratch_shapes=scratch,
        compiler_params=_params(2),
        name="mlstm",
    )(mq, mq, mk, mk, mk_t, mk_t, mv_t, mv_t, gates3, gates3, gates_rows, gates_rows)
    return out if need_ctx else (out[0], None)


def _diffattn_kernel(lam_ref, g_ref, q_ref, *refs, n_seg, lam_init):
    k_refs = refs[:n_seg]
    v_refs = refs[n_seg:2 * n_seg]
    o_ref = refs[2 * n_seg]
    lv = lam_ref[...]
    lam = (jnp.exp(jnp.sum(lv[0:1] * lv[1:2], axis=1, keepdims=True))
           - jnp.exp(jnp.sum(lv[2:3] * lv[3:4], axis=1, keepdims=True)) + lam_init)
    for hh in range(q_ref.shape[1] // LANES):
        q = q_ref[:, hh * LANES:(hh + 1) * LANES]
        ks = [k_ref[:, hh * LANES:(hh + 1) * LANES] for k_ref in k_refs]
        vs = [v_ref[:, 2 * hh * DA_DV:2 * (hh + 1) * DA_DV] for v_ref in v_refs]
        lane = lax.broadcasted_iota(jnp.int32, q.shape, 1)
        zero = jnp.zeros_like(q)
        halves = [jnp.where(lane < DA_HALF, q, zero), jnp.where(lane >= DA_HALF, q, zero)]
        scores = [[_dot_nt(qh, k) for k in ks] for qh in halves]
        maxes = [functools.reduce(jnp.maximum, [jnp.max(s, axis=1, keepdims=True) for s in ss])
                 for ss in scores]
        probs = [[jnp.exp2((s - m).astype(BF16)) for s in ss] for ss, m in zip(scores, maxes)]
        heads = []
        for ps in probs:
            acc = None
            for p, v in zip(ps, vs):
                part = _dot(p, v)
                acc = part if acc is None else acc + part
            heads.append(acc[:, :DA_DV] * (1.0 / acc[:, DA_DV:]))
        o = heads[0] - lam * heads[1]
        ms = jnp.mean(o * o, axis=1, keepdims=True)
        o_ref[:, hh * DA_DV:(hh + 1) * DA_DV] = (o * lax.rsqrt(ms + LN_EPS) * g_ref[...]
                                                 * (1.0 - lam_init)).astype(o_ref.dtype)


def _diffattn(dq, dk, dv, lam_vecs, da_g, *, q_row0, q_len, segments, batch, lam_init, heads_per_step=1):
    tq = min(DIFFATTN_TILE, q_len)
    nq = q_len // tq
    q0 = q_row0 // tq
    hps = heads_per_step

    def seg_spec(row0, length, width):
        return pl.BlockSpec((length, hps * width), lambda b, h, i: (row0 // length + b, h))

    return pl.pallas_call(
        functools.partial(_diffattn_kernel, n_seg=len(segments), lam_init=lam_init),
        grid=(batch, DA_HEADS // hps, nq),
        in_specs=[pl.BlockSpec(lam_vecs.shape, lambda b, h, i: (0, 0)),
                  pl.BlockSpec((1, DA_DV), lambda b, h, i: (0, 0)),
                  pl.BlockSpec((tq, hps * LANES), lambda b, h, i: (q0 + b * nq + i, h))]
                 + [seg_spec(r0, ln, LANES) for r0, ln in segments]
                 + [seg_spec(r0, ln, 2 * DA_DV) for r0, ln in segments],
        out_specs=pl.BlockSpec((tq, hps * DA_DV), lambda b, h, i: (b * nq + i, h)),
        out_shape=jax.ShapeDtypeStruct((batch * q_len, DA_V_W), BF16),
        compiler_params=_params(3),
        name="diffattn",
    )(lam_vecs, da_g.reshape(1, DA_DV), dq, *([dk] * len(segments)), *([dv] * len(segments)))


def _mixer_out_kernel(h_ref, mod_ref, sb_ref, p_ref, pprev_ref, pnext_ref, smo_ref, *refs,
                      seq, ctx_len, n_lat, alpha, two_source):
    tm = h_ref.shape[0]
    r0 = pl.program_id(0) * tm
    is_lat = r0 < n_lat
    if two_source:
        hml_l, hml_c, yda_l, yda_c = refs[:4]
        refs = refs[4:]
        hml = jnp.where(is_lat, hml_l[...], hml_c[...])
        yda = jnp.where(is_lat, yda_l[...], yda_c[...])
    else:
        hml, yda = refs[0][...], refs[1][...]
        refs = refs[2:]
    gs_ref, gm_ref, gd_ref, convw_ref, wsc_ref, wml_ref, wda_ref, wo_ref, lng_ref, lnb_ref, o_ref = refs
    pos0 = jnp.where(is_lat, r0 % seq, (r0 - n_lat) % ctx_len)
    seq_len = jnp.where(is_lat, seq, ctx_len)
    starts_seq = pos0 == 0
    ends_seq = (pos0 + tm) % seq_len == 0
    inner_starts = tuple(range(ctx_len, tm, ctx_len))
    slab_row = lax.broadcasted_iota(jnp.int32, (8, 1), 0)

    def put_row(x, r, new_row):
        s = r // 8 * 8
        slab = jnp.where(slab_row == r - s, new_row, x[s:s + 8, :])
        parts = ([x[:s, :]] if s else []) + [slab] + ([x[s + 8:, :]] if s + 8 < x.shape[0] else [])
        return jnp.concatenate(parts, axis=0)

    p = p_ref[...].astype(F32)
    prev_row = pprev_ref[...].astype(F32)[BF16_SUBLANES - 1:BF16_SUBLANES, :]
    next_row = pnext_ref[...].astype(F32)[0:1, :]
    p_before = put_row(pltpu.roll(p, 1, 0), 0, jnp.where(starts_seq, 0.0, prev_row))
    p_after = put_row(pltpu.roll(p, tm - 1, 0), tm - 1, jnp.where(ends_seq, 0.0, next_row))
    for b in inner_starts:
        p_before = put_row(p_before, b, jnp.where(is_lat, p[b - 1:b, :], 0.0))
        p_after = put_row(p_after, b - 1, jnp.where(is_lat, p[b:b + 1, :], 0.0))
    cw = convw_ref[...]
    conv = cw[0:1, :] * p_before + cw[1:2, :] * p + cw[2:3, :] * p_after
    y_sc = (sb_ref[...].astype(F32) * conv).astype(BF16)
    y_ml = smo_ref[...] * hml
    y = (gs_ref[...].astype(F32) * _dot(y_sc, wsc_ref[...].astype(BF16))
         + gm_ref[...].astype(F32) * _dot(y_ml, wml_ref[...].astype(BF16))
         + gd_ref[...].astype(F32) * _dot(yda, wda_ref[...].astype(BF16)))
    y = y.astype(BF16)
    half = tm // 2
    for r in range(2):
        rows = slice(r * half, (r + 1) * half)
        z = _dot(y[rows, :], wo_ref[...].astype(BF16))
        o_ref[rows, :] = _layer_norm(alpha * h_ref[rows, :] + mod_ref[0, 5:6, :] * z,
                                     lng_ref[...], lnb_ref[...])


def _mixer_out(h, mods3, sb, p, smo, hml, yda, gs, gm, gd, conv_w, w_sc, w_ml, w_da, w_o, ln_g, ln_b,
               *, layer, n_rows, seq, ctx_len, batch, n_lat, alpha, hml_ctx=None, yda_ctx=None):
    d = h.shape[1]
    tm = math.gcd(math.gcd(MIXER_OUT_TILE, seq), batch * ctx_len)
    halo = BF16_SUBLANES
    last_halo = p.shape[0] // halo - 1
    two_source = hml_ctx is not None

    def row_spec(width):
        return pl.BlockSpec((tm, width), lambda i: (i, 0))

    if two_source:
        branch = [hml, hml_ctx, yda, yda_ctx]
        branch_specs = (_two_source_specs(tm, ML_V_W, n_lat // tm) + _two_source_specs(tm, DA_V_W, n_lat // tm))
    else:
        branch = [hml, yda]
        branch_specs = [row_spec(ML_V_W), row_spec(DA_V_W)]
    kernel = functools.partial(_mixer_out_kernel, seq=seq, ctx_len=ctx_len, n_lat=n_lat, alpha=alpha,
                               two_source=two_source)
    return pl.pallas_call(
        kernel,
        grid=(n_rows // tm,),
        in_specs=[row_spec(d),
                  pl.BlockSpec((1, N_MOD, d), lambda i: (jnp.minimum(i * tm // seq, batch), 0, 0)),
                  row_spec(SC_WIDTH), row_spec(SC_WIDTH),
                  pl.BlockSpec((halo, SC_WIDTH), lambda i: (jnp.maximum(i * (tm // halo) - 1, 0), 0)),
                  pl.BlockSpec((halo, SC_WIDTH),
                               lambda i: (jnp.minimum((i + 1) * (tm // halo), last_halo), 0)),
                  row_spec(ML_V_W)] + branch_specs + [
                  row_spec(d), row_spec(d), row_spec(d),
                  _resident((SC_KSIZE, SC_WIDTH)),
                  _resident_layer(w_sc, layer), _resident_layer(w_ml, layer), _resident_layer(w_da, layer),
                  _resident_layer(w_o, layer),
                  pl.BlockSpec((1, d), lambda i: (0, 0)),
                  pl.BlockSpec((1, d), lambda i: (0, 0))],
        out_specs=row_spec(d),
        out_shape=jax.ShapeDtypeStruct((n_rows, d), F32),
        compiler_params=_params(1),
        name="mixer_out",
    )(h, mods3, sb, p, p, p, smo, *branch, gs, gm, gd, conv_w, w_sc, w_ml, w_da, w_o,
      ln_g.reshape(1, d), ln_b.reshape(1, d))


def _rope_tables(seq, tile):
    n_freq = DA_HALF // 4
    t = jnp.arange(seq)
    row_ids = (t // GRID_W).astype(F32)
    col_ids = (t % GRID_W).astype(F32)
    inv = ROPE_BASE ** (-jnp.arange(n_freq, dtype=F32) / n_freq)
    ang = jnp.concatenate([row_ids[:, None] * inv, col_ids[:, None] * inv], axis=-1)
    cos, sin = jnp.cos(ang), jnp.sin(ang)
    reps = LANES // DA_HALF
    cos_t = jnp.tile(jnp.concatenate([cos, cos], axis=-1), (1, reps))
    sin_t = jnp.tile(jnp.concatenate([-sin, sin], axis=-1), (1, reps))
    cos_t = jnp.concatenate([cos_t, jnp.ones((tile, LANES), F32)], axis=0)
    sin_t = jnp.concatenate([sin_t, jnp.zeros((tile, LANES), F32)], axis=0)
    return cos_t, sin_t


def kernel(x, c, ctx, c_ctx, w_ada, b_ada, ln_g, ln_b, ffn1_up, ffn1_down, ffn2_up, ffn2_down, w_in, b_in,
           conv_w, w_sc, w_ml, w_da, w_o, lam_q1, lam_k1, lam_q2, lam_k2, da_norm_g):
    batch, seq, d = x.shape
    ctx_len = ctx.shape[1]
    depth = w_ada.shape[0]
    alpha = (2 * depth) ** 0.25
    n_lat = batch * seq
    n_ctx = batch * ctx_len

    n_cond = -(-(batch + 1) // 8) * 8
    cc = jnp.concatenate([c, c_ctx[None, :], jnp.zeros((n_cond - batch - 1, d), F32)], axis=0)
    mods = _ada(cc, w_ada, b_ada).reshape(depth, n_cond, N_MOD, d)

    cos_t, sin_t = _rope_tables(seq, min(MIXER_IN_TILE, seq))
    gate_lo = GATE_COL0
    gate_hi = gate_lo + N_GATES
    w_in_t = jnp.swapaxes(w_in, 1, 2)

    h, h_ctx = x.reshape(n_lat, d), ctx.reshape(n_ctx, d)
    for l in range(depth):
        last = l == depth - 1
        lam_init = 0.8 - 0.6 * math.exp(-0.3 * l)
        mods3 = mods[l]
        b_head = b_in[l, :gate_lo][None, :]
        b_tail = b_in[l, gate_hi:][None, :]
        b_gate = b_in[l, gate_lo:gate_hi][None, :]
        lam_vecs = jnp.stack([lam_q1[l], lam_k1[l], lam_q2[l], lam_k2[l]]).astype(F32)

        h, w_proj_t = _ffn(h, mods3, ffn1_up, ffn1_down, ln_g[l, 0], ln_b[l, 0], layer=l, mod_base=0,
                           n_rows=n_lat + n_ctx, seq=seq, batch=batch, alpha=alpha, h_ctx=h_ctx,
                           cast_stack=w_in_t)
        h_ctx = None

        (sb, p, mq, mk, mv, smo, mg, dq, dk, dv, gs, gm, gd) = _mixer_in(
            h, mods3, w_proj_t, b_head, b_tail, b_gate, cos_t, sin_t,
            seq=seq, batch=batch, n_lat=n_lat)

        hml, hml_ctx = _mlstm_scan(mq, mk, mv, mg, seq=seq, ctx_len=ctx_len, batch=batch, n_lat=n_lat,
                                   need_ctx=not last)
        yda = _diffattn(dq, dk, dv, lam_vecs, da_norm_g[l], q_row0=0, q_len=seq,
                        segments=((0, seq), (n_lat, ctx_len)), batch=batch, lam_init=lam_init)
        yda_ctx = None
        if not last:
            yda_ctx = _diffattn(dq, dk, dv, lam_vecs, da_norm_g[l], q_row0=n_lat, q_len=ctx_len,
                                segments=((n_lat, ctx_len),), batch=batch, lam_init=lam_init,
                                heads_per_step=DA_HEADS)

        n_rows = n_lat if last else n_lat + n_ctx
        h = _mixer_out(h, mods3, sb, p, smo, hml, yda, gs, gm, gd, conv_w[l], w_sc, w_ml, w_da, w_o,
                       ln_g[l, 1], ln_b[l, 1], hml_ctx=hml_ctx, yda_ctx=yda_ctx, layer=l,
                       n_rows=n_rows, seq=seq, ctx_len=ctx_len, batch=batch, n_lat=n_lat, alpha=alpha)
        h = _ffn(h, mods3, ffn2_up, ffn2_down, ln_g[l, 2], ln_b[l, 2], layer=l,
                 mod_base=6, n_rows=n_rows, seq=seq, batch=batch, alpha=alpha)
    return h[:n_lat].reshape(batch, seq, d)
```

```python
import functools
import math

import jax
import jax.numpy as jnp
from jax import lax
from jax.experimental import pallas as pl
from jax.experimental.pallas import tpu as pltpu

GRID_W = 64
N_MOD = 9
SC_WIDTH = 512
SC_KSIZE = 3
ML_HEADS = 4
ML_DQK = 128
ML_DV = 256
DA_HEADS = 4
DA_HALF = 64
DA_DV = 2 * DA_HALF
ROPE_BASE = 10000.0
LN_EPS = 1e-5

ML_QK_W = ML_HEADS * ML_DQK
ML_V_W = ML_HEADS * ML_DV
DA_QK_W = DA_HEADS * 2 * DA_HALF
DA_V_W = DA_HEADS * DA_DV
N_GATES = 4 * ML_HEADS

ML_BLOCK = 256
LANES = 128
BF16_SUBLANES = 16
V7X_VMEM_BYTES = 64 * 1024 * 1024
VMEM_LIMIT = V7X_VMEM_BYTES - 8 * 1024 * 1024

FFN_TILE = 512
FFN_COLS = 256
MIXER_IN_TILE = 256
MIXER_OUT_TILE = 512
DIFFATTN_TILE = 1024
ADA_COLS = 3072

F32 = jnp.float32
BF16 = jnp.bfloat16
NEG_BIG = -1e30
LOG2_E = 1.4426950408889634

assert ML_DQK == LANES and DA_DV == LANES and ML_BLOCK == 2 * LANES


def _dot(a, b):
    return jnp.dot(a, b, preferred_element_type=F32)


def _dot_nt(a, b):
    return lax.dot_general(a, b, (((1,), (1,)), ((), ())), preferred_element_type=F32)


def _wide(x, width):
    return jnp.concatenate([x] * (width // LANES), axis=1)


def _layer_norm(y, g, b):
    mu = jnp.mean(y, axis=-1, keepdims=True)
    yc = y - mu
    var = jnp.mean(yc * yc, axis=-1, keepdims=True)
    return yc * lax.rsqrt(var + LN_EPS) * g + b


def _resident(shape):
    return pl.BlockSpec(shape, lambda *_: (0,) * len(shape), pipeline_mode=pl.Buffered(1))


def _resident_layer(stacked, layer):
    n = stacked.ndim - 1
    return pl.BlockSpec((None,) + stacked.shape[1:], lambda *_: (layer,) + (0,) * n,
                        pipeline_mode=pl.Buffered(1))


def _params(n_axes):
    return pltpu.CompilerParams(dimension_semantics=("parallel",) * n_axes,
                                vmem_limit_bytes=VMEM_LIMIT)


def _ada_kernel(c_ref, w_ref, b_ref, o_ref):
    c = c_ref[...]
    a = (c * jax.nn.sigmoid(c)).astype(BF16)
    o_ref[0] = _dot(a, w_ref[0].astype(BF16)) + b_ref[0]


def _ada(cc, w_ada, b_ada):
    depth, d, n = w_ada.shape
    tn = math.gcd(n, ADA_COLS)
    return pl.pallas_call(
        _ada_kernel,
        grid=(depth, n // tn),
        in_specs=[pl.BlockSpec(cc.shape, lambda l, j: (0, 0)),
                  pl.BlockSpec((1, d, tn), lambda l, j: (l, 0, j)),
                  pl.BlockSpec((1, 1, tn), lambda l, j: (l, 0, j))],
        out_specs=pl.BlockSpec((1, cc.shape[0], tn), lambda l, j: (l, 0, j)),
        out_shape=jax.ShapeDtypeStruct((depth, cc.shape[0], n), F32),
        compiler_params=_params(2),
        name="ada",
    )(cc, w_ada, b_ada.reshape(depth, 1, n))


def _two_source_specs(tm, width, lat_tiles):
    return [pl.BlockSpec((tm, width), lambda i: (jnp.minimum(i, lat_tiles - 1), 0)),
            pl.BlockSpec((tm, width), lambda i: (jnp.maximum(i - lat_tiles, 0), 0))]


def _ffn_kernel(*refs, mod_base, alpha, lat_tiles, cast_extra):
    if lat_tiles is None:
        h_ref, hc_ref = refs[0], None
        refs = refs[1:]
    else:
        h_ref, hc_ref = refs[:2]
        refs = refs[2:]
    if cast_extra:
        mod_ref, wup_ref, wdn_ref, lng_ref, lnb_ref, extra_ref, o_ref, extra_o, g_scr = refs
        extra_o[...] = extra_ref[...].astype(BF16)
    else:
        mod_ref, wup_ref, wdn_ref, lng_ref, lnb_ref, o_ref, g_scr = refs
    h = h_ref[...] if hc_ref is None else jnp.where(pl.program_id(0) < lat_tiles, h_ref[...], hc_ref[...])
    f = wdn_ref.shape[0]
    shift = mod_ref[0, mod_base:mod_base + 1, :]
    scale = mod_ref[0, mod_base + 1:mod_base + 2, :]
    gate = mod_ref[0, mod_base + 2:mod_base + 3, :]
    u = (h * (1.0 + scale) + shift).astype(BF16)
    for c in range(f // FFN_COLS):
        lo, hi = c * FFN_COLS, (c + 1) * FFN_COLS
        a = _dot(u, wup_ref[:, lo:hi].astype(BF16))
        v = _dot(u, wup_ref[:, f + lo:f + hi].astype(BF16))
        g_scr[:, lo:hi] = (a * jax.nn.sigmoid(a) * v).astype(BF16)
    half = h.shape[0] // 2
    for r in range(2):
        rows = slice(r * half, (r + 1) * half)
        d = _dot(g_scr[rows, :], wdn_ref[...].astype(BF16))
        o_ref[rows, :] = _layer_norm(alpha * h[rows, :] + (0.5 * gate) * d, lng_ref[...], lnb_ref[...])


def _ffn(h, mods3, w_up, w_dn, ln_g, ln_b, *, layer, mod_base, n_rows, seq, batch, alpha, h_ctx=None,
         cast_stack=None):
    d = h.shape[1]
    f = w_dn.shape[1]
    tm = min(FFN_TILE, seq)
    steps = n_rows // tm
    lat_tiles = None if h_ctx is None else h.shape[0] // tm
    kernel = functools.partial(_ffn_kernel, mod_base=mod_base, alpha=alpha, lat_tiles=lat_tiles,
                               cast_extra=cast_stack is not None)
    if h_ctx is None:
        sources, source_specs = [h], [pl.BlockSpec((tm, d), lambda i: (i, 0))]
    else:
        sources, source_specs = [h, h_ctx], _two_source_specs(tm, d, lat_tiles)
    extra, extra_specs = [], []
    out_specs = [pl.BlockSpec((tm, d), lambda i: (i, 0))]
    out_shape = [jax.ShapeDtypeStruct((n_rows, d), F32)]
    if cast_stack is not None:
        n_extra = cast_stack.shape[1]
        rows_blk = -(-pl.cdiv(n_extra, steps) // BF16_SUBLANES) * BF16_SUBLANES
        last_blk = pl.cdiv(n_extra, rows_blk) - 1
        extra = [cast_stack]
        extra_specs = [pl.BlockSpec((None, rows_blk, d), lambda i: (layer, jnp.minimum(i, last_blk), 0))]
        out_specs.append(pl.BlockSpec((rows_blk, d), lambda i: (jnp.minimum(i, last_blk), 0)))
        out_shape.append(jax.ShapeDtypeStruct((n_extra, d), BF16))
    out = pl.pallas_call(
        kernel,
        grid=(steps,),
        in_specs=source_specs + [
                  pl.BlockSpec((1, N_MOD, d), lambda i: (jnp.minimum(i * tm // seq, batch), 0, 0)),
                  _resident_layer(w_up, layer),
                  _resident_layer(w_dn, layer),
                  pl.BlockSpec((1, d), lambda i: (0, 0)),
                  pl.BlockSpec((1, d), lambda i: (0, 0))] + extra_specs,
        out_specs=out_specs,
        out_shape=out_shape,
        scratch_shapes=[pltpu.VMEM((tm, f), BF16)],
        compiler_params=pltpu.CompilerParams(
            dimension_semantics=("arbitrary" if cast_stack is not None else "parallel",),
            vmem_limit_bytes=VMEM_LIMIT),
        name="ffn",
    )(*sources, mods3, w_up, w_dn, ln_g.reshape(1, d), ln_b.reshape(1, d), *extra)
    return out if cast_stack is not None else out[0]


GATE_COL0 = 3 * SC_WIDTH + 2 * ML_QK_W + 2 * ML_V_W


def _proj_cols(d):
    parts = ((("sb", SC_WIDTH), ("sc", SC_WIDTH), ("sx", SC_WIDTH),
              ("mq", ML_QK_W), ("mk", ML_QK_W), ("mv", ML_V_W), ("mo", ML_V_W)),
             (("dq", DA_QK_W), ("dk", DA_QK_W), ("dv", DA_V_W), ("gs", d), ("gm", d), ("gd", d)))
    cols, widths = {}, []
    for part, sizes in enumerate(parts):
        off = 0
        for name, w in sizes:
            cols[name] = (part, off, off + w)
            off += w
        widths.append(off)
    return cols, widths


def _mixer_in_kernel(h_ref, mod_ref, w_ref, bh_ref, bt_ref, bg_ref, cos_ref, sin_ref,
                     sb_o, p_o, mq_o, mk_o, mv_o, smo_o, mg_o, dq_o, dk_o, dv_o, gs_o, gm_o, gd_o):
    d = h_ref.shape[1]
    tm = h_ref.shape[0]
    cols, _ = _proj_cols(d)
    h = h_ref[...]
    shift = mod_ref[0, 3:4, :]
    scale = mod_ref[0, 4:5, :]
    u = (h * (1.0 + scale) + shift).astype(BF16)

    def proj(name):
        part, lo, hi = cols[name]
        b_ref = (bh_ref, bt_ref)[part]
        row0 = 0 if part == 0 else GATE_COL0 + N_GATES
        return _dot_nt(u, w_ref[row0 + lo:row0 + hi, :]) + b_ref[:, lo:hi]

    sb_o[...] = proj("sb").astype(BF16)
    p_o[...] = (proj("sc") * proj("sx")).astype(BF16)
    mq_o[...] = proj("mq").astype(BF16)
    mk_o[...] = (proj("mk") * (ML_DQK ** -0.5)).astype(BF16)
    mv_o[...] = proj("mv").astype(BF16)
    smo_o[...] = jax.nn.sigmoid(proj("mo")).astype(BF16)
    mg_o[...] = _dot_nt(u, w_ref[GATE_COL0:GATE_COL0 + N_GATES, :]) + bg_ref[...]

    cos_t = cos_ref[...]
    sin_t = sin_ref[...]
    lane = lax.broadcasted_iota(jnp.int32, cos_t.shape, 1)
    first_half = (lane % DA_HALF) < (DA_HALF // 2)

    def rope_store(z, out_ref, mult):
        for k in range(z.shape[1] // LANES):
            x = z[:, k * LANES:(k + 1) * LANES]
            partner = jnp.where(first_half,
                                pltpu.roll(x, LANES - DA_HALF // 2, 1),
                                pltpu.roll(x, DA_HALF // 2, 1))
            out_ref[:, k * LANES:(k + 1) * LANES] = ((x * cos_t + partner * sin_t) * mult).astype(BF16)

    rope_store(proj("dq"), dq_o, (DA_HALF ** -0.5) * LOG2_E)
    rope_store(proj("dk"), dk_o, 1.0)
    dv = proj("dv").astype(BF16)
    ones = jnp.ones((tm, DA_DV), BF16)
    for k in range(DA_HEADS):
        dv_o[:, 2 * k * DA_DV:(2 * k + 1) * DA_DV] = dv[:, k * DA_DV:(k + 1) * DA_DV]
        dv_o[:, (2 * k + 1) * DA_DV:(2 * k + 2) * DA_DV] = ones
    gs_o[...] = jax.nn.sigmoid(proj("gs")).astype(BF16)
    gm_o[...] = jax.nn.sigmoid(proj("gm")).astype(BF16)
    gd_o[...] = jax.nn.sigmoid(proj("gd")).astype(BF16)


def _mixer_in(h, mods3, w_t, b_head, b_tail, b_gate, cos_t, sin_t, *, seq, batch, n_lat):
    rows, d = h.shape
    _, (n_head, n_tail) = _proj_cols(d)
    tm = min(MIXER_IN_TILE, seq)
    lat_tiles = n_lat // tm
    rope_blocks = seq // tm

    def row_spec(width):
        return pl.BlockSpec((tm, width), lambda i: (i, 0))

    def rope_map(i):
        return (jnp.where(i < lat_tiles, i % rope_blocks, rope_blocks), 0)

    widths = (SC_WIDTH, SC_WIDTH, ML_QK_W, ML_QK_W, ML_V_W, ML_V_W, N_GATES,
              DA_QK_W, DA_QK_W, 2 * DA_V_W, d, d, d)
    dtypes = (BF16,) * 6 + (F32,) + (BF16,) * 6
    return pl.pallas_call(
        _mixer_in_kernel,
        grid=(rows // tm,),
        in_specs=[row_spec(d),
                  pl.BlockSpec((1, N_MOD, d), lambda i: (jnp.minimum(i * tm // seq, batch), 0, 0)),
                  _resident(w_t.shape),
                  _resident((1, n_head)),
                  _resident((1, n_tail)),
                  _resident((1, N_GATES)),
                  pl.BlockSpec((tm, LANES), rope_map),
                  pl.BlockSpec((tm, LANES), rope_map)],
        out_specs=[row_spec(w) for w in widths],
        out_shape=[jax.ShapeDtypeStruct((rows, w), dt) for w, dt in zip(widths, dtypes)],
        compiler_params=_params(1),
        name="mixer_in",
    )(h, mods3, w_t, b_head, b_tail, b_gate, cos_t, sin_t)


def _log_sigmoid(x):
    return jnp.minimum(x, 0.0) - jnp.log(1.0 + jnp.exp(-jnp.abs(x)))


def _split3(x):
    pieces, rest = [], x
    for _ in range(3):
        piece = rest.astype(BF16)
        pieces.append(piece)
        rest = rest - piece.astype(F32)
    return pieces


def _mlstm_gate_kernel(g_ref, o_ref, o3_ref):
    blk = ML_BLOCK
    t_idx = lax.broadcasted_iota(jnp.int32, (blk, blk), 0)
    s_idx = lax.broadcasted_iota(jnp.int32, (blk, blk), 1)
    tril = (s_idx <= t_idx).astype(BF16)
    lane = lax.broadcasted_iota(jnp.int32, (blk, N_GATES), 1)
    src = lax.broadcasted_iota(jnp.int32, (N_GATES, LANES), 0)
    dst = lax.broadcasted_iota(jnp.int32, (N_GATES, LANES), 1)
    places = [(dst == src + term * N_GATES).astype(BF16) for term in range(3)]
    for j in range(g_ref.shape[0] // blk):
        sl = pl.ds(j * blk, blk)
        g = g_ref[sl, :]
        ls = _log_sigmoid(g)
        prefix = functools.reduce(jnp.add, [_dot(tril, piece) for piece in _split3(ls)])
        suffix = prefix[blk - 1:blk, :] - prefix + ls
        out = jnp.where(lane < 2 * ML_HEADS, g, jnp.where(lane < 3 * ML_HEADS, prefix, suffix)) * LOG2_E
        o_ref[sl, :] = out
        placed = functools.reduce(jnp.add, [_dot(piece, place) for piece, place in zip(_split3(out), places)])
        o3_ref[sl, :] = placed.astype(BF16)


def _mlstm_gates(mg):
    rows = mg.shape[0]
    step = math.gcd(rows, 8 * ML_BLOCK)
    return pl.pallas_call(
        _mlstm_gate_kernel,
        grid=(rows // step,),
        in_specs=[pl.BlockSpec((step, N_GATES), lambda i: (i, 0))],
        out_specs=[pl.BlockSpec((step, N_GATES), lambda i: (i, 0)),
                   pl.BlockSpec((step, LANES), lambda i: (i, 0))],
        out_shape=[jax.ShapeDtypeStruct((rows, N_GATES), F32),
                   jax.ShapeDtypeStruct((rows, LANES), BF16)],
        compiler_params=_params(1),
        name="mlstm_gates",
    )(mg)


ML_HEADS_PER_STEP = 4
N_ROWS_PAD = BF16_SUBLANES
ST_ROWS = 2 * N_ROWS_PAD + 2 * ML_DV


def _mlstm_scan_kernel(*refs, need_ctx, heads_per_step):
    ins, rest = refs[:12], refs[12:]
    if need_ctx:
        hl_ref, hc_ref, st_in, m_in = rest
    else:
        hl_ref, st_in, m_in = rest
        hc_ref = None
    ql, qc, kl, kc, ktl, ktc, vtl, vtc, g3l, g3c, grl, grc = ins
    for hh in range(heads_per_step):
        qk_cols = pl.ds(hh * ML_DQK, ML_DQK)
        v_cols = pl.ds(hh * ML_DV, ML_DV)
        _mlstm_head(pl.program_id(1) * heads_per_step + hh,
                    ql.at[:, qk_cols], qc.at[:, qk_cols], kl.at[:, qk_cols], kc.at[:, qk_cols],
                    ktl.at[:, qk_cols, :], ktc.at[:, qk_cols, :], vtl.at[:, v_cols, :], vtc.at[:, v_cols, :],
                    g3l, g3c, grl, grc, hl_ref.at[:, v_cols],
                    None if hc_ref is None else hc_ref.at[:, v_cols], st_in.at[hh], m_in.at[hh])


def _mlstm_head(head, ql_ref, qc_ref, kl_ref, kc_ref, ktl_ref, ktc_ref, vtl_ref, vtc_ref,
                g3l_ref, g3c_ref, grl_ref, grc_ref, hl_ref, hc_ref, st_in, m_in):
    need_ctx = hc_ref is not None
    blk = ML_BLOCK
    n_lat_blk = ql_ref.shape[0] // blk
    n_ctx_blk = qc_ref.shape[0] // blk

    s_idx = lax.broadcasted_iota(jnp.int32, (blk, blk), 0)
    t_idx = lax.broadcasted_iota(jnp.int32, (blk, blk), 1)
    lane_t = lax.broadcasted_iota(jnp.int32, (1, blk), 1)
    ones_n = jnp.ones((N_ROWS_PAD, blk), BF16)
    sel_src = lax.broadcasted_iota(jnp.int32, (LANES, 2 * LANES), 0)
    sel_dst = lax.broadcasted_iota(jnp.int32, (LANES, 2 * LANES), 1)
    sel_bwd = sel_dst >= LANES
    col = sel_src % N_GATES
    in_terms = sel_src < 3 * N_GATES
    plus = jnp.logical_and(col == head + jnp.where(sel_bwd, ML_HEADS, 0), in_terms)
    minus = jnp.logical_and(col == head + jnp.where(sel_bwd, 3 * ML_HEADS, 2 * ML_HEADS), in_terms)
    sel = (plus.astype(F32) - minus.astype(F32)).astype(BF16)

    def gate_rows(gr_ref, j, bwd):
        i_idx = head + (ML_HEADS if bwd else 0)
        f_idx = head + (3 * ML_HEADS if bwd else 2 * ML_HEADS)
        return gr_ref[j, pl.ds(f_idx, 1), :], gr_ref[j, pl.ds(i_idx, 1), :]

    def scan_states():
        ctx_blocks = [(grc_ref, ktc_ref, vtc_ref, j, j) for j in range(n_ctx_blk)]
        lat_blocks = [(grl_ref, ktl_ref, vtl_ref, j, n_ctx_blk + j) for j in range(n_lat_blk)]
        orders = (ctx_blocks + lat_blocks, ctx_blocks[::-1] + lat_blocks[::-1])
        m_prev = [jnp.zeros((1, 1), F32)] * 2
        ct_run = [jnp.zeros((ML_DV, ML_DQK), F32)] * 2
        n_run = [jnp.zeros((N_ROWS_PAD, ML_DQK), F32)] * 2
        for step in range(len(orders[0])):
            for d, bwd in enumerate((False, True)):
                gr_ref, kt_ref, vt_ref, j, slot = orders[d][step]
                c_row0 = 2 * N_ROWS_PAD + d * ML_DV
                f_row, i_row = gate_rows(gr_ref, j, bwd)
                b_end = jnp.sum(jnp.where(lane_t == (0 if bwd else blk - 1), f_row, 0.0), axis=1,
                                keepdims=True)
                g_log = b_end + (i_row - f_row)
                m_new = jnp.maximum(b_end + m_prev[d], jnp.max(g_log, axis=1, keepdims=True))
                a_prev = jnp.exp2(b_end + m_prev[d] - m_new)
                kw = (kt_ref[j].astype(F32) * jnp.exp2(g_log - m_new)).astype(BF16)
                st_in[slot, d * N_ROWS_PAD:(d + 1) * N_ROWS_PAD, :] = n_run[d].astype(BF16)
                st_in[slot, c_row0:c_row0 + ML_DV, :] = ct_run[d].astype(BF16)
                m_in[slot, :, d * LANES:(d + 1) * LANES] = jnp.broadcast_to(m_prev[d], (8, LANES))
                ct_run[d] = a_prev * ct_run[d] + _dot_nt(vt_ref[j], kw)
                n_run[d] = a_prev * n_run[d] + _dot_nt(ones_n, kw)
                m_prev[d] = m_new

    def outputs(q, k, vt, g3, rows, slot):
        qk_t = _dot_nt(k, q)
        state = _dot_nt(st_in[slot], q)
        r_both = _dot(g3, sel)
        m_prev_both = m_in[slot][0:1, :]
        scaled, inter = [], []
        for d, bwd in enumerate((False, True)):
            f_row, _ = rows[d]
            r_rep = r_both[:, d * LANES:(d + 1) * LANES]
            mask = (s_idx >= t_idx) if bwd else (s_idx <= t_idx)
            d_log = jnp.where(mask, _wide(r_rep, blk) + f_row, NEG_BIG)
            m_loc = jnp.max(d_log, axis=0, keepdims=True)
            s = qk_t * jnp.exp2(d_log - m_loc)
            den_loc = jnp.sum(s, axis=0, keepdims=True)
            m_inter = f_row + m_prev_both[:, d * LANES:d * LANES + 1]
            m_t = jnp.maximum(m_inter, m_loc)
            w_inter = jnp.exp2(m_inter - m_t)
            w_loc = jnp.exp2(m_loc - m_t)
            qn = state[d * N_ROWS_PAD:d * N_ROWS_PAD + 1, :]
            den = w_inter * qn + w_loc * den_loc
            inv = 1.0 / jnp.maximum(jnp.abs(den), jnp.exp2(-m_t))
            scaled.append(s * (w_loc * inv))
            c_row0 = 2 * N_ROWS_PAD + d * ML_DV
            inter.append(state[c_row0:c_row0 + ML_DV, :] * (w_inter * inv))
        return _dot(vt, (scaled[0] + scaled[1]).astype(BF16)) + inter[0] + inter[1]

    def lat_rows(j):
        return pl.ds(pl.multiple_of(j * blk, blk), blk)

    scan_states()

    if need_ctx:
        for j in range(n_ctx_blk):
            sl = pl.ds(j * blk, blk)
            rows = (gate_rows(grc_ref, j, False), gate_rows(grc_ref, j, True))
            out_t = outputs(qc_ref[sl, :], kc_ref[sl, :], vtc_ref[j], g3c_ref[sl, :], rows, j)
            hc_ref[sl, :] = out_t.T.astype(hc_ref.dtype)

    def out_body(j, carry):
        sl = lat_rows(j)
        rows = (gate_rows(grl_ref, j, False), gate_rows(grl_ref, j, True))
        out_t = outputs(ql_ref[sl, :], kl_ref[sl, :], vtl_ref[j], g3l_ref[sl, :], rows, n_ctx_blk + j)
        hl_ref[sl, :] = out_t.T.astype(hl_ref.dtype)
        return carry

    lax.fori_loop(0, n_lat_blk, out_body, 0, unroll=True)


def _mlstm_scan(mq, mk, mv, mg, *, seq, ctx_len, batch, n_lat, need_ctx):
    blk = ML_BLOCK
    rows = mq.shape[0]
    ctx0 = n_lat // ctx_len
    n_blk = (seq + ctx_len) // blk
    gates, gates3 = _mlstm_gates(mg)
    gates_rows = gates.T.reshape(N_GATES, rows // blk, blk).transpose(1, 0, 2)
    mk_t = mk.reshape(rows // blk, blk, ML_QK_W).transpose(0, 2, 1)
    mv_t = mv.reshape(rows // blk, blk, ML_V_W).transpose(0, 2, 1)

    hps = ML_HEADS_PER_STEP

    def lat(width):
        return pl.BlockSpec((seq, hps * width), lambda b, h: (b, h))

    def ctx(width):
        return pl.BlockSpec((ctx_len, hps * width), lambda b, h: (ctx0 + b, h))

    def lat_t(width):
        return pl.BlockSpec((seq // blk, hps * width, blk), lambda b, h: (b, h, 0))

    def ctx_t(width):
        return pl.BlockSpec((ctx_len // blk, hps * width, blk), lambda b, h: (ctx0 + b, h, 0))

    in_specs = [lat(ML_DQK), ctx(ML_DQK), lat(ML_DQK), ctx(ML_DQK),
                lat_t(ML_DQK), ctx_t(ML_DQK), lat_t(ML_DV), ctx_t(ML_DV),
                pl.BlockSpec((seq, LANES), lambda b, h: (b, 0)),
                pl.BlockSpec((ctx_len, LANES), lambda b, h: (ctx0 + b, 0)),
                pl.BlockSpec((seq // blk, N_GATES, blk), lambda b, h: (b, 0, 0)),
                pl.BlockSpec((ctx_len // blk, N_GATES, blk), lambda b, h: (ctx0 + b, 0, 0))]
    out_specs = [pl.BlockSpec((seq, hps * ML_DV), lambda b, h: (b, h))]
    out_shape = [jax.ShapeDtypeStruct((n_lat, ML_V_W), BF16)]
    if need_ctx:
        out_specs.append(pl.BlockSpec((ctx_len, hps * ML_DV), lambda b, h: (b, h)))
        out_shape.append(jax.ShapeDtypeStruct((batch * ctx_len, ML_V_W), BF16))
    scratch = [pltpu.VMEM((hps, n_blk, ST_ROWS, ML_DQK), BF16),
               pltpu.VMEM((hps, n_blk, 8, 2 * LANES), F32)]
    out = pl.pallas_call(
        functools.partial(_mlstm_scan_kernel, need_ctx=need_ctx, heads_per_step=hps),
        grid=(batch, ML_HEADS // hps),
        in_specs=in_specs,
        out_specs=out_specs,
        out_shape=out_shape,
        scratch_shapes=scratch,
        compiler_params=_params(2),
        name="mlstm",
    )(mq, mq, mk, mk, mk_t, mk_t, mv_t, mv_t, gates3, gates3, gates_rows, gates_rows)
    return out if need_ctx else (out[0], None)


def _diffattn_kernel(lam_ref, g_ref, q_ref, *refs, n_seg, lam_init):
    k_refs = refs[:n_seg]
    v_refs = refs[n_seg:2 * n_seg]
    o_ref = refs[2 * n_seg]
    lv = lam_ref[...]
    lam = (jnp.exp(jnp.sum(lv[0:1] * lv[1:2], axis=1, keepdims=True))
           - jnp.exp(jnp.sum(lv[2:3] * lv[3:4], axis=1, keepdims=True)) + lam_init)
    for hh in range(q_ref.shape[1] // LANES):
        q = q_ref[:, hh * LANES:(hh + 1) * LANES]
        ks = [k_ref[:, hh * LANES:(hh + 1) * LANES] for k_ref in k_refs]
        vs = [v_ref[:, 2 * hh * DA_DV:2 * (hh + 1) * DA_DV] for v_ref in v_refs]
        lane = lax.broadcasted_iota(jnp.int32, q.shape, 1)
        zero = jnp.zeros_like(q)
        halves = [jnp.where(lane < DA_HALF, q, zero), jnp.where(lane >= DA_HALF, q, zero)]
        scores = [[_dot_nt(qh, k) for k in ks] for qh in halves]
        maxes = [functools.reduce(jnp.maximum, [jnp.max(s, axis=1, keepdims=True) for s in ss])
                 for ss in scores]
        probs = [[jnp.exp2((s - m).astype(BF16)) for s in ss] for ss, m in zip(scores, maxes)]
        heads = []
        for ps in probs:
            acc = None
            for p, v in zip(ps, vs):
                part = _dot(p, v)
                acc = part if acc is None else acc + part
            heads.append(acc[:, :DA_DV] * (1.0 / acc[:, DA_DV:]))
        o = heads[0] - lam * heads[1]
        ms = jnp.mean(o * o, axis=1, keepdims=True)
        o_ref[:, hh * DA_DV:(hh + 1) * DA_DV] = (o * lax.rsqrt(ms + LN_EPS) * g_ref[...]
                                                 * (1.0 - lam_init)).astype(o_ref.dtype)


def _diffattn(dq, dk, dv, lam_vecs, da_g, *, q_row0, q_len, segments, batch, lam_init, heads_per_step=1):
    tq = min(DIFFATTN_TILE, q_len)
    nq = q_len // tq
    q0 = q_row0 // tq
    hps = heads_per_step

    def seg_spec(row0, length, width):
        return pl.BlockSpec((length, hps * width), lambda b, h, i: (row0 // length + b, h))

    return pl.pallas_call(
        functools.partial(_diffattn_kernel, n_seg=len(segments), lam_init=lam_init),
        grid=(batch, DA_HEADS // hps, nq),
        in_specs=[pl.BlockSpec(lam_vecs.shape, lambda b, h, i: (0, 0)),
                  pl.BlockSpec((1, DA_DV), lambda b, h, i: (0, 0)),
                  pl.BlockSpec((tq, hps * LANES), lambda b, h, i: (q0 + b * nq + i, h))]
                 + [seg_spec(r0, ln, LANES) for r0, ln in segments]
                 + [seg_spec(r0, ln, 2 * DA_DV) for r0, ln in segments],
        out_specs=pl.BlockSpec((tq, hps * DA_DV), lambda b, h, i: (b * nq + i, h)),
        out_shape=jax.ShapeDtypeStruct((batch * q_len, DA_V_W), BF16),
        compiler_params=_params(3),
        name="diffattn",
    )(lam_vecs, da_g.reshape(1, DA_DV), dq, *([dk] * len(segments)), *([dv] * len(segments)))


def _mixer_out_kernel(h_ref, mod_ref, sb_ref, p_ref, pprev_ref, pnext_ref, smo_ref, *refs,
                      seq, ctx_len, n_lat, alpha, two_source):
    tm = h_ref.shape[0]
    r0 = pl.program_id(0) * tm
    is_lat = r0 < n_lat
    if two_source:
        hml_l, hml_c, yda_l, yda_c = refs[:4]
        refs = refs[4:]
        hml = jnp.where(is_lat, hml_l[...], hml_c[...])
        yda = jnp.where(is_lat, yda_l[...], yda_c[...])
    else:
        hml, yda = refs[0][...], refs[1][...]
        refs = refs[2:]
    gs_ref, gm_ref, gd_ref, convw_ref, wsc_ref, wml_ref, wda_ref, wo_ref, lng_ref, lnb_ref, o_ref = refs
    pos0 = jnp.where(is_lat, r0 % seq, (r0 - n_lat) % ctx_len)
    seq_len = jnp.where(is_lat, seq, ctx_len)
    starts_seq = pos0 == 0
    ends_seq = (pos0 + tm) % seq_len == 0
    inner_starts = tuple(range(ctx_len, tm, ctx_len))
    slab_row = lax.broadcasted_iota(jnp.int32, (8, 1), 0)

    def put_row(x, r, new_row):
        s = r // 8 * 8
        slab = jnp.where(slab_row == r - s, new_row, x[s:s + 8, :])
        parts = ([x[:s, :]] if s else []) + [slab] + ([x[s + 8:, :]] if s + 8 < x.shape[0] else [])
        return jnp.concatenate(parts, axis=0)

    p = p_ref[...].astype(F32)
    prev_row = pprev_ref[...].astype(F32)[BF16_SUBLANES - 1:BF16_SUBLANES, :]
    next_row = pnext_ref[...].astype(F32)[0:1, :]
    p_before = put_row(pltpu.roll(p, 1, 0), 0, jnp.where(starts_seq, 0.0, prev_row))
    p_after = put_row(pltpu.roll(p, tm - 1, 0), tm - 1, jnp.where(ends_seq, 0.0, next_row))
    for b in inner_starts:
        p_before = put_row(p_before, b, jnp.where(is_lat, p[b - 1:b, :], 0.0))
        p_after = put_row(p_after, b - 1, jnp.where(is_lat, p[b:b + 1, :], 0.0))
    cw = convw_ref[...]
    conv = cw[0:1, :] * p_before + cw[1:2, :] * p + cw[2:3, :] * p_after
    y_sc = (sb_ref[...].astype(F32) * conv).astype(BF16)
    y_ml = smo_ref[...] * hml
    y = (gs_ref[...].astype(F32) * _dot(y_sc, wsc_ref[...].astype(BF16))
         + gm_ref[...].astype(F32) * _dot(y_ml, wml_ref[...].astype(BF16))
         + gd_ref[...].astype(F32) * _dot(yda, wda_ref[...].astype(BF16)))
    y = y.astype(BF16)
    half = tm // 2
    for r in range(2):
        rows = slice(r * half, (r + 1) * half)
        z = _dot(y[rows, :], wo_ref[...].astype(BF16))
        o_ref[rows, :] = _layer_norm(alpha * h_ref[rows, :] + mod_ref[0, 5:6, :] * z,
                                     lng_ref[...], lnb_ref[...])


def _mixer_out(h, mods3, sb, p, smo, hml, yda, gs, gm, gd, conv_w, w_sc, w_ml, w_da, w_o, ln_g, ln_b,
               *, layer, n_rows, seq, ctx_len, batch, n_lat, alpha, hml_ctx=None, yda_ctx=None):
    d = h.shape[1]
    tm = math.gcd(math.gcd(MIXER_OUT_TILE, seq), batch * ctx_len)
    halo = BF16_SUBLANES
    last_halo = p.shape[0] // halo - 1
    two_source = hml_ctx is not None

    def row_spec(width):
        return pl.BlockSpec((tm, width), lambda i: (i, 0))

    if two_source:
        branch = [hml, hml_ctx, yda, yda_ctx]
        branch_specs = (_two_source_specs(tm, ML_V_W, n_lat // tm) + _two_source_specs(tm, DA_V_W, n_lat // tm))
    else:
        branch = [hml, yda]
        branch_specs = [row_spec(ML_V_W), row_spec(DA_V_W)]
    kernel = functools.partial(_mixer_out_kernel, seq=seq, ctx_len=ctx_len, n_lat=n_lat, alpha=alpha,
                               two_source=two_source)
    return pl.pallas_call(
        kernel,
        grid=(n_rows // tm,),
        in_specs=[row_spec(d),
                  pl.BlockSpec((1, N_MOD, d), lambda i: (jnp.minimum(i * tm // seq, batch), 0, 0)),
                  row_spec(SC_WIDTH), row_spec(SC_WIDTH),
                  pl.BlockSpec((halo, SC_WIDTH), lambda i: (jnp.maximum(i * (tm // halo) - 1, 0), 0)),
                  pl.BlockSpec((halo, SC_WIDTH),
                               lambda i: (jnp.minimum((i + 1) * (tm // halo), last_halo), 0)),
                  row_spec(ML_V_W)] + branch_specs + [
                  row_spec(d), row_spec(d), row_spec(d),
                  _resident((SC_KSIZE, SC_WIDTH)),
                  _resident_layer(w_sc, layer), _resident_layer(w_ml, layer), _resident_layer(w_da, layer),
                  _resident_layer(w_o, layer),
                  pl.BlockSpec((1, d), lambda i: (0, 0)),
                  pl.BlockSpec((1, d), lambda i: (0, 0))],
        out_specs=row_spec(d),
        out_shape=jax.ShapeDtypeStruct((n_rows, d), F32),
        compiler_params=_params(1),
        name="mixer_out",
    )(h, mods3, sb, p, p, p, smo, *branch, gs, gm, gd, conv_w, w_sc, w_ml, w_da, w_o,
      ln_g.reshape(1, d), ln_b.reshape(1, d))


def _rope_tables(seq, tile):
    n_freq = DA_HALF // 4
    t = jnp.arange(seq)
    row_ids = (t // GRID_W).astype(F32)
    col_ids = (t % GRID_W).astype(F32)
    inv = ROPE_BASE ** (-jnp.arange(n_freq, dtype=F32) / n_freq)
    ang = jnp.concatenate([row_ids[:, None] * inv, col_ids[:, None] * inv], axis=-1)
    cos, sin = jnp.cos(ang), jnp.sin(ang)
    reps = LANES // DA_HALF
    cos_t = jnp.tile(jnp.concatenate([cos, cos], axis=-1), (1, reps))
    sin_t = jnp.tile(jnp.concatenate([-sin, sin], axis=-1), (1, reps))
    cos_t = jnp.concatenate([cos_t, jnp.ones((tile, LANES), F32)], axis=0)
    sin_t = jnp.concatenate([sin_t, jnp.zeros((tile, LANES), F32)], axis=0)
    return cos_t, sin_t


def kernel(x, c, ctx, c_ctx, w_ada, b_ada, ln_g, ln_b, ffn1_up, ffn1_down, ffn2_up, ffn2_down, w_in, b_in,
           conv_w, w_sc, w_ml, w_da, w_o, lam_q1, lam_k1, lam_q2, lam_k2, da_norm_g):
    batch, seq, d = x.shape
    ctx_len = ctx.shape[1]
    depth = w_ada.shape[0]
    alpha = (2 * depth) ** 0.25
    n_lat = batch * seq
    n_ctx = batch * ctx_len

    n_cond = -(-(batch + 1) // 8) * 8
    cc = jnp.concatenate([c, c_ctx[None, :], jnp.zeros((n_cond - batch - 1, d), F32)], axis=0)
    mods = _ada(cc, w_ada, b_ada).reshape(depth, n_cond, N_MOD, d)

    cos_t, sin_t = _rope_tables(seq, min(MIXER_IN_TILE, seq))
    gate_lo = GATE_COL0
    gate_hi = gate_lo + N_GATES
    w_in_t = jnp.swapaxes(w_in, 1, 2)

    h, h_ctx = x.reshape(n_lat, d), ctx.reshape(n_ctx, d)
    for l in range(depth):
        last = l == depth - 1
        lam_init = 0.8 - 0.6 * math.exp(-0.3 * l)
        mods3 = mods[l]
        b_head = b_in[l, :gate_lo][None, :]
        b_tail = b_in[l, gate_hi:][None, :]
        b_gate = b_in[l, gate_lo:gate_hi][None, :]
        lam_vecs = jnp.stack([lam_q1[l], lam_k1[l], lam_q2[l], lam_k2[l]]).astype(F32)

        h, w_proj_t = _ffn(h, mods3, ffn1_up, ffn1_down, ln_g[l, 0], ln_b[l, 0], layer=l, mod_base=0,
                           n_rows=n_lat + n_ctx, seq=seq, batch=batch, alpha=alpha, h_ctx=h_ctx,
                           cast_stack=w_in_t)
        h_ctx = None

        (sb, p, mq, mk, mv, smo, mg, dq, dk, dv, gs, gm, gd) = _mixer_in(
            h, mods3, w_proj_t, b_head, b_tail, b_gate, cos_t, sin_t,
            seq=seq, batch=batch, n_lat=n_lat)

        hml, hml_ctx = _mlstm_scan(mq, mk, mv, mg, seq=seq, ctx_len=ctx_len, batch=batch, n_lat=n_lat,
                                   need_ctx=not last)
        yda = _diffattn(dq, dk, dv, lam_vecs, da_norm_g[l], q_row0=0, q_len=seq,
                        segments=((0, seq), (n_lat, ctx_len)), batch=batch, lam_init=lam_init)
        yda_ctx = None
        if not last:
            yda_ctx = _diffattn(dq, dk, dv, lam_vecs, da_norm_g[l], q_row0=n_lat, q_len=ctx_len,
                                segments=((n_lat, ctx_len),), batch=batch, lam_init=lam_init,
                                heads_per_step=DA_HEADS)

        n_rows = n_lat if last else n_lat + n_ctx
        h = _mixer_out(h, mods3, sb, p, smo, hml, yda, gs, gm, gd, conv_w[l], w_sc, w_ml, w_da, w_o,
                       ln_g[l, 1], ln_b[l, 1], hml_ctx=hml_ctx, yda_ctx=yda_ctx, layer=l,
                       n_rows=n_rows, seq=seq, ctx_len=ctx_len, batch=batch, n_lat=n_lat, alpha=alpha)
        h = _ffn(h, mods3, ffn2_up, ffn2_down, ln_g[l, 2], ln_b[l, 2], layer=l,
                 mod_base=6, n_rows=n_rows, seq=seq, batch=batch, alpha=alpha)
    return h[:n_lat].reshape(batch, seq, d)
```
